```python
import jax, jax.numpy as jnp
from jax import lax
import numpy as np

D_MODEL = 1024
BATCH = 16
SEQ = 2048
DEPTH = 2

GRID_W = 64
CTX_LEN = 256
BRANCH_W = D_MODEL // 4
D_MIX = 4 * BRANCH_W

MLA_V_DIM = 64
MLA_HEADS = BRANCH_W // MLA_V_DIM
MLA_NOPE_DIM = 64
MLA_ROPE_DIM = 32
MLA_QK_DIM = MLA_NOPE_DIM + MLA_ROPE_DIM
MLA_Q_RANK = 192
MLA_KV_RANK = 128

GQA_HEAD_DIM = 64
GQA_HEADS = BRANCH_W // GQA_HEAD_DIM
GQA_KV_HEADS = GQA_HEADS // 2

CHUNK = 128
CM_GROUPS = 4
CM_GDIM = BRANCH_W // CM_GROUPS

FNET_GROUPS = 4

Q_BLOCK = 128
ROPE_THETA = 10000.0
EPS = 1e-6

IN_SPLITS = (
    MLA_Q_RANK, MLA_KV_RANK, MLA_ROPE_DIM, BRANCH_W,
    GQA_HEADS * GQA_HEAD_DIM, GQA_KV_HEADS * GQA_HEAD_DIM, GQA_KV_HEADS * GQA_HEAD_DIM, BRANCH_W,
    BRANCH_W, BRANCH_W, BRANCH_W,
    BRANCH_W, BRANCH_W,
)
IN_WIDTH = sum(IN_SPLITS)
SPLIT_IDX = tuple(sum(IN_SPLITS[: i + 1]) for i in range(len(IN_SPLITS) - 1))

kernel_name = "hybrid_parallel_groups_mla_gqa_gmlp_fnet_dit"


def rmsnorm(x, g):
    xf = x.astype(jnp.float32)
    y = xf * lax.rsqrt(jnp.mean(xf * xf, axis=-1, keepdims=True) + EPS)
    return (y * g.astype(jnp.float32)).astype(x.dtype)


def layernorm(x, g, b):
    xf = x.astype(jnp.float32)
    mu = jnp.mean(xf, axis=-1, keepdims=True)
    var = jnp.mean(jnp.square(xf - mu), axis=-1, keepdims=True)
    y = (xf - mu) * lax.rsqrt(var + EPS)
    return (y * g.astype(jnp.float32) + b.astype(jnp.float32)).astype(x.dtype)


def rope_1d(x, pos):
    d = x.shape[-1]
    inv = ROPE_THETA ** (-jnp.arange(0, d, 2, dtype=jnp.float32) / d)
    ang = pos.astype(jnp.float32)[:, None] * inv[None, :]
    cos = jnp.concatenate([jnp.cos(ang), jnp.cos(ang)], -1)[:, None, :]
    sin = jnp.concatenate([jnp.sin(ang), jnp.sin(ang)], -1)[:, None, :]
    xf = x.astype(jnp.float32)
    rot = jnp.concatenate([-xf[..., d // 2:], xf[..., : d // 2]], -1)
    return (xf * cos + rot * sin).astype(x.dtype)


def axial_rope(x, row, col):
    half = x.shape[-1] // 2
    return jnp.concatenate([rope_1d(x[..., :half], row), rope_1d(x[..., half:], col)], -1)


def attend(q, k, v):
    B, S, H, dk = q.shape
    Hk, dv = k.shape[2], v.shape[-1]
    G = H // Hk
    scale = dk ** -0.5
    nb = S // Q_BLOCK
    qb = q.reshape(B, nb, Q_BLOCK, Hk, G, dk).transpose(1, 0, 2, 3, 4, 5)

    def block(qblk):
        s = jnp.einsum("bqkgd,blkd->bkgql", qblk, k).astype(jnp.float32) * scale
        pr = jax.nn.softmax(s, axis=-1).astype(v.dtype)
        return jnp.einsum("bkgql,blkd->bqkgd", pr, v)

    out = lax.map(block, qb)
    return out.transpose(1, 0, 2, 3, 4, 5).reshape(B, S, H * dv)


def attn_features(parts, p, row, col):
    cq, ckv, kr = parts[0], parts[1], parts[2]
    q2, k2, v2 = parts[4], parts[5], parts[6]
    B, S, _ = cq.shape
    q1 = (rmsnorm(cq, p["mla_q_norm"]) @ p["mla_w_uq"]).reshape(B, S, MLA_HEADS, MLA_QK_DIM)
    kv = (rmsnorm(ckv, p["mla_kv_norm"]) @ p["mla_w_ukv"]).reshape(B, S, MLA_HEADS, MLA_NOPE_DIM + MLA_V_DIM)
    k_nope, v1 = kv[..., :MLA_NOPE_DIM], kv[..., MLA_NOPE_DIM:]
    k_rope = jnp.broadcast_to(kr[:, :, None, :], (B, S, MLA_HEADS, MLA_ROPE_DIM))
    k1 = jnp.concatenate([k_nope, k_rope], -1)
    q1 = rmsnorm(q1, p["mla_qn"])
    k1 = rmsnorm(k1, p["mla_kn"])
    q2 = rmsnorm(q2.reshape(B, S, GQA_HEADS, GQA_HEAD_DIM), p["gqa_qn"])
    k2 = rmsnorm(k2.reshape(B, S, GQA_KV_HEADS, GQA_HEAD_DIM), p["gqa_kn"])
    v2 = v2.reshape(B, S, GQA_KV_HEADS, GQA_HEAD_DIM)
    if row is not None:
        q1 = jnp.concatenate([q1[..., :MLA_NOPE_DIM], axial_rope(q1[..., MLA_NOPE_DIM:], row, col)], -1)
        k1 = jnp.concatenate([k1[..., :MLA_NOPE_DIM], axial_rope(k1[..., MLA_NOPE_DIM:], row, col)], -1)
        q2 = axial_rope(q2, row, col)
        k2 = axial_rope(k2, row, col)
    return q1, k1, v1, q2, k2, v2


def chunk_mlp(u, v, p):
    B, S, W = v.shape
    vn = layernorm(v, p["cm_ln_g"], p["cm_ln_b"]).reshape(B, S // CHUNK, CHUNK, CM_GROUPS, CM_GDIM)
    s = jnp.einsum("gpq,bnqgc->bnpgc", p["cm_w_s"], vn) + p["cm_b_s"].T[:, :, None]
    return u * s.reshape(B, S, W)


def fourier(f, w_f):
    B, S, W = f.shape
    ff = f.astype(jnp.float32).reshape(B, S, FNET_GROUPS, W // FNET_GROUPS)
    y = jnp.fft.fft2(ff, axes=(1, 3), norm="ortho").real.astype(f.dtype).reshape(B, S, W)
    return y @ w_f


def merge(parts, att_a, att_b, p):
    ga, gb, u, vc, gc, f, gd = parts[3], parts[7], parts[8], parts[9], parts[10], parts[11], parts[12]
    ya = att_a * jax.nn.silu(ga)
    yb = att_b * jax.nn.silu(gb)
    yc = chunk_mlp(u, vc, p) * jax.nn.silu(gc)
    yd = fourier(f, p["fnet_w"]) * jax.nn.silu(gd)
    return jnp.concatenate([ya, yb, yc, yd], -1) @ p["w_out"]


def layer(x, ctx, c, c_ctx, p, row, col, update_ctx):
    sh, sc, gt = jnp.split(jax.nn.silu(c) @ p["w_mod"] + p["b_mod"], 3, axis=-1)
    shc, scc, gtc = jnp.split(jax.nn.silu(c_ctx) @ p["w_mod"] + p["b_mod"], 3, axis=-1)
    hx = rmsnorm(x, p["norm_g"]) * (1.0 + sc[:, None, :]) + sh[:, None, :]
    hc = rmsnorm(ctx, p["norm_g"]) * (1.0 + scc) + shc
    px = jnp.split(hx @ p["w_in"], SPLIT_IDX, axis=-1)
    pc = jnp.split(hc @ p["w_in"], SPLIT_IDX, axis=-1)
    qa_x, ka_x, va_x, qb_x, kb_x, vb_x = attn_features(px, p, row, col)
    qa_c, ka_c, va_c, qb_c, kb_c, vb_c = attn_features(pc, p, None, None)
    att_a = attend(qa_x, jnp.concatenate([ka_x, ka_c], 1), jnp.concatenate([va_x, va_c], 1))
    att_b = attend(qb_x, jnp.concatenate([kb_x, kb_c], 1), jnp.concatenate([vb_x, vb_c], 1))
    x_new = x + gt[:, None, :] * merge(px, att_a, att_b, p)
    if update_ctx:
        att_ac = attend(qa_c, ka_c, va_c)
        att_bc = attend(qb_c, kb_c, vb_c)
        ctx = ctx + gtc * merge(pc, att_ac, att_bc, p)
    return x_new, ctx


def setup_inputs(seed: int = 0) -> dict:
    key = jax.random.key(seed)
    ks = jax.random.split(key, 24)
    f32 = jnp.float32

    def nrm(k, shape, scale):
        return jax.random.normal(k, shape, f32) * scale

    def gain(k, shape):
        return 1.0 + 0.02 * jax.random.normal(k, shape, f32)

    L, D = DEPTH, D_MODEL
    return {
        "x": nrm(ks[0], (BATCH, SEQ, D), 1.0),
        "c": nrm(ks[1], (BATCH, D), 1.0),
        "ctx": nrm(ks[2], (BATCH, CTX_LEN, D), 1.0),
        "c_ctx": nrm(ks[3], (D,), 1.0),
        "norm_g": gain(ks[4], (L, D)),
        "w_mod": nrm(ks[5], (L, D, 3 * D), 0.5 * D ** -0.5),
        "b_mod": nrm(ks[6], (L, 3 * D), 0.02),
        "w_in": nrm(ks[7], (L, D, IN_WIDTH), D ** -0.5),
        "mla_q_norm": gain(ks[8], (L, MLA_Q_RANK)),
        "mla_w_uq": nrm(ks[9], (L, MLA_Q_RANK, MLA_HEADS * MLA_QK_DIM), MLA_Q_RANK ** -0.5),
        "mla_kv_norm": gain(ks[10], (L, MLA_KV_RANK)),
        "mla_w_ukv": nrm(ks[11], (L, MLA_KV_RANK, MLA_HEADS * (MLA_NOPE_DIM + MLA_V_DIM)), MLA_KV_RANK ** -0.5),
        "mla_qn": gain(ks[12], (L, MLA_QK_DIM)),
        "mla_kn": gain(ks[13], (L, MLA_QK_DIM)),
        "gqa_qn": gain(ks[14], (L, GQA_HEAD_DIM)),
        "gqa_kn": gain(ks[15], (L, GQA_HEAD_DIM)),
        "cm_ln_g": gain(ks[16], (L, BRANCH_W)),
        "cm_ln_b": nrm(ks[17], (L, BRANCH_W), 0.02),
        "cm_w_s": nrm(ks[18], (L, CM_GROUPS, CHUNK, CHUNK), CHUNK ** -0.5),
        "cm_b_s": gain(ks[19], (L, CM_GROUPS, CHUNK)),
        "fnet_w": nrm(ks[20], (L, BRANCH_W, BRANCH_W), BRANCH_W ** -0.5),
        "w_out": nrm(ks[21], (L, D_MIX, D), D_MIX ** -0.5),
    }


def reference(x, c, ctx, c_ctx, norm_g, w_mod, b_mod, w_in, mla_q_norm, mla_w_uq, mla_kv_norm,
              mla_w_ukv, mla_qn, mla_kn, gqa_qn, gqa_kn, cm_ln_g, cm_ln_b, cm_w_s, cm_b_s,
              fnet_w, w_out):
    S = x.shape[1]
    ROWS = S // GRID_W
    row = jnp.repeat(jnp.arange(ROWS, dtype=jnp.int32), GRID_W)
    col = jnp.tile(jnp.arange(GRID_W, dtype=jnp.int32), ROWS)
    for l in range(DEPTH):
        p = dict(norm_g=norm_g[l], w_mod=w_mod[l], b_mod=b_mod[l], w_in=w_in[l],
                 mla_q_norm=mla_q_norm[l], mla_w_uq=mla_w_uq[l], mla_kv_norm=mla_kv_norm[l],
                 mla_w_ukv=mla_w_ukv[l], mla_qn=mla_qn[l], mla_kn=mla_kn[l],
                 gqa_qn=gqa_qn[l], gqa_kn=gqa_kn[l], cm_ln_g=cm_ln_g[l], cm_ln_b=cm_ln_b[l],
                 cm_w_s=cm_w_s[l], cm_b_s=cm_b_s[l], fnet_w=fnet_w[l], w_out=w_out[l])
        x, ctx = layer(x, ctx, c, c_ctx, p, row, col, l < DEPTH - 1)
    return x
```

```python
import functools

import numpy as np
import jax
import jax.numpy as jnp
from jax import lax
from jax.experimental import pallas as pl
from jax.experimental.pallas import tpu as pltpu

D_MODEL = 1024
GRID_W = 64
BRANCH_W = 256
MLA_HEADS = 4
MLA_NOPE_DIM = 64
MLA_ROPE_DIM = 32
MLA_QK_DIM = 96
MLA_V_DIM = 64
MLA_Q_RANK = 192
MLA_KV_RANK = 128
GQA_HEADS = 4
GQA_KV_HEADS = 2
GQA_HEAD_DIM = 64
CHUNK = 128
CM_GROUPS = 4
FNET_GROUPS = 4
FNET_GDIM = BRANCH_W // FNET_GROUPS
ROPE_THETA = 10000.0
EPS = 1e-6

LANES = 128
V_ROWS = MLA_V_DIM + 16
PROJ_TILE = 256
ATTN_TILE = 256
VMEM_LIMIT = 48 * 1024 * 1024

F32 = jnp.float32
BF16 = jnp.bfloat16

C_CQ, C_CKV, C_KR, C_GA, C_Q2, C_K2, C_V2, C_GB, C_U, C_VC, C_GC, C_F, C_GD, C_END = (
    0, 256, 384, 512, 768, 1024, 1152, 1280, 1536, 1792, 2048, 2304, 2560, 2816)


def _dot(a, b):
    return jnp.dot(a, b, preferred_element_type=F32)


def _dot_nt(a, b):
    return lax.dot_general(a, b, (((1,), (1,)), ((), ())), preferred_element_type=F32)


def _silu(x):
    return x * jax.nn.sigmoid(x)


def _lane_id(shape):
    return lax.broadcasted_iota(jnp.int32, shape, len(shape) - 1)


def _mod_kernel(c_ref, w_ref, b_ref, o_ref):
    c = c_ref[...]
    o_ref[...] = _dot(_silu(c).astype(BF16), w_ref[...].astype(BF16)) + b_ref[...]


def _modulation(cc, w_mod, b_mod):
    n_layers = w_mod.shape[0]
    rows = cc.shape[0]
    return pl.pallas_call(
        _mod_kernel,
        grid=(n_layers, 3),
        in_specs=[
            pl.BlockSpec((rows, D_MODEL), lambda l, j: (0, 0)),
            pl.BlockSpec((None, D_MODEL, D_MODEL), lambda l, j: (l, 0, j)),
            pl.BlockSpec((None, 1, D_MODEL), lambda l, j: (l, 0, j)),
        ],
        out_specs=pl.BlockSpec((None, rows, D_MODEL), lambda l, j: (l, 0, j)),
        out_shape=jax.ShapeDtypeStruct((n_layers, rows, 3 * D_MODEL), F32),
        compiler_params=pltpu.CompilerParams(vmem_limit_bytes=VMEM_LIMIT),
        name="modulation",
    )(cc, w_mod, b_mod.reshape(n_layers, 1, 3 * D_MODEL))


def _rope(y, tab_ref, half):
    width = y.shape[1]
    reps = width // LANES
    cos, sin_a, sin_b = tab_ref[0], tab_ref[1], tab_ref[2]
    if reps > 1:
        cos = jnp.concatenate([cos] * reps, axis=1)
        sin_a = jnp.concatenate([sin_a] * reps, axis=1)
        sin_b = jnp.concatenate([sin_b] * reps, axis=1)
    return (y * cos + pltpu.roll(y, width - half, 1) * sin_a + pltpu.roll(y, half, 1) * sin_b)


def _head_rms_128(x, gain, dim):
    outs = []
    for g in range(x.shape[1] // LANES):
        xg = x[:, g * LANES:(g + 1) * LANES]
        ms = jnp.sum(xg * xg, axis=-1, keepdims=True) * (1.0 / dim)
        outs.append(xg * lax.rsqrt(ms + EPS) * gain)
    return jnp.concatenate(outs, axis=1) if len(outs) > 1 else outs[0]


def _head_rms_64(x, gain):
    outs = []
    for g in range(x.shape[1] // LANES):
        xg = x[:, g * LANES:(g + 1) * LANES]
        lo = _lane_id(xg.shape) < GQA_HEAD_DIM
        x2 = xg * xg
        s_lo = jnp.sum(jnp.where(lo, x2, 0.0), axis=-1, keepdims=True)
        s_hi = jnp.sum(jnp.where(lo, 0.0, x2), axis=-1, keepdims=True)
        ms = jnp.where(lo, s_lo, s_hi) * (1.0 / GQA_HEAD_DIM)
        outs.append(xg * lax.rsqrt(ms + EPS) * gain)
    return jnp.concatenate(outs, axis=1) if len(outs) > 1 else outs[0]


def _proj_kernel(*refs, rope, kv_only, ts):
    it = iter(refs)
    x_ref, mod_ref, ng_ref, w_ref, pv_ref, wuq_ref, wukv_ref, wcm_ref, bcm_ref = (
        next(it) for _ in range(9))
    ra_ref = rb_ref = None
    if rope:
        ra_ref, rb_ref = next(it), next(it)
    if kv_only:
        k1_o, v1t_o, k2_o, v2t_o = (next(it) for _ in range(4))
    else:
        (q1_o, k1_o, v1t_o, q2_o, k2_o, v2t_o, sga_o, sgb_o, yc_o, f_o, sgd_o) = (
            next(it) for _ in range(11))

    x = x_ref[...]
    shift, scale = mod_ref[0:1, :], mod_ref[1:2, :]
    ms = jnp.mean(x * x, axis=-1, keepdims=True)
    h = (x * lax.rsqrt(ms + EPS) * ng_ref[...]) * (1.0 + scale) + shift
    hb = h.astype(BF16)

    def proj(lo, hi):
        return _dot(hb, w_ref[:, lo:hi])

    g_q, g_kv = pv_ref[0:1, :], pv_ref[1:2, 0:LANES]
    g_qn, g_kn = pv_ref[2:3, 0:LANES], pv_ref[3:4, 0:LANES]
    g_q2, g_k2 = pv_ref[4:5, 0:LANES], pv_ref[5:6, 0:LANES]
    ln_g, ln_b = pv_ref[6:7, :], pv_ref[7:8, :]

    ckv = proj(C_CKV, C_KR)
    msk = jnp.mean(ckv * ckv, axis=-1, keepdims=True)
    ckvn = (ckv * lax.rsqrt(msk + EPS) * g_kv).astype(BF16)
    kv = _dot(ckvn, wukv_ref[...])
    kr = proj(C_KR, C_GA)
    k1 = kv[:, 0:4 * LANES] + jnp.concatenate([kr] * MLA_HEADS, axis=1)
    k1 = _head_rms_128(k1, g_kn, MLA_QK_DIM)
    if rope:
        k1 = _rope(k1, ra_ref, MLA_ROPE_DIM // 4)
    k1_o[...] = k1.astype(BF16)
    v1t_o[...] = jnp.transpose(kv[:, 4 * LANES:]).astype(BF16)

    k2 = _head_rms_64(proj(C_K2, C_V2), g_k2)
    if rope:
        k2 = _rope(k2, rb_ref, GQA_HEAD_DIM // 4)
    lo = _lane_id(k2.shape) < GQA_HEAD_DIM
    k2r = pltpu.roll(k2, GQA_HEAD_DIM, 1)
    zero = jnp.zeros_like(k2)
    k2_o[:, 0 * LANES:1 * LANES] = jnp.where(lo, k2, zero).astype(BF16)
    k2_o[:, 1 * LANES:2 * LANES] = jnp.where(lo, zero, k2r).astype(BF16)
    k2_o[:, 2 * LANES:3 * LANES] = jnp.where(lo, k2r, zero).astype(BF16)
    k2_o[:, 3 * LANES:4 * LANES] = jnp.where(lo, zero, k2).astype(BF16)
    v2t = jnp.transpose(proj(C_V2, C_GB)).astype(BF16)
    hd = GQA_HEAD_DIM
    v2t_o[0 * hd:1 * hd, :] = v2t[0:hd]
    v2t_o[1 * hd:2 * hd, :] = v2t[0:hd]
    v2t_o[2 * hd:3 * hd, :] = v2t[hd:2 * hd]
    v2t_o[3 * hd:4 * hd, :] = v2t[hd:2 * hd]
    if kv_only:
        return

    cq = proj(C_CQ, C_CKV)
    msq = jnp.sum(cq * cq, axis=-1, keepdims=True) * (1.0 / MLA_Q_RANK)
    cqn = (cq * lax.rsqrt(msq + EPS) * g_q).astype(BF16)
    q1 = _head_rms_128(_dot(cqn, wuq_ref[...]), g_qn, MLA_QK_DIM)
    if rope:
        q1 = _rope(q1, ra_ref, MLA_ROPE_DIM // 4)
    q1_o[...] = (q1 * (MLA_QK_DIM ** -0.5)).astype(BF16)

    q2 = _head_rms_64(proj(C_Q2, C_K2), g_q2)
    if rope:
        q2 = _rope(q2, rb_ref, GQA_HEAD_DIM // 4)
    q2_o[...] = (q2 * (GQA_HEAD_DIM ** -0.5)).astype(BF16)

    sga_o[...] = _silu(proj(C_GA, C_Q2))
    sgb_o[...] = _silu(proj(C_GB, C_U))
    sgd_o[...] = _silu(proj(C_GD, C_END))
    f_o[...] = proj(C_F, C_GD).astype(BF16)

    u = proj(C_U, C_VC)
    vc = proj(C_VC, C_GC)
    sgc = _silu(proj(C_GC, C_F))
    mu = jnp.mean(vc, axis=-1, keepdims=True)
    var = jnp.mean(jnp.square(vc - mu), axis=-1, keepdims=True)
    vn = ((vc - mu) * lax.rsqrt(var + EPS) * ln_g + ln_b).astype(BF16)
    grp = lax.shift_right_logical(_lane_id((CHUNK, BRANCH_W)), 6)
    for c in range(ts // CHUNK):
        vnc = vn[c * CHUNK:(c + 1) * CHUNK, :]
        stacked = jnp.concatenate(
            [jnp.where(grp == g, vnc, jnp.zeros_like(vnc)) for g in range(CM_GROUPS)], axis=0)
        s = _dot(wcm_ref[...], stacked) + bcm_ref[...]
        rows = slice(c * CHUNK, (c + 1) * CHUNK)
        yc_o[rows, :] = (u[rows, :] * s * sgc[rows, :]).astype(BF16)


def _projection(xs, mod, lw, rope_tabs, *, kv_only):
    bsz, seq, _ = xs.shape
    ts = min(PROJ_TILE, seq)
    rope = rope_tabs is not None
    per_sample = mod.shape[0] == bsz

    def full(a):
        nd = a.ndim
        return pl.BlockSpec(a.shape, lambda b, i: (0,) * nd)

    in_specs = [
        pl.BlockSpec((None, ts, D_MODEL), lambda b, i: (b, i, 0)),
        pl.BlockSpec((None, 3, D_MODEL), (lambda b, i: (b, 0, 0)) if per_sample else (lambda b, i: (0, 0, 0))),
        full(lw["norm_g"]), full(lw["w_in"]), full(lw["pvec"]), full(lw["w_uq"]), full(lw["w_ukv"]),
        full(lw["w_cm"]), full(lw["b_cm"]),
    ]
    args = [xs, mod, lw["norm_g"], lw["w_in"], lw["pvec"], lw["w_uq"], lw["w_ukv"], lw["w_cm"], lw["b_cm"]]
    if rope:
        for t in rope_tabs:
            in_specs.append(pl.BlockSpec((3, ts, LANES), lambda b, i: (0, i, 0)))
            args.append(t)

    def rows_out(width, dtype):
        return (pl.BlockSpec((None, ts, width), lambda b, i: (b, i, 0)),
                jax.ShapeDtypeStruct((bsz, seq, width), dtype))

    def cols_out(height):
        return (pl.BlockSpec((None, height, ts), lambda b, i: (b, 0, i)),
                jax.ShapeDtypeStruct((bsz, height, seq), BF16))

    if kv_only:
        outs = [rows_out(512, BF16), cols_out(256), rows_out(512, BF16), cols_out(256)]
    else:
        outs = [rows_out(512, BF16), rows_out(512, BF16), cols_out(256),
                rows_out(256, BF16), rows_out(512, BF16), cols_out(256),
                rows_out(256, F32), rows_out(256, F32), rows_out(256, BF16), rows_out(256, BF16),
                rows_out(256, F32)]
    return pl.pallas_call(
        functools.partial(_proj_kernel, rope=rope, kv_only=kv_only, ts=ts),
        grid=(bsz, seq // ts),
        in_specs=in_specs,
        out_specs=[o[0] for o in outs],
        out_shape=[o[1] for o in outs],
        compiler_params=pltpu.CompilerParams(
            dimension_semantics=("arbitrary", "arbitrary"), vmem_limit_bytes=VMEM_LIMIT),
        name="projection",
    )(*args)


def _attn_kernel(*refs, n_src):
    it = iter(refs)
    qe_ref, qo_ref, g_ref = next(it), next(it), next(it)
    srcs = [(next(it), next(it), next(it)) for _ in range(n_src)]
    o_ref, vte_s, vto_s = next(it), next(it), next(it)

    @pl.when(pl.program_id(2) == 0)
    def _():
        off = 0
        for _, _, vt in srcs:
            n = vt.shape[1]
            vte_s[0:MLA_V_DIM, off:off + n] = vt[0:MLA_V_DIM, :]
            vto_s[0:MLA_V_DIM, off:off + n] = vt[MLA_V_DIM:2 * MLA_V_DIM, :]
            off += n
        ones = jnp.ones((V_ROWS - MLA_V_DIM, off), BF16)
        vte_s[MLA_V_DIM:V_ROWS, :] = ones
        vto_s[MLA_V_DIM:V_ROWS, :] = ones

    def head(q_ref, parity, vt_s):
        q = q_ref[...]
        scores = [_dot_nt(src[parity][...], q) for src in srcs]
        m = functools.reduce(jnp.maximum, [jnp.max(s, axis=0, keepdims=True) for s in scores])
        acc = None
        off = 0
        for s in scores:
            n = s.shape[0]
            p = jnp.exp(s - m).astype(BF16)
            part = _dot(vt_s[:, off:off + n], p)
            acc = part if acc is None else acc + part
            off += n
        return acc[0:MLA_V_DIM, :] / acc[MLA_V_DIM:MLA_V_DIM + 1, :]

    att_t = jnp.concatenate([head(qe_ref, 0, vte_s), head(qo_ref, 1, vto_s)], axis=0)
    o_ref[...] = (jnp.transpose(att_t) * g_ref[...]).astype(BF16)


def _attention(q, gate, sources, *, paired_q):
    bsz, sq, _ = q.shape
    tq = min(ATTN_TILE, sq)
    if paired_q:
        qe_map = qo_map = lambda b, j, i: (b, i, j)
    else:
        qe_map = lambda b, j, i: (b, i, 2 * j)
        qo_map = lambda b, j, i: (b, i, 2 * j + 1)
    in_specs = [
        pl.BlockSpec((None, tq, LANES), qe_map),
        pl.BlockSpec((None, tq, LANES), qo_map),
        pl.BlockSpec((None, tq, LANES), lambda b, j, i: (b, i, j)),
    ]
    args = [q, q, gate]
    total = 0
    for k, vt in sources:
        n = k.shape[1]
        total += n
        in_specs += [
            pl.BlockSpec((None, n, LANES), lambda b, j, i: (b, 0, 2 * j)),
            pl.BlockSpec((None, n, LANES), lambda b, j, i: (b, 0, 2 * j + 1)),
            pl.BlockSpec((None, LANES, n), lambda b, j, i: (b, j, 0)),
        ]
        args += [k, k, vt]
    return pl.pallas_call(
        functools.partial(_attn_kernel, n_src=len(sources)),
        grid=(bsz, 2, sq // tq),
        in_specs=in_specs,
        out_specs=pl.BlockSpec((None, tq, LANES), lambda b, j, i: (b, i, j)),
        out_shape=jax.ShapeDtypeStruct((bsz, sq, BRANCH_W), BF16),
        scratch_shapes=[pltpu.VMEM((V_ROWS, total), BF16), pltpu.VMEM((V_ROWS, total), BF16)],
        compiler_params=pltpu.CompilerParams(
            dimension_semantics=("arbitrary", "arbitrary", "arbitrary"), vmem_limit_bytes=VMEM_LIMIT),
        name="attention",
    )(*args)


def _fourier_kernel(f_ref, g_ref, cm_ref, bc_ref, bs_ref, wf_ref, o_ref, *, seq):
    f = f_ref[...]
    g_cos = _dot(f, bc_ref[...]).astype(BF16)
    g_sin = _dot(f, bs_ref[...]).astype(BF16)
    y = _dot(cm_ref[:, 0:seq], g_cos) + _dot(cm_ref[:, seq:2 * seq], g_sin)
    y = (y * ((seq * FNET_GDIM) ** -0.5)).astype(BF16)
    o_ref[...] = (_dot(y, wf_ref[...]) * g_ref[...]).astype(BF16)


def _fourier(f, gate, dft, lw):
    bsz, seq, _ = f.shape
    cm, bc, bs = dft

    def const(a):
        return pl.BlockSpec(a.shape, lambda b: (0,) * a.ndim, pipeline_mode=pl.Buffered(1))

    return pl.pallas_call(
        functools.partial(_fourier_kernel, seq=seq),
        grid=(bsz,),
        in_specs=[
            pl.BlockSpec((None, seq, BRANCH_W), lambda b: (b, 0, 0)),
            pl.BlockSpec((None, seq, BRANCH_W), lambda b: (b, 0, 0)),
            const(cm), const(bc), const(bs), const(lw["fnet_w"]),
        ],
        out_specs=pl.BlockSpec((None, seq, BRANCH_W), lambda b: (b, 0, 0)),
        out_shape=jax.ShapeDtypeStruct((bsz, seq, BRANCH_W), BF16),
        compiler_params=pltpu.CompilerParams(
            dimension_semantics=("arbitrary",), vmem_limit_bytes=VMEM_LIMIT),
        name="fourier",
    )(f, gate, cm, bc, bs, lw["fnet_w"])


def _merge_kernel(x_ref, mod_ref, ya_ref, yb_ref, yc_ref, yd_ref, w_ref, o_ref):
    cat = jnp.concatenate([ya_ref[...], yb_ref[...], yc_ref[...], yd_ref[...]], axis=1)
    o_ref[...] = x_ref[...] + mod_ref[2:3, :] * _dot(cat, w_ref[...])


def _merge(xs, mod, ya, yb, yc, yd, w_out):
    bsz, seq, _ = xs.shape
    ts = min(PROJ_TILE, seq)
    per_sample = mod.shape[0] == bsz
    branch = pl.BlockSpec((None, ts, BRANCH_W), lambda b, i: (b, i, 0))
    return pl.pallas_call(
        _merge_kernel,
        grid=(bsz, seq // ts),
        in_specs=[
            pl.BlockSpec((None, ts, D_MODEL), lambda b, i: (b, i, 0)),
            pl.BlockSpec((None, 3, D_MODEL), (lambda b, i: (b, 0, 0)) if per_sample else (lambda b, i: (0, 0, 0))),
            branch, branch, branch, branch,
            pl.BlockSpec(w_out.shape, lambda b, i: (0, 0)),
        ],
        out_specs=pl.BlockSpec((None, ts, D_MODEL), lambda b, i: (b, i, 0)),
        out_shape=jax.ShapeDtypeStruct(xs.shape, F32),
        compiler_params=pltpu.CompilerParams(
            dimension_semantics=("arbitrary", "arbitrary"), vmem_limit_bytes=VMEM_LIMIT),
        name="merge",
    )(xs, mod, ya, yb, yc, yd, w_out)


def _rope_table(seq, segments, half):
    pos = jnp.arange(seq, dtype=jnp.int32)
    row = (pos // GRID_W).astype(F32)
    col = (pos % GRID_W).astype(F32)
    idx = np.zeros(LANES, np.int32)
    active = np.zeros(LANES, bool)
    use_col = np.zeros(LANES, bool)
    first = np.zeros(LANES, bool)
    dim = segments[0][1]
    for start, d, kind in segments:
        assert d == dim and d // 2 == half
        for t in range(d):
            idx[start + t] = t % (d // 2)
            active[start + t] = True
            use_col[start + t] = kind == 1
            first[start + t] = t < d // 2
    inv = ROPE_THETA ** (-jnp.arange(0, dim, 2, dtype=F32) / dim)
    inv_lane = inv[idx][None, :]
    ang = jnp.where(use_col[None, :], col[:, None], row[:, None]) * inv_lane
    act = active[None, :]
    cos = jnp.where(act, jnp.cos(ang), 1.0)
    sin = jnp.sin(ang)
    sin_a = jnp.where(act & first[None, :], -sin, 0.0)
    sin_b = jnp.where(act & ~first[None, :], sin, 0.0)
    return jnp.stack([cos, sin_a, sin_b]).astype(F32)


def _dft_tables(seq):
    k = jnp.arange(seq, dtype=jnp.int32)
    ang = ((k[:, None] * k[None, :]) % seq).astype(F32) * (2.0 * np.pi / seq)
    cm = jnp.concatenate([jnp.cos(ang), -jnp.sin(ang)], axis=1).astype(BF16)
    c = jnp.arange(BRANCH_W, dtype=jnp.int32)
    same = (c[:, None] // FNET_GDIM) == (c[None, :] // FNET_GDIM)
    cang = (((c[:, None] % FNET_GDIM) * (c[None, :] % FNET_GDIM)) % FNET_GDIM).astype(F32) * (
        2.0 * np.pi / FNET_GDIM)
    bc = jnp.where(same, jnp.cos(cang), 0.0).astype(BF16)
    bs = jnp.where(same, jnp.sin(cang), 0.0).astype(BF16)
    return cm, bc, bs


def _pad_cols(a, width):
    return jnp.pad(a, ((0, 0), (0, width - a.shape[1])))


def _layer_weights(l, norm_g, w_in, mla_q_norm, mla_w_uq, mla_kv_norm, mla_w_ukv, mla_qn, mla_kn,
                   gqa_qn, gqa_kn, cm_ln_g, cm_ln_b, cm_w_s, cm_b_s, fnet_w, w_out):
    w = w_in[l]
    sp = np.cumsum([0, 192, 128, 32, 256, 256, 128, 128, 256, 256, 256, 256, 256, 256])
    nat = [w[:, sp[i]:sp[i + 1]] for i in range(13)]
    cq, ckv, kr, ga, q2, k2, v2, gb, u, vc, gc, f, gd = nat
    kr_p = jnp.pad(kr, ((0, 0), (MLA_NOPE_DIM, LANES - MLA_NOPE_DIM - MLA_ROPE_DIM)))
    w_in_p = jnp.concatenate(
        [_pad_cols(cq, 256), ckv, kr_p, ga, q2, k2, v2, gb, u, vc, gc, f, gd], axis=1).astype(BF16)
    assert w_in_p.shape[1] == C_END

    wuq = mla_w_uq[l].reshape(MLA_Q_RANK, MLA_HEADS, MLA_QK_DIM)
    wuq = jnp.pad(wuq, ((0, 256 - MLA_Q_RANK), (0, 0), (0, LANES - MLA_QK_DIM)))
    wuq = wuq.reshape(256, MLA_HEADS * LANES).astype(BF16)
    wukv = mla_w_ukv[l].reshape(MLA_KV_RANK, MLA_HEADS, MLA_NOPE_DIM + MLA_V_DIM)
    wk = jnp.pad(wukv[:, :, :MLA_NOPE_DIM], ((0, 0), (0, 0), (0, LANES - MLA_NOPE_DIM)))
    wv = wukv[:, :, MLA_NOPE_DIM:]
    wukv_p = jnp.concatenate(
        [wk.reshape(MLA_KV_RANK, MLA_HEADS * LANES), wv.reshape(MLA_KV_RANK, MLA_HEADS * MLA_V_DIM)],
        axis=1).astype(BF16)

    def row(v):
        return _pad_cols(v[None, :].astype(F32), 256)

    pvec = jnp.concatenate([
        row(mla_q_norm[l]), row(mla_kv_norm[l]), row(mla_qn[l]), row(mla_kn[l]),
        row(jnp.tile(gqa_qn[l], 2)), row(jnp.tile(gqa_kn[l], 2)), row(cm_ln_g[l]), row(cm_ln_b[l])], axis=0)

    w_cm = jnp.transpose(cm_w_s[l], (1, 0, 2)).reshape(CHUNK, CM_GROUPS * CHUNK).astype(BF16)
    b_cm = jnp.broadcast_to(cm_b_s[l].T[:, :, None], (CHUNK, CM_GROUPS, BRANCH_W // CM_GROUPS))
    b_cm = b_cm.reshape(CHUNK, BRANCH_W).astype(F32)
    return dict(norm_g=norm_g[l][None, :], w_in=w_in_p, pvec=pvec, w_uq=wuq, w_ukv=wukv_p,
                w_cm=w_cm, b_cm=b_cm, fnet_w=fnet_w[l].astype(BF16), w_out=w_out[l].astype(BF16))


def kernel(x, c, ctx, c_ctx, norm_g, w_mod, b_mod, w_in, mla_q_norm, mla_w_uq, mla_kv_norm, mla_w_ukv,
           mla_qn, mla_kn, gqa_qn, gqa_kn, cm_ln_g, cm_ln_b, cm_w_s, cm_b_s, fnet_w, w_out):
    bsz, seq, _ = x.shape
    ctx_len = ctx.shape[1]
    depth = w_in.shape[0]

    rows = -(-(bsz + 1) // 8) * 8
    cc = jnp.concatenate([c, c_ctx[None, :], jnp.zeros((rows - bsz - 1, D_MODEL), F32)], axis=0)
    mod_all = _modulation(cc, w_mod, b_mod)

    rope_a = _rope_table(seq, [(MLA_NOPE_DIM, 16, 0), (MLA_NOPE_DIM + 16, 16, 1)], 8)
    rope_b = _rope_table(seq, [(0, 32, 0), (32, 32, 1), (64, 32, 0), (96, 32, 1)], 16)
    dft_x = _dft_tables(seq)
    dft_c = _dft_tables(ctx_len)

    for l in range(depth):
        lw = _layer_weights(l, norm_g, w_in, mla_q_norm, mla_w_uq, mla_kv_norm, mla_w_ukv, mla_qn,
                            mla_kn, gqa_qn, gqa_kn, cm_ln_g, cm_ln_b, cm_w_s, cm_b_s, fnet_w, w_out)
        mod_x = mod_all[l, :bsz].reshape(bsz, 3, D_MODEL)
        mod_c = mod_all[l, bsz:bsz + 1].reshape(1, 3, D_MODEL)
        update_ctx = l < depth - 1

        (q1, k1, v1t, q2, k2, v2t, sga, sgb, yc, f, sgd) = _projection(
            x, mod_x, lw, (rope_a, rope_b), kv_only=False)
        if update_ctx:
            (q1c, k1c, v1tc, q2c, k2c, v2tc, sgac, sgbc, ycc, fc, sgdc) = _projection(
                ctx, mod_c, lw, None, kv_only=False)
        else:
            k1c, v1tc, k2c, v2tc = _projection(ctx, mod_c, lw, None, kv_only=True)

        ya = _attention(q1, sga, [(k1, v1t), (k1c, v1tc)], paired_q=False)
        yb = _attention(q2, sgb, [(k2, v2t), (k2c, v2tc)], paired_q=True)
        yd = _fourier(f, sgd, dft_x, lw)
        x = _merge(x, mod_x, ya, yb, yc, yd, lw["w_out"])
        if update_ctx:
            yac = _attention(q1c, sgac, [(k1c, v1tc)], paired_q=False)
            ybc = _attention(q2c, sgbc, [(k2c, v2tc)], paired_q=True)
            ydc = _fourier(fc, sgdc, dft_c, lw)
            ctx = _merge(ctx, mod_c, yac, ybc, ycc, ydc, lw["w_out"])
    return x
```

```python
import functools

import numpy as np
import jax
import jax.numpy as jnp
from jax import lax
from jax.experimental import pallas as pl
from jax.experimental.pallas import tpu as pltpu

D_MODEL = 1024
GRID_W = 64
BRANCH_W = 256
MLA_HEADS = 4
MLA_NOPE_DIM = 64
MLA_ROPE_DIM = 32
MLA_QK_DIM = 96
MLA_V_DIM = 64
MLA_Q_RANK = 192
MLA_KV_RANK = 128
GQA_HEADS = 4
GQA_KV_HEADS = 2
GQA_HEAD_DIM = 64
CHUNK = 128
CM_GROUPS = 4
FNET_GROUPS = 4
FNET_GDIM = BRANCH_W // FNET_GROUPS
ROPE_THETA = 10000.0
EPS = 1e-6

LANES = 128
V_ROWS = MLA_V_DIM + 16
PROJ_TILE = 256
ATTN_TILE = 256
SCORE_ROWS = 1024
EXP_ROWS = 256
VMEM_LIMIT = 48 * 1024 * 1024

F32 = jnp.float32
BF16 = jnp.bfloat16

C_CQ, C_CKV, C_KR, C_GA, C_Q2, C_K2, C_V2, C_GB, C_U, C_VC, C_GC, C_F, C_GD, C_END = (
    0, 256, 384, 512, 768, 1024, 1152, 1280, 1536, 1792, 2048, 2304, 2560, 2816)


def _dot(a, b):
    return jnp.dot(a, b, preferred_element_type=F32)


def _dot_nt(a, b):
    return lax.dot_general(a, b, (((1,), (1,)), ((), ())), preferred_element_type=F32)


def _silu(x):
    return x * jax.nn.sigmoid(x)


def _lane_id(shape):
    return lax.broadcasted_iota(jnp.int32, shape, len(shape) - 1)


def _mod_kernel(c_ref, w_ref, b_ref, o_ref):
    c = c_ref[...]
    o_ref[...] = _dot(_silu(c).astype(BF16), w_ref[...].astype(BF16)) + b_ref[...]


def _modulation(cc, w_mod, b_mod):
    n_layers = w_mod.shape[0]
    rows = cc.shape[0]
    return pl.pallas_call(
        _mod_kernel,
        grid=(n_layers, 3),
        in_specs=[
            pl.BlockSpec((rows, D_MODEL), lambda l, j: (0, 0)),
            pl.BlockSpec((None, D_MODEL, D_MODEL), lambda l, j: (l, 0, j)),
            pl.BlockSpec((None, 1, D_MODEL), lambda l, j: (l, 0, j)),
        ],
        out_specs=pl.BlockSpec((None, rows, D_MODEL), lambda l, j: (l, 0, j)),
        out_shape=jax.ShapeDtypeStruct((n_layers, rows, 3 * D_MODEL), F32),
        compiler_params=pltpu.CompilerParams(vmem_limit_bytes=VMEM_LIMIT),
        name="modulation",
    )(cc, w_mod, b_mod.reshape(n_layers, 1, 3 * D_MODEL))


def _rope(y, tab_ref, half):
    width = y.shape[1]
    reps = width // LANES
    cos, sin_a, sin_b = tab_ref[0], tab_ref[1], tab_ref[2]
    if reps > 1:
        cos = jnp.concatenate([cos] * reps, axis=1)
        sin_a = jnp.concatenate([sin_a] * reps, axis=1)
        sin_b = jnp.concatenate([sin_b] * reps, axis=1)
    return (y * cos + pltpu.roll(y, width - half, 1) * sin_a + pltpu.roll(y, half, 1) * sin_b)


def _head_rms_128(x, gain, dim):
    outs = []
    for g in range(x.shape[1] // LANES):
        xg = x[:, g * LANES:(g + 1) * LANES]
        ms = jnp.sum(xg * xg, axis=-1, keepdims=True) * (1.0 / dim)
        outs.append(xg * lax.rsqrt(ms + EPS) * gain)
    return jnp.concatenate(outs, axis=1) if len(outs) > 1 else outs[0]


def _head_rms_64(x, gain):
    outs = []
    for g in range(x.shape[1] // LANES):
        xg = x[:, g * LANES:(g + 1) * LANES]
        lo = _lane_id(xg.shape) < GQA_HEAD_DIM
        x2 = xg * xg
        s_lo = jnp.sum(jnp.where(lo, x2, 0.0), axis=-1, keepdims=True)
        s_hi = jnp.sum(jnp.where(lo, 0.0, x2), axis=-1, keepdims=True)
        ms = jnp.where(lo, s_lo, s_hi) * (1.0 / GQA_HEAD_DIM)
        outs.append(xg * lax.rsqrt(ms + EPS) * gain)
    return jnp.concatenate(outs, axis=1) if len(outs) > 1 else outs[0]


def _proj_kernel(*refs, rope, kv_only, ts):
    it = iter(refs)
    x_ref, mod_ref, ng_ref, w_ref, pv_ref, wuq_ref, wukv_ref, wcm_ref, bcm_ref = (
        next(it) for _ in range(9))
    ra_ref = rb_ref = None
    if rope:
        ra_ref, rb_ref = next(it), next(it)
    if kv_only:
        k1_o, v1t_o, k2_o, v2t_o = (next(it) for _ in range(4))
    else:
        (q1_o, k1_o, v1t_o, q2_o, k2_o, v2t_o, sga_o, sgb_o, yc_o, f_o, sgd_o) = (
            next(it) for _ in range(11))

    x = x_ref[...]
    shift, scale = mod_ref[0:1, :], mod_ref[1:2, :]
    ms = jnp.mean(x * x, axis=-1, keepdims=True)
    h = (x * lax.rsqrt(ms + EPS) * ng_ref[...]) * (1.0 + scale) + shift
    hb = h.astype(BF16)

    def proj(lo, hi):
        return _dot(hb, w_ref[:, lo:hi])

    g_q, g_kv = pv_ref[0:1, :], pv_ref[1:2, 0:LANES]
    g_qn, g_kn = pv_ref[2:3, 0:LANES], pv_ref[3:4, 0:LANES]
    g_q2, g_k2 = pv_ref[4:5, 0:LANES], pv_ref[5:6, 0:LANES]
    ln_g, ln_b = pv_ref[6:7, :], pv_ref[7:8, :]

    ckv = proj(C_CKV, C_KR)
    msk = jnp.mean(ckv * ckv, axis=-1, keepdims=True)
    ckvn = (ckv * lax.rsqrt(msk + EPS) * g_kv).astype(BF16)
    kv = _dot(ckvn, wukv_ref[...])
    kr = proj(C_KR, C_GA)
    k1 = kv[:, 0:4 * LANES] + jnp.concatenate([kr] * MLA_HEADS, axis=1)
    k1 = _head_rms_128(k1, g_kn, MLA_QK_DIM)
    if rope:
        k1 = _rope(k1, ra_ref, MLA_ROPE_DIM // 4)
    k1_o[...] = k1.astype(BF16)
    v1t_o[...] = jnp.transpose(kv[:, 4 * LANES:]).astype(BF16)

    k2 = _head_rms_64(proj(C_K2, C_V2), g_k2)
    if rope:
        k2 = _rope(k2, rb_ref, GQA_HEAD_DIM // 4)
    lo = _lane_id(k2.shape) < GQA_HEAD_DIM
    k2r = pltpu.roll(k2, GQA_HEAD_DIM, 1)
    zero = jnp.zeros_like(k2)
    k2_o[:, 0 * LANES:1 * LANES] = jnp.where(lo, k2, zero).astype(BF16)
    k2_o[:, 1 * LANES:2 * LANES] = jnp.where(lo, zero, k2r).astype(BF16)
    k2_o[:, 2 * LANES:3 * LANES] = jnp.where(lo, k2r, zero).astype(BF16)
    k2_o[:, 3 * LANES:4 * LANES] = jnp.where(lo, zero, k2).astype(BF16)
    v2t = jnp.transpose(proj(C_V2, C_GB)).astype(BF16)
    hd = GQA_HEAD_DIM
    v2t_o[0 * hd:1 * hd, :] = v2t[0:hd]
    v2t_o[1 * hd:2 * hd, :] = v2t[0:hd]
    v2t_o[2 * hd:3 * hd, :] = v2t[hd:2 * hd]
    v2t_o[3 * hd:4 * hd, :] = v2t[hd:2 * hd]
    if kv_only:
        return

    cq = proj(C_CQ, C_CKV)
    msq = jnp.sum(cq * cq, axis=-1, keepdims=True) * (1.0 / MLA_Q_RANK)
    cqn = (cq * lax.rsqrt(msq + EPS) * g_q).astype(BF16)
    q1 = _head_rms_128(_dot(cqn, wuq_ref[...]), g_qn, MLA_QK_DIM)
    if rope:
        q1 = _rope(q1, ra_ref, MLA_ROPE_DIM // 4)
    q1_o[...] = (q1 * (MLA_QK_DIM ** -0.5)).astype(BF16)

    q2 = _head_rms_64(proj(C_Q2, C_K2), g_q2)
    if rope:
        q2 = _rope(q2, rb_ref, GQA_HEAD_DIM // 4)
    q2_o[...] = (q2 * (GQA_HEAD_DIM ** -0.5)).astype(BF16)

    sga_o[...] = _silu(proj(C_GA, C_Q2))
    sgb_o[...] = _silu(proj(C_GB, C_U))
    sgd_o[...] = _silu(proj(C_GD, C_END))
    f_o[...] = proj(C_F, C_GD).astype(BF16)

    u = proj(C_U, C_VC)
    vc = proj(C_VC, C_GC)
    sgc = _silu(proj(C_GC, C_F))
    mu = jnp.mean(vc, axis=-1, keepdims=True)
    var = jnp.mean(jnp.square(vc - mu), axis=-1, keepdims=True)
    vn = ((vc - mu) * lax.rsqrt(var + EPS) * ln_g + ln_b).astype(BF16)
    grp = lax.shift_right_logical(_lane_id((CHUNK, BRANCH_W)), 6)
    for c in range(ts // CHUNK):
        vnc = vn[c * CHUNK:(c + 1) * CHUNK, :]
        stacked = jnp.concatenate(
            [jnp.where(grp == g, vnc, jnp.zeros_like(vnc)) for g in range(CM_GROUPS)], axis=0)
        s = _dot(wcm_ref[...], stacked) + bcm_ref[...]
        rows = slice(c * CHUNK, (c + 1) * CHUNK)
        yc_o[rows, :] = (u[rows, :] * s * sgc[rows, :]).astype(BF16)


def _projection(xs, mod, lw, rope_tabs, *, kv_only):
    bsz, seq, _ = xs.shape
    ts = min(PROJ_TILE, seq)
    rope = rope_tabs is not None
    per_sample = mod.shape[0] == bsz

    def full(a):
        nd = a.ndim
        return pl.BlockSpec(a.shape, lambda b, i: (0,) * nd)

    in_specs = [
        pl.BlockSpec((None, ts, D_MODEL), lambda b, i: (b, i, 0)),
        pl.BlockSpec((None, 3, D_MODEL), (lambda b, i: (b, 0, 0)) if per_sample else (lambda b, i: (0, 0, 0))),
        full(lw["norm_g"]), full(lw["w_in"]), full(lw["pvec"]), full(lw["w_uq"]), full(lw["w_ukv"]),
        full(lw["w_cm"]), full(lw["b_cm"]),
    ]
    args = [xs, mod, lw["norm_g"], lw["w_in"], lw["pvec"], lw["w_uq"], lw["w_ukv"], lw["w_cm"], lw["b_cm"]]
    if rope:
        for t in rope_tabs:
            in_specs.append(pl.BlockSpec((3, ts, LANES), lambda b, i: (0, i, 0)))
            args.append(t)

    def rows_out(width, dtype):
        return (pl.BlockSpec((None, ts, width), lambda b, i: (b, i, 0)),
                jax.ShapeDtypeStruct((bsz, seq, width), dtype))

    def cols_out(height):
        return (pl.BlockSpec((None, height, ts), lambda b, i: (b, 0, i)),
                jax.ShapeDtypeStruct((bsz, height, seq), BF16))

    if kv_only:
        outs = [rows_out(512, BF16), cols_out(256), rows_out(512, BF16), cols_out(256)]
    else:
        outs = [rows_out(512, BF16), rows_out(512, BF16), cols_out(256),
                rows_out(256, BF16), rows_out(512, BF16), cols_out(256),
                rows_out(256, F32), rows_out(256, F32), rows_out(256, BF16), rows_out(256, BF16),
                rows_out(256, F32)]
    return pl.pallas_call(
        functools.partial(_proj_kernel, rope=rope, kv_only=kv_only, ts=ts),
        grid=(bsz, seq // ts),
        in_specs=in_specs,
        out_specs=[o[0] for o in outs],
        out_shape=[o[1] for o in outs],
        compiler_params=pltpu.CompilerParams(
            dimension_semantics=("arbitrary", "arbitrary"), vmem_limit_bytes=VMEM_LIMIT),
        name="projection",
    )(*args)


def _interleave(major, minor):
    out, j = [], 0
    for i, th in enumerate(major):
        out.append(th)
        want = ((i + 1) * len(minor)) // len(major)
        out.extend(minor[j:want])
        j = want
    return out


def _attn_kernel(*refs, n_src, tq, n_qt):
    it = iter(refs)
    q_refs = (next(it), next(it))
    g_ref = next(it)
    srcs = [(next(it), next(it), next(it)) for _ in range(n_src)]
    o_ref = next(it)
    vt_s = (next(it), next(it))
    s_bufs = ((next(it), next(it)), (next(it), next(it)))

    off = 0
    for _, _, vt in srcs:
        n = vt.shape[1]
        vt_s[0][0:MLA_V_DIM, off:off + n] = vt[0:MLA_V_DIM, :]
        vt_s[1][0:MLA_V_DIM, off:off + n] = vt[MLA_V_DIM:2 * MLA_V_DIM, :]
        off += n
    n_keys = off
    ones = jnp.ones((V_ROWS - MLA_V_DIM, n_keys), BF16)
    vt_s[0][MLA_V_DIM:V_ROWS, :] = ones
    vt_s[1][MLA_V_DIM:V_ROWS, :] = ones

    def rows(t):
        if isinstance(t, int):
            return slice(t * tq, (t + 1) * tq)
        return pl.ds(pl.multiple_of(t * tq, tq), tq)

    def score_thunks(t, slot, m_out):
        thunks = []
        for h in range(2):
            off = 0
            for src in srcs:
                n = src[h].shape[0]
                for lo in range(0, n, SCORE_ROWS):
                    hi = min(lo + SCORE_ROWS, n)

                    def th(h=h, src=src, lo=lo, hi=hi, off=off):
                        s = _dot_nt(src[h][lo:hi, :], q_refs[h][rows(t), :])
                        s_bufs[slot][h][off + lo:off + hi, :] = s
                        mx = jnp.max(s, axis=0, keepdims=True)
                        m_out[h] = mx if m_out[h] is None else jnp.maximum(m_out[h], mx)
                    thunks.append(th)
                off += n
        return thunks

    def value_thunks(t, slot, m_in):
        acc = [None, None]
        thunks = []
        for c in range(0, n_keys, EXP_ROWS):
            for h in range(2):
                def th(h=h, c=c):
                    p = jnp.exp(s_bufs[slot][h][c:c + EXP_ROWS, :] - m_in[h]).astype(BF16)
                    part = _dot(vt_s[h][:, c:c + EXP_ROWS], p)
                    acc[h] = part if acc[h] is None else acc[h] + part
                thunks.append(th)

        def fin():
            outs = [a[0:MLA_V_DIM, :] / a[MLA_V_DIM:MLA_V_DIM + 1, :] for a in acc]
            att = jnp.transpose(jnp.concatenate(outs, axis=0))
            o_ref[rows(t), :] = (att * g_ref[rows(t), :]).astype(BF16)
        thunks.append(fin)
        return thunks

    def run(thunks):
        for th in thunks:
            th()

    def stage(t_score, slot_score, t_val, slot_val, m_val):
        m_new = [None, None]
        run(_interleave(value_thunks(t_val, slot_val, m_val), score_thunks(t_score, slot_score, m_new)))
        return m_new

    m0 = [None, None]
    run(score_thunks(0, 0, m0))
    if n_qt > 1:
        def body(u, carry):
            m_a = stage(2 * u + 1, 1, 2 * u, 0, list(carry))
            m_b = stage(2 * u + 2, 0, 2 * u + 1, 1, m_a)
            return tuple(m_b)
        m_even = lax.fori_loop(0, n_qt // 2 - 1, body, tuple(m0))
        m_last = stage(n_qt - 1, 1, n_qt - 2, 0, list(m_even))
        run(value_thunks(n_qt - 1, 1, m_last))
    else:
        run(value_thunks(0, 0, m0))


def _attention(q, gate, sources, *, paired_q):
    bsz, sq, _ = q.shape
    tq = min(ATTN_TILE, sq)
    n_qt = sq // tq
    assert n_qt == 1 or n_qt % 2 == 0
    if paired_q:
        qe_map = qo_map = lambda b, j: (b, 0, j)
    else:
        qe_map = lambda b, j: (b, 0, 2 * j)
        qo_map = lambda b, j: (b, 0, 2 * j + 1)
    in_specs = [
        pl.BlockSpec((None, sq, LANES), qe_map),
        pl.BlockSpec((None, sq, LANES), qo_map),
        pl.BlockSpec((None, sq, LANES), lambda b, j: (b, 0, j)),
    ]
    args = [q, q, gate]
    total = 0
    for k, vt in sources:
        n = k.shape[1]
        total += n
        in_specs += [
            pl.BlockSpec((None, n, LANES), lambda b, j: (b, 0, 2 * j)),
            pl.BlockSpec((None, n, LANES), lambda b, j: (b, 0, 2 * j + 1)),
            pl.BlockSpec((None, LANES, n), lambda b, j: (b, j, 0)),
        ]
        args += [k, k, vt]
    scratch = [pltpu.VMEM((V_ROWS, total), BF16), pltpu.VMEM((V_ROWS, total), BF16)]
    scratch += [pltpu.VMEM((total, tq), F32) for _ in range(4)]
    return pl.pallas_call(
        functools.partial(_attn_kernel, n_src=len(sources), tq=tq, n_qt=n_qt),
        grid=(bsz, 2),
        in_specs=in_specs,
        out_specs=pl.BlockSpec((None, sq, LANES), lambda b, j: (b, 0, j)),
        out_shape=jax.ShapeDtypeStruct((bsz, sq, BRANCH_W), BF16),
        scratch_shapes=scratch,
        compiler_params=pltpu.CompilerParams(
            dimension_semantics=("arbitrary", "arbitrary"), vmem_limit_bytes=VMEM_LIMIT),
        name="attention",
    )(*args)


def _fourier_kernel(f_ref, g_ref, cm_ref, bc_ref, bs_ref, wf_ref, o_ref, *, seq):
    f = f_ref[...]
    g_cos = _dot(f, bc_ref[...]).astype(BF16)
    g_sin = _dot(f, bs_ref[...]).astype(BF16)
    y = _dot(cm_ref[:, 0:seq], g_cos) + _dot(cm_ref[:, seq:2 * seq], g_sin)
    y = (y * ((seq * FNET_GDIM) ** -0.5)).astype(BF16)
    o_ref[...] = (_dot(y, wf_ref[...]) * g_ref[...]).astype(BF16)


def _fourier(f, gate, dft, lw):
    bsz, seq, _ = f.shape
    cm, bc, bs = dft

    def const(a):
        return pl.BlockSpec(a.shape, lambda b: (0,) * a.ndim, pipeline_mode=pl.Buffered(1))

    return pl.pallas_call(
        functools.partial(_fourier_kernel, seq=seq),
        grid=(bsz,),
        in_specs=[
            pl.BlockSpec((None, seq, BRANCH_W), lambda b: (b, 0, 0)),
            pl.BlockSpec((None, seq, BRANCH_W), lambda b: (b, 0, 0)),
            const(cm), const(bc), const(bs), const(lw["fnet_w"]),
        ],
        out_specs=pl.BlockSpec((None, seq, BRANCH_W), lambda b: (b, 0, 0)),
        out_shape=jax.ShapeDtypeStruct((bsz, seq, BRANCH_W), BF16),
        compiler_params=pltpu.CompilerParams(
            dimension_semantics=("arbitrary",), vmem_limit_bytes=VMEM_LIMIT),
        name="fourier",
    )(f, gate, cm, bc, bs, lw["fnet_w"])


def _merge_kernel(x_ref, mod_ref, ya_ref, yb_ref, yc_ref, yd_ref, w_ref, o_ref):
    cat = jnp.concatenate([ya_ref[...], yb_ref[...], yc_ref[...], yd_ref[...]], axis=1)
    o_ref[...] = x_ref[...] + mod_ref[2:3, :] * _dot(cat, w_ref[...])


def _merge(xs, mod, ya, yb, yc, yd, w_out):
    bsz, seq, _ = xs.shape
    ts = min(PROJ_TILE, seq)
    per_sample = mod.shape[0] == bsz
    branch = pl.BlockSpec((None, ts, BRANCH_W), lambda b, i: (b, i, 0))
    return pl.pallas_call(
        _merge_kernel,
        grid=(bsz, seq // ts),
        in_specs=[
            pl.BlockSpec((None, ts, D_MODEL), lambda b, i: (b, i, 0)),
            pl.BlockSpec((None, 3, D_MODEL), (lambda b, i: (b, 0, 0)) if per_sample else (lambda b, i: (0, 0, 0))),
            branch, branch, branch, branch,
            pl.BlockSpec(w_out.shape, lambda b, i: (0, 0)),
        ],
        out_specs=pl.BlockSpec((None, ts, D_MODEL), lambda b, i: (b, i, 0)),
        out_shape=jax.ShapeDtypeStruct(xs.shape, F32),
        compiler_params=pltpu.CompilerParams(
            dimension_semantics=("arbitrary", "arbitrary"), vmem_limit_bytes=VMEM_LIMIT),
        name="merge",
    )(xs, mod, ya, yb, yc, yd, w_out)


def _rope_table(seq, segments, half):
    pos = jnp.arange(seq, dtype=jnp.int32)
    row = (pos // GRID_W).astype(F32)
    col = (pos % GRID_W).astype(F32)
    idx = np.zeros(LANES, np.int32)
    active = np.zeros(LANES, bool)
    use_col = np.zeros(LANES, bool)
    first = np.zeros(LANES, bool)
    dim = segments[0][1]
    for start, d, kind in segments:
        assert d == dim and d // 2 == half
        for t in range(d):
            idx[start + t] = t % (d // 2)
            active[start + t] = True
            use_col[start + t] = kind == 1
            first[start + t] = t < d // 2
    inv = ROPE_THETA ** (-jnp.arange(0, dim, 2, dtype=F32) / dim)
    inv_lane = inv[idx][None, :]
    ang = jnp.where(use_col[None, :], col[:, None], row[:, None]) * inv_lane
    act = active[None, :]
    cos = jnp.where(act, jnp.cos(ang), 1.0)
    sin = jnp.sin(ang)
    sin_a = jnp.where(act & first[None, :], -sin, 0.0)
    sin_b = jnp.where(act & ~first[None, :], sin, 0.0)
    return jnp.stack([cos, sin_a, sin_b]).astype(F32)


def _dft_tables(seq):
    k = jnp.arange(seq, dtype=jnp.int32)
    ang = ((k[:, None] * k[None, :]) % seq).astype(F32) * (2.0 * np.pi / seq)
    cm = jnp.concatenate([jnp.cos(ang), -jnp.sin(ang)], axis=1).astype(BF16)
    c = jnp.arange(BRANCH_W, dtype=jnp.int32)
    same = (c[:, None] // FNET_GDIM) == (c[None, :] // FNET_GDIM)
    cang = (((c[:, None] % FNET_GDIM) * (c[None, :] % FNET_GDIM)) % FNET_GDIM).astype(F32) * (
        2.0 * np.pi / FNET_GDIM)
    bc = jnp.where(same, jnp.cos(cang), 0.0).astype(BF16)
    bs = jnp.where(same, jnp.sin(cang), 0.0).astype(BF16)
    return cm, bc, bs


def _pad_cols(a, width):
    return jnp.pad(a, ((0, 0), (0, width - a.shape[1])))


def _layer_weights(l, norm_g, w_in, mla_q_norm, mla_w_uq, mla_kv_norm, mla_w_ukv, mla_qn, mla_kn,
                   gqa_qn, gqa_kn, cm_ln_g, cm_ln_b, cm_w_s, cm_b_s, fnet_w, w_out):
    w = w_in[l]
    sp = np.cumsum([0, 192, 128, 32, 256, 256, 128, 128, 256, 256, 256, 256, 256, 256])
    nat = [w[:, sp[i]:sp[i + 1]] for i in range(13)]
    cq, ckv, kr, ga, q2, k2, v2, gb, u, vc, gc, f, gd = nat
    kr_p = jnp.pad(kr, ((0, 0), (MLA_NOPE_DIM, LANES - MLA_NOPE_DIM - MLA_ROPE_DIM)))
    w_in_p = jnp.concatenate(
        [_pad_cols(cq, 256), ckv, kr_p, ga, q2, k2, v2, gb, u, vc, gc, f, gd], axis=1).astype(BF16)
    assert w_in_p.shape[1] == C_END

    wuq = mla_w_uq[l].reshape(MLA_Q_RANK, MLA_HEADS, MLA_QK_DIM)
    wuq = jnp.pad(wuq, ((0, 256 - MLA_Q_RANK), (0, 0), (0, LANES - MLA_QK_DIM)))
    wuq = wuq.reshape(256, MLA_HEADS * LANES).astype(BF16)
    wukv = mla_w_ukv[l].reshape(MLA_KV_RANK, MLA_HEADS, MLA_NOPE_DIM + MLA_V_DIM)
    wk = jnp.pad(wukv[:, :, :MLA_NOPE_DIM], ((0, 0), (0, 0), (0, LANES - MLA_NOPE_DIM)))
    wv = wukv[:, :, MLA_NOPE_DIM:]
    wukv_p = jnp.concatenate(
        [wk.reshape(MLA_KV_RANK, MLA_HEADS * LANES), wv.reshape(MLA_KV_RANK, MLA_HEADS * MLA_V_DIM)],
        axis=1).astype(BF16)

    def row(v):
        return _pad_cols(v[None, :].astype(F32), 256)

    pvec = jnp.concatenate([
        row(mla_q_norm[l]), row(mla_kv_norm[l]), row(mla_qn[l]), row(mla_kn[l]),
        row(jnp.tile(gqa_qn[l], 2)), row(jnp.tile(gqa_kn[l], 2)), row(cm_ln_g[l]), row(cm_ln_b[l])], axis=0)

    w_cm = jnp.transpose(cm_w_s[l], (1, 0, 2)).reshape(CHUNK, CM_GROUPS * CHUNK).astype(BF16)
    b_cm = jnp.broadcast_to(cm_b_s[l].T[:, :, None], (CHUNK, CM_GROUPS, BRANCH_W // CM_GROUPS))
    b_cm = b_cm.reshape(CHUNK, BRANCH_W).astype(F32)
    return dict(norm_g=norm_g[l][None, :], w_in=w_in_p, pvec=pvec, w_uq=wuq, w_ukv=wukv_p,
                w_cm=w_cm, b_cm=b_cm, fnet_w=fnet_w[l].astype(BF16), w_out=w_out[l].astype(BF16))


def kernel(x, c, ctx, c_ctx, norm_g, w_mod, b_mod, w_in, mla_q_norm, mla_w_uq, mla_kv_norm, mla_w_ukv,
           mla_qn, mla_kn, gqa_qn, gqa_kn, cm_ln_g, cm_ln_b, cm_w_s, cm_b_s, fnet_w, w_out):
    bsz, seq, _ = x.shape
    ctx_len = ctx.shape[1]
    depth = w_in.shape[0]

    rows = -(-(bsz + 1) // 8) * 8
    cc = jnp.concatenate([c, c_ctx[None, :], jnp.zeros((rows - bsz - 1, D_MODEL), F32)], axis=0)
    mod_all = _modulation(cc, w_mod, b_mod)

    rope_a = _rope_table(seq, [(MLA_NOPE_DIM, 16, 0), (MLA_NOPE_DIM + 16, 16, 1)], 8)
    rope_b = _rope_table(seq, [(0, 32, 0), (32, 32, 1), (64, 32, 0), (96, 32, 1)], 16)
    dft_x = _dft_tables(seq)
    dft_c = _dft_tables(ctx_len)

    for l in range(depth):
        lw = _layer_weights(l, norm_g, w_in, mla_q_norm, mla_w_uq, mla_kv_norm, mla_w_ukv, mla_qn,
                            mla_kn, gqa_qn, gqa_kn, cm_ln_g, cm_ln_b, cm_w_s, cm_b_s, fnet_w, w_out)
        mod_x = mod_all[l, :bsz].reshape(bsz, 3, D_MODEL)
        mod_c = mod_all[l, bsz:bsz + 1].reshape(1, 3, D_MODEL)
        update_ctx = l < depth - 1

        (q1, k1, v1t, q2, k2, v2t, sga, sgb, yc, f, sgd) = _projection(
            x, mod_x, lw, (rope_a, rope_b), kv_only=False)
        if update_ctx:
            (q1c, k1c, v1tc, q2c, k2c, v2tc, sgac, sgbc, ycc, fc, sgdc) = _projection(
                ctx, mod_c, lw, None, kv_only=False)
        else:
            k1c, v1tc, k2c, v2tc = _projection(ctx, mod_c, lw, None, kv_only=True)

        ya = _attention(q1, sga, [(k1, v1t), (k1c, v1tc)], paired_q=False)
        yb = _attention(q2, sgb, [(k2, v2t), (k2c, v2tc)], paired_q=True)
        yd = _fourier(f, sgd, dft_x, lw)
        x = _merge(x, mod_x, ya, yb, yc, yd, lw["w_out"])
        if update_ctx:
            yac = _attention(q1c, sgac, [(k1c, v1tc)], paired_q=False)
            ybc = _attention(q2c, sgbc, [(k2c, v2tc)], paired_q=True)
            ydc = _fourier(fc, sgdc, dft_c, lw)
            ctx = _merge(ctx, mod_c, yac, ybc, ycc, ydc, lw["w_out"])
    return x
```

```python
import functools

import numpy as np
import jax
import jax.numpy as jnp
from jax import lax
from jax.experimental import pallas as pl
from jax.experimental.pallas import tpu as pltpu

D_MODEL = 1024
GRID_W = 64
BRANCH_W = 256
MLA_HEADS = 4
MLA_NOPE_DIM = 64
MLA_ROPE_DIM = 32
MLA_QK_DIM = 96
MLA_V_DIM = 64
MLA_Q_RANK = 192
MLA_KV_RANK = 128
GQA_HEADS = 4
GQA_KV_HEADS = 2
GQA_HEAD_DIM = 64
CHUNK = 128
CM_GROUPS = 4
FNET_GROUPS = 4
FNET_GDIM = BRANCH_W // FNET_GROUPS
ROPE_THETA = 10000.0
EPS = 1e-6

LANES = 128
V_ROWS = MLA_V_DIM + 16
PROJ_TILE = 512
MERGE_TILE = 1024
ATTN_TILE = 256
SCORE_ROWS = 512
EXP_ROWS = 256
VMEM_LIMIT = 48 * 1024 * 1024

F32 = jnp.float32
BF16 = jnp.bfloat16

C_CQ, C_CKV, C_KR, C_GA, C_Q2, C_K2, C_V2, C_GB, C_U, C_VC, C_GC, C_F, C_GD, C_END = (
    0, 256, 384, 512, 768, 1024, 1152, 1280, 1536, 1792, 2048, 2304, 2560, 2816)


def _dot(a, b):
    return jnp.dot(a, b, preferred_element_type=F32)


def _dot_nt(a, b):
    return lax.dot_general(a, b, (((1,), (1,)), ((), ())), preferred_element_type=F32)


def _silu(x):
    return x * jax.nn.sigmoid(x)


def _lane_id(shape):
    return lax.broadcasted_iota(jnp.int32, shape, len(shape) - 1)


def _mod_kernel(c_ref, w_ref, b_ref, o_ref):
    c = c_ref[...]
    o_ref[...] = _dot(_silu(c).astype(BF16), w_ref[...].astype(BF16)) + b_ref[...]


def _modulation(cc, w_mod, b_mod):
    n_layers = w_mod.shape[0]
    rows = cc.shape[0]
    return pl.pallas_call(
        _mod_kernel,
        grid=(n_layers, 3),
        in_specs=[
            pl.BlockSpec((rows, D_MODEL), lambda l, j: (0, 0)),
            pl.BlockSpec((None, D_MODEL, D_MODEL), lambda l, j: (l, 0, j)),
            pl.BlockSpec((None, 1, D_MODEL), lambda l, j: (l, 0, j)),
        ],
        out_specs=pl.BlockSpec((None, rows, D_MODEL), lambda l, j: (l, 0, j)),
        out_shape=jax.ShapeDtypeStruct((n_layers, rows, 3 * D_MODEL), F32),
        compiler_params=pltpu.CompilerParams(vmem_limit_bytes=VMEM_LIMIT),
        name="modulation",
    )(cc, w_mod, b_mod.reshape(n_layers, 1, 3 * D_MODEL))


def _rope(y, tab_ref, half):
    width = y.shape[1]
    reps = width // LANES
    cos, sin_a, sin_b = tab_ref[0], tab_ref[1], tab_ref[2]
    if reps > 1:
        cos = jnp.concatenate([cos] * reps, axis=1)
        sin_a = jnp.concatenate([sin_a] * reps, axis=1)
        sin_b = jnp.concatenate([sin_b] * reps, axis=1)
    return (y * cos + pltpu.roll(y, width - half, 1) * sin_a + pltpu.roll(y, half, 1) * sin_b)


def _head_rms_64(x, gain):
    outs = []
    for g in range(x.shape[1] // LANES):
        xg = x[:, g * LANES:(g + 1) * LANES]
        lo = _lane_id(xg.shape) < GQA_HEAD_DIM
        x2 = xg * xg
        s_lo = jnp.sum(jnp.where(lo, x2, 0.0), axis=-1, keepdims=True)
        s_hi = jnp.sum(jnp.where(lo, 0.0, x2), axis=-1, keepdims=True)
        ms = jnp.where(lo, s_lo, s_hi) * (1.0 / GQA_HEAD_DIM)
        outs.append(xg * lax.rsqrt(ms + EPS) * gain)
    return jnp.concatenate(outs, axis=1) if len(outs) > 1 else outs[0]


def _proj_kernel(*refs, rope, kv_only):
    it = iter(refs)
    x_ref, mod_ref, ng_ref, w_ref, pv_ref, wuq_ref, wukv_ref, wcm_ref, bcm_ref = (
        next(it) for _ in range(9))
    ra_ref = rb_ref = None
    if rope:
        ra_ref, rb_ref = next(it), next(it)
    if kv_only:
        k1_o, v1t_o, k2_o, v2t_o = (next(it) for _ in range(4))
    else:
        (q1_o, k1_o, v1t_o, q2_o, k2_o, v2t_o, sga_o, sgb_o, yc_o, f_o, sgd_o) = (
            next(it) for _ in range(11))

    x = x_ref[...]
    shift, scale = mod_ref[0:1, :], mod_ref[1:2, :]
    ms = jnp.mean(x * x, axis=-1, keepdims=True)
    h = (x * lax.rsqrt(ms + EPS) * ng_ref[...]) * (1.0 + scale) + shift
    hb = h.astype(BF16)

    def proj(lo, hi):
        return _dot(hb, w_ref[:, lo:hi])

    g_q, g_kv = pv_ref[0:1, :], pv_ref[1:2, 0:LANES]
    g_qn, g_kn = pv_ref[2:3, 0:LANES], pv_ref[3:4, 0:LANES]
    g_q2, g_k2 = pv_ref[4:5, 0:LANES], pv_ref[5:6, 0:LANES]
    ln_g, ln_b = pv_ref[6:7, :], pv_ref[7:8, :]
    g_qn_perm = pv_ref[8:9, 0:LANES]

    d = proj(C_CKV, C_GA)
    ckv, kr = d[:, 0:LANES], d[:, LANES:2 * LANES]
    msk = jnp.mean(ckv * ckv, axis=-1, keepdims=True)
    ckvn = (ckv * lax.rsqrt(msk + EPS) * g_kv).astype(BF16)
    kv = _dot(ckvn, wukv_ref[...])
    krg = kr * g_kn
    if rope:
        krg = _rope(krg, ra_ref, MLA_ROPE_DIM // 4)
    kr_sq = jnp.sum(kr * kr, axis=-1, keepdims=True)
    for hh in range(MLA_HEADS):
        kn = kv[:, hh * LANES:(hh + 1) * LANES]
        msh = (jnp.sum(kn * kn, axis=-1, keepdims=True) + kr_sq) * (1.0 / MLA_QK_DIM)
        k1_o[:, hh * LANES:(hh + 1) * LANES] = ((kn * g_kn + krg) * lax.rsqrt(msh + EPS)).astype(BF16)
    v1t_o[...] = jnp.transpose(kv[:, 4 * LANES:]).astype(BF16)

    d = proj(C_K2, C_GB)
    k2 = _head_rms_64(d[:, 0:LANES], g_k2)
    if rope:
        k2 = _rope(k2, rb_ref, GQA_HEAD_DIM // 4)
    lo = _lane_id(k2.shape) < GQA_HEAD_DIM
    k2r = pltpu.roll(k2, GQA_HEAD_DIM, 1)
    zero = jnp.zeros_like(k2)
    k2_o[:, 0 * LANES:1 * LANES] = jnp.where(lo, k2, zero).astype(BF16)
    k2_o[:, 1 * LANES:2 * LANES] = jnp.where(lo, zero, k2r).astype(BF16)
    k2_o[:, 2 * LANES:3 * LANES] = jnp.where(lo, k2r, zero).astype(BF16)
    k2_o[:, 3 * LANES:4 * LANES] = jnp.where(lo, zero, k2).astype(BF16)
    v2t = jnp.transpose(d[:, LANES:2 * LANES]).astype(BF16)
    hd = GQA_HEAD_DIM
    v2t_o[0 * hd:1 * hd, :] = v2t[0:hd]
    v2t_o[1 * hd:2 * hd, :] = v2t[0:hd]
    v2t_o[2 * hd:3 * hd, :] = v2t[hd:2 * hd]
    v2t_o[3 * hd:4 * hd, :] = v2t[hd:2 * hd]
    if kv_only:
        return

    cq = proj(C_CQ, C_CKV)
    msq = jnp.sum(cq * cq, axis=-1, keepdims=True) * (1.0 / MLA_Q_RANK)
    cqn = (cq * lax.rsqrt(msq + EPS) * g_q).astype(BF16)
    qq = _dot(cqn, wuq_ref[...])
    for hh in range(MLA_HEADS):
        qs = qq[:, hh * LANES:(hh + 1) * LANES]
        r = lax.rsqrt(jnp.sum(qs * qs, axis=-1, keepdims=True) * (1.0 / MLA_QK_DIM) + EPS)
        r = r * (MLA_QK_DIM ** -0.5)
        y = qs * g_qn
        if rope:
            qp = qq[:, (MLA_HEADS + hh) * LANES:(MLA_HEADS + hh + 1) * LANES]
            y = y * ra_ref[0] + (qp * g_qn_perm) * (ra_ref[1] + ra_ref[2])
        q1_o[:, hh * LANES:(hh + 1) * LANES] = (y * r).astype(BF16)

    q2 = _head_rms_64(proj(C_Q2, C_K2), g_q2)
    if rope:
        q2 = _rope(q2, rb_ref, GQA_HEAD_DIM // 4)
    q2_o[...] = (q2 * (GQA_HEAD_DIM ** -0.5)).astype(BF16)

    sga_o[...] = _silu(proj(C_GA, C_Q2)).astype(BF16)
    sgb_o[...] = _silu(proj(C_GB, C_U)).astype(BF16)
    d = proj(C_F, C_END)
    f_o[...] = d[:, 0:BRANCH_W].astype(BF16)
    sgd_o[...] = _silu(d[:, BRANCH_W:]).astype(BF16)

    d = proj(C_U, C_F)
    u, vc = d[:, 0:BRANCH_W], d[:, BRANCH_W:2 * BRANCH_W]
    sgc = _silu(d[:, 2 * BRANCH_W:])
    mu = jnp.mean(vc, axis=-1, keepdims=True)
    var = jnp.mean(jnp.square(vc - mu), axis=-1, keepdims=True)
    vn = ((vc - mu) * lax.rsqrt(var + EPS) * ln_g + ln_b).astype(BF16)
    grp = lax.shift_right_logical(_lane_id((CHUNK, BRANCH_W)), 6)
    for c in range(x.shape[0] // CHUNK):
        cr = slice(c * CHUNK, (c + 1) * CHUNK)
        vnc = vn[cr, :]
        stacked = jnp.concatenate(
            [jnp.where(grp == g, vnc, jnp.zeros_like(vnc)) for g in range(CM_GROUPS)], axis=0)
        s = _dot(wcm_ref[...], stacked) + bcm_ref[...]
        yc_o[cr, :] = (u[cr, :] * s * sgc[cr, :]).astype(BF16)


def _projection(xs, mod, lw, rope_tabs, *, kv_only):
    bsz, seq, _ = xs.shape
    ts = min(PROJ_TILE, seq)
    rope = rope_tabs is not None
    per_sample = mod.shape[0] == bsz

    def full(a):
        nd = a.ndim
        return pl.BlockSpec(a.shape, lambda b, i: (0,) * nd)

    in_specs = [
        pl.BlockSpec((None, ts, D_MODEL), lambda b, i: (b, i, 0)),
        pl.BlockSpec((None, 3, D_MODEL), (lambda b, i: (b, 0, 0)) if per_sample else (lambda b, i: (0, 0, 0))),
        full(lw["norm_g"]), full(lw["w_in"]), full(lw["pvec"]), full(lw["w_uq"]), full(lw["w_ukv"]),
        full(lw["w_cm"]), full(lw["b_cm"]),
    ]
    args = [xs, mod, lw["norm_g"], lw["w_in"], lw["pvec"], lw["w_uq"], lw["w_ukv"], lw["w_cm"], lw["b_cm"]]
    if rope:
        for t in rope_tabs:
            in_specs.append(pl.BlockSpec((3, ts, LANES), lambda b, i: (0, i, 0)))
            args.append(t)

    def rows_out(width):
        return (pl.BlockSpec((None, ts, width), lambda b, i: (b, i, 0)),
                jax.ShapeDtypeStruct((bsz, seq, width), BF16))

    def cols_out(height):
        return (pl.BlockSpec((None, height, ts), lambda b, i: (b, 0, i)),
                jax.ShapeDtypeStruct((bsz, height, seq), BF16))

    if kv_only:
        outs = [rows_out(512), cols_out(256), rows_out(512), cols_out(256)]
    else:
        outs = [rows_out(512), rows_out(512), cols_out(256), rows_out(256), rows_out(512), cols_out(256),
                rows_out(256), rows_out(256), rows_out(256), rows_out(256), rows_out(256)]
    return pl.pallas_call(
        functools.partial(_proj_kernel, rope=rope, kv_only=kv_only),
        grid=(bsz, seq // ts),
        in_specs=in_specs,
        out_specs=[o[0] for o in outs],
        out_shape=[o[1] for o in outs],
        compiler_params=pltpu.CompilerParams(
            dimension_semantics=("arbitrary", "arbitrary"), vmem_limit_bytes=VMEM_LIMIT),
        name="projection",
    )(*args)


def _interleave(major, minor):
    out, j = [], 0
    for i, th in enumerate(major):
        out.append(th)
        want = ((i + 1) * len(minor)) // len(major)
        out.extend(minor[j:want])
        j = want
    return out


def _attn_kernel(*refs, n_src, tq, n_qt):
    it = iter(refs)
    q_refs = (next(it), next(it))
    g_ref = next(it)
    srcs = [(next(it), next(it), next(it)) for _ in range(n_src)]
    o_ref = next(it)
    vt_s = (next(it), next(it))
    s_bufs = ((next(it), next(it)), (next(it), next(it)))

    off = 0
    for _, _, vt in srcs:
        n = vt.shape[1]
        vt_s[0][0:MLA_V_DIM, off:off + n] = vt[0:MLA_V_DIM, :]
        vt_s[1][0:MLA_V_DIM, off:off + n] = vt[MLA_V_DIM:2 * MLA_V_DIM, :]
        off += n
    n_keys = off
    ones = jnp.ones((V_ROWS - MLA_V_DIM, n_keys), BF16)
    vt_s[0][MLA_V_DIM:V_ROWS, :] = ones
    vt_s[1][MLA_V_DIM:V_ROWS, :] = ones

    def rows(t):
        if isinstance(t, int):
            return slice(t * tq, (t + 1) * tq)
        return pl.ds(pl.multiple_of(t * tq, tq), tq)

    def score_thunks(t, slot, m_out):
        thunks = []
        off = 0
        for src in srcs:
            n = src[0].shape[0]
            for lo in range(0, n, SCORE_ROWS):
                hi = min(lo + SCORE_ROWS, n)
                for h in range(2):
                    def th(h=h, src=src, lo=lo, hi=hi, off=off):
                        s = _dot_nt(src[h][lo:hi, :], q_refs[h][rows(t), :])
                        s_bufs[slot][h][off + lo:off + hi, :] = s
                        mx = jnp.max(s, axis=0, keepdims=True)
                        m_out[h] = mx if m_out[h] is None else jnp.maximum(m_out[h], mx)
                    thunks.append(th)
            off += n
        return thunks

    def value_thunks(t, slot, m_in):
        acc = [None, None]
        thunks = []
        for c in range(0, n_keys, EXP_ROWS):
            for h in range(2):
                def th(h=h, c=c):
                    p = jnp.exp(s_bufs[slot][h][c:c + EXP_ROWS, :] - m_in[h]).astype(BF16)
                    part = _dot(vt_s[h][:, c:c + EXP_ROWS], p)
                    acc[h] = part if acc[h] is None else acc[h] + part
                thunks.append(th)

        def fin():
            outs = [a[0:MLA_V_DIM, :] / a[MLA_V_DIM:MLA_V_DIM + 1, :] for a in acc]
            att = jnp.transpose(jnp.concatenate(outs, axis=0))
            o_ref[rows(t), :] = (att * g_ref[rows(t), :]).astype(BF16)
        thunks.append(fin)
        return thunks

    def run(thunks):
        for th in thunks:
            th()

    def stage(t_score, slot_score, t_val, slot_val, m_val):
        m_new = [None, None]
        run(_interleave(value_thunks(t_val, slot_val, m_val), score_thunks(t_score, slot_score, m_new)))
        return m_new

    m0 = [None, None]
    run(score_thunks(0, 0, m0))
    if n_qt > 1:
        def body(u, carry):
            m_a = stage(2 * u + 1, 1, 2 * u, 0, list(carry))
            m_b = stage(2 * u + 2, 0, 2 * u + 1, 1, m_a)
            return tuple(m_b)
        m_even = lax.fori_loop(0, n_qt // 2 - 1, body, tuple(m0))
        m_last = stage(n_qt - 1, 1, n_qt - 2, 0, list(m_even))
        run(value_thunks(n_qt - 1, 1, m_last))
    else:
        run(value_thunks(0, 0, m0))


def _attention(q, gate, sources, *, paired_q):
    bsz, sq, _ = q.shape
    tq = min(ATTN_TILE, sq)
    n_qt = sq // tq
    assert n_qt == 1 or n_qt % 2 == 0
    if paired_q:
        qe_map = qo_map = lambda b, j: (b, 0, j)
    else:
        qe_map = lambda b, j: (b, 0, 2 * j)
        qo_map = lambda b, j: (b, 0, 2 * j + 1)
    in_specs = [
        pl.BlockSpec((None, sq, LANES), qe_map),
        pl.BlockSpec((None, sq, LANES), qo_map),
        pl.BlockSpec((None, sq, LANES), lambda b, j: (b, 0, j)),
    ]
    args = [q, q, gate]
    total = 0
    for k, vt in sources:
        n = k.shape[1]
        total += n
        in_specs += [
            pl.BlockSpec((None, n, LANES), lambda b, j: (b, 0, 2 * j)),
            pl.BlockSpec((None, n, LANES), lambda b, j: (b, 0, 2 * j + 1)),
            pl.BlockSpec((None, LANES, n), lambda b, j: (b, j, 0)),
        ]
        args += [k, k, vt]
    scratch = [pltpu.VMEM((V_ROWS, total), BF16), pltpu.VMEM((V_ROWS, total), BF16)]
    scratch += [pltpu.VMEM((total, tq), F32) for _ in range(4)]
    return pl.pallas_call(
        functools.partial(_attn_kernel, n_src=len(sources), tq=tq, n_qt=n_qt),
        grid=(bsz, 2),
        in_specs=in_specs,
        out_specs=pl.BlockSpec((None, sq, LANES), lambda b, j: (b, 0, j)),
        out_shape=jax.ShapeDtypeStruct((bsz, sq, BRANCH_W), BF16),
        scratch_shapes=scratch,
        compiler_params=pltpu.CompilerParams(
            dimension_semantics=("arbitrary", "arbitrary"), vmem_limit_bytes=VMEM_LIMIT),
        name="attention",
    )(*args)


def _fourier_kernel(f_ref, g_ref, cm_ref, bc_ref, bs_ref, wf_ref, o_ref, *, seq):
    f = f_ref[...]
    g_cos = _dot(f, bc_ref[...]).astype(BF16)
    g_sin = _dot(f, bs_ref[...]).astype(BF16)
    y = _dot(cm_ref[:, 0:seq], g_cos) + _dot(cm_ref[:, seq:2 * seq], g_sin)
    y = (y * ((seq * FNET_GDIM) ** -0.5)).astype(BF16)
    o_ref[...] = (_dot(y, wf_ref[...]) * g_ref[...]).astype(BF16)


def _fourier(f, gate, dft, lw):
    bsz, seq, _ = f.shape
    cm, bc, bs = dft

    def const(a):
        return pl.BlockSpec(a.shape, lambda b: (0,) * a.ndim, pipeline_mode=pl.Buffered(1))

    return pl.pallas_call(
        functools.partial(_fourier_kernel, seq=seq),
        grid=(bsz,),
        in_specs=[
            pl.BlockSpec((None, seq, BRANCH_W), lambda b: (b, 0, 0)),
            pl.BlockSpec((None, seq, BRANCH_W), lambda b: (b, 0, 0)),
            const(cm), const(bc), const(bs), const(lw["fnet_w"]),
        ],
        out_specs=pl.BlockSpec((None, seq, BRANCH_W), lambda b: (b, 0, 0)),
        out_shape=jax.ShapeDtypeStruct((bsz, seq, BRANCH_W), BF16),
        compiler_params=pltpu.CompilerParams(
            dimension_semantics=("arbitrary",), vmem_limit_bytes=VMEM_LIMIT),
        name="fourier",
    )(f, gate, cm, bc, bs, lw["fnet_w"])


def _merge_kernel(x_ref, mod_ref, ya_ref, yb_ref, yc_ref, yd_ref, w_ref, o_ref):
    cat = jnp.concatenate([ya_ref[...], yb_ref[...], yc_ref[...], yd_ref[...]], axis=1)
    o_ref[...] = x_ref[...] + mod_ref[2:3, :] * _dot(cat, w_ref[...])


def _merge(xs, mod, ya, yb, yc, yd, w_out):
    bsz, seq, _ = xs.shape
    ts = min(MERGE_TILE, seq)
    per_sample = mod.shape[0] == bsz
    branch = pl.BlockSpec((None, ts, BRANCH_W), lambda b, i: (b, i, 0))
    return pl.pallas_call(
        _merge_kernel,
        grid=(bsz, seq // ts),
        in_specs=[
            pl.BlockSpec((None, ts, D_MODEL), lambda b, i: (b, i, 0)),
            pl.BlockSpec((None, 3, D_MODEL), (lambda b, i: (b, 0, 0)) if per_sample else (lambda b, i: (0, 0, 0))),
            branch, branch, branch, branch,
            pl.BlockSpec(w_out.shape, lambda b, i: (0, 0)),
        ],
        out_specs=pl.BlockSpec((None, ts, D_MODEL), lambda b, i: (b, i, 0)),
        out_shape=jax.ShapeDtypeStruct(xs.shape, F32),
        compiler_params=pltpu.CompilerParams(
            dimension_semantics=("arbitrary", "arbitrary"), vmem_limit_bytes=VMEM_LIMIT),
        name="merge",
    )(xs, mod, ya, yb, yc, yd, w_out)


def _rope_table(seq, segments, half):
    pos = jnp.arange(seq, dtype=jnp.int32)
    row = (pos // GRID_W).astype(F32)
    col = (pos % GRID_W).astype(F32)
    idx = np.zeros(LANES, np.int32)
    active = np.zeros(LANES, bool)
    use_col = np.zeros(LANES, bool)
    first = np.zeros(LANES, bool)
    dim = segments[0][1]
    for start, d, kind in segments:
        assert d == dim and d // 2 == half
        for t in range(d):
            idx[start + t] = t % (d // 2)
            active[start + t] = True
            use_col[start + t] = kind == 1
            first[start + t] = t < d // 2
    inv = ROPE_THETA ** (-jnp.arange(0, dim, 2, dtype=F32) / dim)
    inv_lane = inv[idx][None, :]
    ang = jnp.where(use_col[None, :], col[:, None], row[:, None]) * inv_lane
    act = active[None, :]
    cos = jnp.where(act, jnp.cos(ang), 1.0)
    sin = jnp.sin(ang)
    sin_a = jnp.where(act & first[None, :], -sin, 0.0)
    sin_b = jnp.where(act & ~first[None, :], sin, 0.0)
    return jnp.stack([cos, sin_a, sin_b]).astype(F32)


def _dft_tables(seq):
    def trig(prod, period):
        ang = (prod % period).astype(F32) * (2.0 * np.pi / period)
        return jnp.cos(ang), jnp.sin(ang)

    k = jnp.arange(seq, dtype=jnp.int32)
    ca, sa = trig(k[:, None] * (jnp.arange(seq // GRID_W, dtype=jnp.int32) * GRID_W)[None, :], seq)
    cb, sb = trig(k[:, None] * jnp.arange(GRID_W, dtype=jnp.int32)[None, :], seq)
    ca, sa, cb, sb = ca[:, :, None], sa[:, :, None], cb[:, None, :], sb[:, None, :]
    cos = (ca * cb - sa * sb).reshape(seq, seq)
    sin = (sa * cb + ca * sb).reshape(seq, seq)
    cm = jnp.concatenate([cos, -sin], axis=1).astype(BF16)
    c = jnp.arange(BRANCH_W, dtype=jnp.int32)
    same = (c[:, None] // FNET_GDIM) == (c[None, :] // FNET_GDIM)
    cc, cs = trig((c[:, None] % FNET_GDIM) * (c[None, :] % FNET_GDIM), FNET_GDIM)
    bc = jnp.where(same, cc, 0.0).astype(BF16)
    bs = jnp.where(same, cs, 0.0).astype(BF16)
    return cm, bc, bs


def _pad_cols(a, width):
    return jnp.pad(a, ((0, 0), (0, width - a.shape[1])))


def _layer_weights(l, norm_g, w_in, mla_q_norm, mla_w_uq, mla_kv_norm, mla_w_ukv, mla_qn, mla_kn,
                   gqa_qn, gqa_kn, cm_ln_g, cm_ln_b, cm_w_s, cm_b_s, fnet_w, w_out):
    w = w_in[l]
    sp = np.cumsum([0, 192, 128, 32, 256, 256, 128, 128, 256, 256, 256, 256, 256, 256])
    nat = [w[:, sp[i]:sp[i + 1]] for i in range(13)]
    cq, ckv, kr, ga, q2, k2, v2, gb, u, vc, gc, f, gd = nat
    kr_p = jnp.pad(kr, ((0, 0), (MLA_NOPE_DIM, LANES - MLA_NOPE_DIM - MLA_ROPE_DIM)))
    w_in_p = jnp.concatenate(
        [_pad_cols(cq, 256), ckv, kr_p, ga, q2, k2, v2, gb, u, vc, gc, f, gd], axis=1).astype(BF16)
    assert w_in_p.shape[1] == C_END

    seg = np.arange(MLA_ROPE_DIM) % 16
    partner = MLA_NOPE_DIM + (np.arange(MLA_ROPE_DIM) // 16) * 16 + (seg + 8) % 16
    perm = np.concatenate([np.arange(MLA_NOPE_DIM), partner])
    wuq = mla_w_uq[l].reshape(MLA_Q_RANK, MLA_HEADS, MLA_QK_DIM)
    wuq = jnp.concatenate([wuq, wuq[:, :, perm]], axis=1)
    wuq = jnp.pad(wuq, ((0, 256 - MLA_Q_RANK), (0, 0), (0, LANES - MLA_QK_DIM)))
    wuq = wuq.reshape(256, 2 * MLA_HEADS * LANES).astype(BF16)
    wukv = mla_w_ukv[l].reshape(MLA_KV_RANK, MLA_HEADS, MLA_NOPE_DIM + MLA_V_DIM)
    wk = jnp.pad(wukv[:, :, :MLA_NOPE_DIM], ((0, 0), (0, 0), (0, LANES - MLA_NOPE_DIM)))
    wv = wukv[:, :, MLA_NOPE_DIM:]
    wukv_p = jnp.concatenate(
        [wk.reshape(MLA_KV_RANK, MLA_HEADS * LANES), wv.reshape(MLA_KV_RANK, MLA_HEADS * MLA_V_DIM)],
        axis=1).astype(BF16)

    def row(v):
        return _pad_cols(v[None, :].astype(F32), 256)

    pvec = jnp.concatenate([
        row(mla_q_norm[l]), row(mla_kv_norm[l]), row(mla_qn[l]), row(mla_kn[l]),
        row(jnp.tile(gqa_qn[l], 2)), row(jnp.tile(gqa_kn[l], 2)), row(cm_ln_g[l]), row(cm_ln_b[l]),
        row(mla_qn[l][perm])] + [row(jnp.zeros((1,), F32))] * 7, axis=0)

    w_cm = jnp.transpose(cm_w_s[l], (1, 0, 2)).reshape(CHUNK, CM_GROUPS * CHUNK).astype(BF16)
    b_cm = jnp.broadcast_to(cm_b_s[l].T[:, :, None], (CHUNK, CM_GROUPS, BRANCH_W // CM_GROUPS))
    b_cm = b_cm.reshape(CHUNK, BRANCH_W).astype(F32)
    return dict(norm_g=norm_g[l][None, :], w_in=w_in_p, pvec=pvec, w_uq=wuq, w_ukv=wukv_p,
                w_cm=w_cm, b_cm=b_cm, fnet_w=fnet_w[l].astype(BF16), w_out=w_out[l].astype(BF16))


def kernel(x, c, ctx, c_ctx, norm_g, w_mod, b_mod, w_in, mla_q_norm, mla_w_uq, mla_kv_norm, mla_w_ukv,
           mla_qn, mla_kn, gqa_qn, gqa_kn, cm_ln_g, cm_ln_b, cm_w_s, cm_b_s, fnet_w, w_out):
    bsz, seq, _ = x.shape
    ctx_len = ctx.shape[1]
    depth = w_in.shape[0]

    rows = -(-(bsz + 1) // 8) * 8
    cc = jnp.concatenate([c, c_ctx[None, :], jnp.zeros((rows - bsz - 1, D_MODEL), F32)], axis=0)
    mod_all = _modulation(cc, w_mod, b_mod)

    rope_a = _rope_table(seq, [(MLA_NOPE_DIM, 16, 0), (MLA_NOPE_DIM + 16, 16, 1)], 8)
    rope_b = _rope_table(seq, [(0, 32, 0), (32, 32, 1), (64, 32, 0), (96, 32, 1)], 16)
    dft_x = _dft_tables(seq)
    dft_c = _dft_tables(ctx_len)

    for l in range(depth):
        lw = _layer_weights(l, norm_g, w_in, mla_q_norm, mla_w_uq, mla_kv_norm, mla_w_ukv, mla_qn,
                            mla_kn, gqa_qn, gqa_kn, cm_ln_g, cm_ln_b, cm_w_s, cm_b_s, fnet_w, w_out)
        mod_x = mod_all[l, :bsz].reshape(bsz, 3, D_MODEL)
        mod_c = mod_all[l, bsz:bsz + 1].reshape(1, 3, D_MODEL)
        update_ctx = l < depth - 1

        (q1, k1, v1t, q2, k2, v2t, sga, sgb, yc, f, sgd) = _projection(
            x, mod_x, lw, (rope_a, rope_b), kv_only=False)
        if update_ctx:
            (q1c, k1c, v1tc, q2c, k2c, v2tc, sgac, sgbc, ycc, fc, sgdc) = _projection(
                ctx, mod_c, lw, None, kv_only=False)
        else:
            k1c, v1tc, k2c, v2tc = _projection(ctx, mod_c, lw, None, kv_only=True)

        ya = _attention(q1, sga, [(k1, v1t), (k1c, v1tc)], paired_q=False)
        yb = _attention(q2, sgb, [(k2, v2t), (k2c, v2tc)], paired_q=True)
        yd = _fourier(f, sgd, dft_x, lw)
        x = _merge(x, mod_x, ya, yb, yc, yd, lw["w_out"])
        if update_ctx:
            yac = _attention(q1c, sgac, [(k1c, v1tc)], paired_q=False)
            ybc = _attention(q2c, sgbc, [(k2c, v2tc)], paired_q=True)
            ydc = _fourier(fc, sgdc, dft_c, lw)
            ctx = _merge(ctx, mod_c, yac, ybc, ycc, ydc, lw["w_out"])
    return x
```

```python
import functools

import numpy as np
import jax
import jax.numpy as jnp
from jax import lax
from jax.experimental import pallas as pl
from jax.experimental.pallas import tpu as pltpu

D_MODEL = 1024
GRID_W = 64
BRANCH_W = 256
MLA_HEADS = 4
MLA_NOPE_DIM = 64
MLA_ROPE_DIM = 32
MLA_QK_DIM = 96
MLA_V_DIM = 64
MLA_Q_RANK = 192
MLA_KV_RANK = 128
GQA_HEADS = 4
GQA_KV_HEADS = 2
GQA_HEAD_DIM = 64
CHUNK = 128
CM_GROUPS = 4
FNET_GROUPS = 4
FNET_GDIM = BRANCH_W // FNET_GROUPS
ROPE_THETA = 10000.0
EPS = 1e-6
LOG2_E = 1.4426950408889634

LANES = 128
V_ROWS = MLA_V_DIM + 16
PROJ_TILE = 512
MERGE_TILE = 1024
ATTN_TILE = 256
SCORE_ROWS = (256, 256)
EXP_ROWS = (256, 256)
VMEM_LIMIT = 48 * 1024 * 1024

F32 = jnp.float32
BF16 = jnp.bfloat16

C_CQ, C_CKV, C_KR, C_K2, C_V2, C_GB, C_GA, C_Q2, C_U, C_VC, C_GC, C_F, C_GD, C_END = (
    0, 256, 384, 512, 640, 768, 1024, 1280, 1536, 1792, 2048, 2304, 2560, 2816)


def _dot(a, b):
    return jnp.dot(a, b, preferred_element_type=F32)


def _dot_nt(a, b):
    return lax.dot_general(a, b, (((1,), (1,)), ((), ())), preferred_element_type=F32)


def _silu(x):
    return x * jax.nn.sigmoid(x)


def _lane_id(shape):
    return lax.broadcasted_iota(jnp.int32, shape, len(shape) - 1)


def _mod_kernel(c_ref, w_ref, b_ref, o_ref):
    c = c_ref[...]
    o_ref[...] = _dot(_silu(c).astype(BF16), w_ref[...].astype(BF16)) + b_ref[...]


def _modulation(cc, w_mod, b_mod):
    n_layers = w_mod.shape[0]
    rows = cc.shape[0]
    return pl.pallas_call(
        _mod_kernel,
        grid=(n_layers, 3),
        in_specs=[
            pl.BlockSpec((rows, D_MODEL), lambda l, j: (0, 0)),
            pl.BlockSpec((None, D_MODEL, D_MODEL), lambda l, j: (l, 0, j)),
            pl.BlockSpec((None, 1, D_MODEL), lambda l, j: (l, 0, j)),
        ],
        out_specs=pl.BlockSpec((None, rows, D_MODEL), lambda l, j: (l, 0, j)),
        out_shape=jax.ShapeDtypeStruct((n_layers, rows, 3 * D_MODEL), F32),
        compiler_params=pltpu.CompilerParams(vmem_limit_bytes=VMEM_LIMIT),
        name="modulation",
    )(cc, w_mod, b_mod.reshape(n_layers, 1, 3 * D_MODEL))


def _rope(y, tab_ref, half):
    width = y.shape[1]
    reps = width // LANES
    cos, sin_a, sin_b = tab_ref[0], tab_ref[1], tab_ref[2]
    if reps > 1:
        cos = jnp.concatenate([cos] * reps, axis=1)
        sin_a = jnp.concatenate([sin_a] * reps, axis=1)
        sin_b = jnp.concatenate([sin_b] * reps, axis=1)
    return (y * cos + pltpu.roll(y, width - half, 1) * sin_a + pltpu.roll(y, half, 1) * sin_b)


def _head_rms_64(x, gain):
    outs = []
    for g in range(x.shape[1] // LANES):
        xg = x[:, g * LANES:(g + 1) * LANES]
        lo = _lane_id(xg.shape) < GQA_HEAD_DIM
        x2 = xg * xg
        s_lo = jnp.sum(jnp.where(lo, x2, 0.0), axis=-1, keepdims=True)
        s_hi = jnp.sum(jnp.where(lo, 0.0, x2), axis=-1, keepdims=True)
        ms = jnp.where(lo, s_lo, s_hi) * (1.0 / GQA_HEAD_DIM)
        outs.append(xg * lax.rsqrt(ms + EPS) * gain)
    return jnp.concatenate(outs, axis=1) if len(outs) > 1 else outs[0]


def _proj_kernel(*refs, rope, kv_only):
    it = iter(refs)
    x_ref, mod_ref, ng_ref, w_ref, pv_ref, wuq_ref, wukv_ref, wcm_ref, bcm_ref = (
        next(it) for _ in range(9))
    ra_ref = rb_ref = None
    if rope:
        ra_ref, rb_ref = next(it), next(it)
    if kv_only:
        k1_o, v1t_o, k2_o, v2t_o = (next(it) for _ in range(4))
    else:
        (q1_o, k1_o, v1t_o, q2_o, k2_o, v2t_o, sga_o, sgb_o, yc_o, f_o, sgd_o) = (
            next(it) for _ in range(11))

    x = x_ref[...]
    shift, scale = mod_ref[0:1, :], mod_ref[1:2, :]
    ms = jnp.mean(x * x, axis=-1, keepdims=True)
    h = (x * lax.rsqrt(ms + EPS) * ng_ref[...]) * (1.0 + scale) + shift
    hb = h.astype(BF16)

    def proj(lo, hi):
        return _dot(hb, w_ref[:, lo:hi])

    g_q, g_kv = pv_ref[0:1, :], pv_ref[1:2, 0:LANES]
    g_qn, g_kn = pv_ref[2:3, 0:LANES], pv_ref[3:4, 0:LANES]
    g_q2, g_k2 = pv_ref[4:5, 0:LANES], pv_ref[5:6, 0:LANES]
    ln_g, ln_b = pv_ref[6:7, :], pv_ref[7:8, :]
    g_qn_perm = pv_ref[8:9, 0:LANES]

    d0 = proj(C_CQ, C_K2)
    cq = d0[:, 0:2 * LANES]
    ckv, kr = d0[:, 2 * LANES:3 * LANES], d0[:, 3 * LANES:4 * LANES]
    msk = jnp.mean(ckv * ckv, axis=-1, keepdims=True)
    ckvn = (ckv * lax.rsqrt(msk + EPS) * g_kv).astype(BF16)
    kv = _dot(ckvn, wukv_ref[...])
    krg = kr * g_kn
    if rope:
        krg = _rope(krg, ra_ref, MLA_ROPE_DIM // 4)
    kr_sq = jnp.sum(kr * kr, axis=-1, keepdims=True)
    for hh in range(MLA_HEADS):
        kn = kv[:, hh * LANES:(hh + 1) * LANES]
        msh = (jnp.sum(kn * kn, axis=-1, keepdims=True) + kr_sq) * (1.0 / MLA_QK_DIM)
        k1_o[:, hh * LANES:(hh + 1) * LANES] = ((kn * g_kn + krg) * lax.rsqrt(msh + EPS)).astype(BF16)
    v1t_o[...] = jnp.transpose(kv[:, 4 * LANES:]).astype(BF16)

    d1 = proj(C_K2, C_GA)
    k2 = _head_rms_64(d1[:, 0:LANES], g_k2)
    if rope:
        k2 = _rope(k2, rb_ref, GQA_HEAD_DIM // 4)
    lo = _lane_id(k2.shape) < GQA_HEAD_DIM
    k2r = pltpu.roll(k2, GQA_HEAD_DIM, 1)
    zero = jnp.zeros_like(k2)
    k2_o[:, 0 * LANES:1 * LANES] = jnp.where(lo, k2, zero).astype(BF16)
    k2_o[:, 1 * LANES:2 * LANES] = jnp.where(lo, zero, k2r).astype(BF16)
    k2_o[:, 2 * LANES:3 * LANES] = jnp.where(lo, k2r, zero).astype(BF16)
    k2_o[:, 3 * LANES:4 * LANES] = jnp.where(lo, zero, k2).astype(BF16)
    v2t = jnp.transpose(d1[:, LANES:2 * LANES]).astype(BF16)
    hd = GQA_HEAD_DIM
    v2t_o[0 * hd:1 * hd, :] = v2t[0:hd]
    v2t_o[1 * hd:2 * hd, :] = v2t[0:hd]
    v2t_o[2 * hd:3 * hd, :] = v2t[hd:2 * hd]
    v2t_o[3 * hd:4 * hd, :] = v2t[hd:2 * hd]
    if kv_only:
        return
    sgb_o[...] = _silu(d1[:, 2 * LANES:]).astype(BF16)

    msq = jnp.sum(cq * cq, axis=-1, keepdims=True) * (1.0 / MLA_Q_RANK)
    cqn = (cq * lax.rsqrt(msq + EPS) * g_q).astype(BF16)
    qq = _dot(cqn, wuq_ref[...])
    for hh in range(MLA_HEADS):
        qs = qq[:, hh * LANES:(hh + 1) * LANES]
        r = lax.rsqrt(jnp.sum(qs * qs, axis=-1, keepdims=True) * (1.0 / MLA_QK_DIM) + EPS)
        r = r * (LOG2_E * MLA_QK_DIM ** -0.5)
        y = qs * g_qn
        if rope:
            qp = qq[:, (MLA_HEADS + hh) * LANES:(MLA_HEADS + hh + 1) * LANES]
            y = y * ra_ref[0] + (qp * g_qn_perm) * (ra_ref[1] + ra_ref[2])
        q1_o[:, hh * LANES:(hh + 1) * LANES] = (y * r).astype(BF16)

    d2 = proj(C_GA, C_U)
    sga_o[...] = _silu(d2[:, 0:BRANCH_W]).astype(BF16)
    q2 = _head_rms_64(d2[:, BRANCH_W:], g_q2)
    if rope:
        q2 = _rope(q2, rb_ref, GQA_HEAD_DIM // 4)
    q2_o[...] = (q2 * (LOG2_E * GQA_HEAD_DIM ** -0.5)).astype(BF16)

    d3 = proj(C_U, C_GC)
    d4 = proj(C_GC, C_GD)
    u, vc = d3[:, 0:BRANCH_W], d3[:, BRANCH_W:]
    sgc = _silu(d4[:, 0:BRANCH_W])
    f_o[...] = d4[:, BRANCH_W:].astype(BF16)
    sgd_o[...] = _silu(proj(C_GD, C_END)).astype(BF16)
    mu = jnp.mean(vc, axis=-1, keepdims=True)
    var = jnp.mean(jnp.square(vc - mu), axis=-1, keepdims=True)
    vn = ((vc - mu) * lax.rsqrt(var + EPS) * ln_g + ln_b).astype(BF16)
    grp = lax.shift_right_logical(_lane_id((CHUNK, BRANCH_W)), 6)
    for c in range(x.shape[0] // CHUNK):
        cr = slice(c * CHUNK, (c + 1) * CHUNK)
        vnc = vn[cr, :]
        stacked = jnp.concatenate(
            [jnp.where(grp == g, vnc, jnp.zeros_like(vnc)) for g in range(CM_GROUPS)], axis=0)
        s = _dot(wcm_ref[...], stacked) + bcm_ref[...]
        yc_o[cr, :] = (u[cr, :] * s * sgc[cr, :]).astype(BF16)


def _projection(xs, mod, lw, rope_tabs, *, kv_only):
    bsz, seq, _ = xs.shape
    ts = min(PROJ_TILE, seq)
    rope = rope_tabs is not None
    per_sample = mod.shape[0] == bsz

    def full(a):
        nd = a.ndim
        return pl.BlockSpec(a.shape, lambda b, i: (0,) * nd)

    in_specs = [
        pl.BlockSpec((None, ts, D_MODEL), lambda b, i: (b, i, 0)),
        pl.BlockSpec((None, 3, D_MODEL), (lambda b, i: (b, 0, 0)) if per_sample else (lambda b, i: (0, 0, 0))),
        full(lw["norm_g"]), full(lw["w_in"]), full(lw["pvec"]), full(lw["w_uq"]), full(lw["w_ukv"]),
        full(lw["w_cm"]), full(lw["b_cm"]),
    ]
    args = [xs, mod, lw["norm_g"], lw["w_in"], lw["pvec"], lw["w_uq"], lw["w_ukv"], lw["w_cm"], lw["b_cm"]]
    if rope:
        for t in rope_tabs:
            in_specs.append(pl.BlockSpec((3, ts, LANES), lambda b, i: (0, i, 0)))
            args.append(t)

    def rows_out(width):
        return (pl.BlockSpec((None, ts, width), lambda b, i: (b, i, 0)),
                jax.ShapeDtypeStruct((bsz, seq, width), BF16))

    def cols_out(height):
        return (pl.BlockSpec((None, height, ts), lambda b, i: (b, 0, i)),
                jax.ShapeDtypeStruct((bsz, height, seq), BF16))

    if kv_only:
        outs = [rows_out(512), cols_out(256), rows_out(512), cols_out(256)]
    else:
        outs = [rows_out(512), rows_out(512), cols_out(256), rows_out(256), rows_out(512), cols_out(256),
                rows_out(256), rows_out(256), rows_out(256), rows_out(256), rows_out(256)]
    return pl.pallas_call(
        functools.partial(_proj_kernel, rope=rope, kv_only=kv_only),
        grid=(bsz, seq // ts),
        in_specs=in_specs,
        out_specs=[o[0] for o in outs],
        out_shape=[o[1] for o in outs],
        compiler_params=pltpu.CompilerParams(
            dimension_semantics=("arbitrary", "arbitrary"), vmem_limit_bytes=VMEM_LIMIT),
        name="projection",
    )(*args)


def _interleave(major, minor):
    out, j = [], 0
    for i, th in enumerate(major):
        out.append(th)
        want = ((i + 1) * len(minor)) // len(major)
        out.extend(minor[j:want])
        j = want
    return out


def _attn_kernel(*refs, n_src, tq, n_qt):
    carry = n_qt > 1
    it = iter(refs)
    q_refs = (next(it), next(it))
    g_ref = next(it)
    srcs = [(next(it), next(it), next(it)) for _ in range(n_src)]
    if carry:
        qn_refs = (next(it), next(it))
        ksrcs_next = [(next(it), next(it)) for _ in range(n_src)]
    o_ref = next(it)
    vt_s = (next(it), next(it))
    s_bufs = ((next(it), next(it)), (next(it), next(it)))
    m_s = (next(it), next(it)) if carry else None
    ksrcs = [(ke, ko) for ke, ko, _ in srcs]

    off = 0
    for _, _, vt in srcs:
        n = vt.shape[1]
        vt_s[0][0:MLA_V_DIM, off:off + n] = vt[0:MLA_V_DIM, :]
        vt_s[1][0:MLA_V_DIM, off:off + n] = vt[MLA_V_DIM:2 * MLA_V_DIM, :]
        off += n
    n_keys = off
    ones = jnp.ones((V_ROWS - MLA_V_DIM, n_keys), BF16)
    vt_s[0][MLA_V_DIM:V_ROWS, :] = ones
    vt_s[1][MLA_V_DIM:V_ROWS, :] = ones

    def rows(t):
        if isinstance(t, int):
            return slice(t * tq, (t + 1) * tq)
        return pl.ds(pl.multiple_of(t * tq, tq), tq)

    def score_thunks(q_of, keys, slot, m_out):
        per_head = ([], [])
        for h in range(2):
            off = 0
            for pair in keys:
                n = pair[h].shape[0]
                for lo in range(0, n, SCORE_ROWS[h]):
                    hi = min(lo + SCORE_ROWS[h], n)

                    def th(h=h, k_ref=pair[h], lo=lo, hi=hi, off=off):
                        s = _dot_nt(k_ref[lo:hi, :], q_of(h))
                        s_bufs[slot][h][off + lo:off + hi, :] = s
                        mx = jnp.max(s, axis=0, keepdims=True)
                        m_out[h] = mx if m_out[h] is None else jnp.maximum(m_out[h], mx)
                    per_head[h].append(th)
                off += n
        return _interleave(per_head[0], per_head[1])

    def own_scores(t, slot, m_out):
        return score_thunks(lambda h: q_refs[h][rows(t), :], ksrcs, slot, m_out)

    def value_thunks(t, slot, m_in):
        acc = [None, None]
        per_head = ([], [])
        for h in range(2):
            for c in range(0, n_keys, EXP_ROWS[h]):
                def th(h=h, c=c, e=min(c + EXP_ROWS[h], n_keys)):
                    p = jnp.exp2(s_bufs[slot][h][c:e, :] - m_in[h]).astype(BF16)
                    part = _dot(vt_s[h][:, c:e], p)
                    acc[h] = part if acc[h] is None else acc[h] + part
                per_head[h].append(th)
        thunks = _interleave(per_head[0], per_head[1])

        def fin():
            outs = [a[0:MLA_V_DIM, :] / a[MLA_V_DIM:MLA_V_DIM + 1, :] for a in acc]
            att = jnp.transpose(jnp.concatenate(outs, axis=0))
            o_ref[rows(t), :] = (att * g_ref[rows(t), :]).astype(BF16)
        thunks.append(fin)
        return thunks

    def run(thunks):
        for th in thunks:
            th()

    def stage(scores, t_val, slot_val, m_val):
        m_new = [None, None]
        run(_interleave(value_thunks(t_val, slot_val, m_val), scores(m_new)))
        return m_new

    if not carry:
        m0 = [None, None]
        run(own_scores(0, 0, m0))
        run(value_thunks(0, 0, m0))
        return

    @pl.when((pl.program_id(0) == 0) & (pl.program_id(1) == 0))
    def _():
        m_first = [None, None]
        run(own_scores(0, 0, m_first))
        for h in range(2):
            m_s[h][...] = m_first[h]

    def body(u, m_in):
        m_a = stage(lambda m: own_scores(2 * u + 1, 1, m), 2 * u, 0, list(m_in))
        m_b = stage(lambda m: own_scores(2 * u + 2, 0, m), 2 * u + 1, 1, m_a)
        return tuple(m_b)
    m_even = lax.fori_loop(0, n_qt // 2 - 1, body, (m_s[0][...], m_s[1][...]))
    m_last = stage(lambda m: own_scores(n_qt - 1, 1, m), n_qt - 2, 0, list(m_even))
    m_next = stage(lambda m: score_thunks(lambda h: qn_refs[h][...], ksrcs_next, 0, m), n_qt - 1, 1, m_last)
    for h in range(2):
        m_s[h][...] = m_next[h]


def _attention(q, gate, sources, *, paired_q):
    bsz, sq, _ = q.shape
    tq = min(ATTN_TILE, sq)
    n_qt = sq // tq
    assert n_qt == 1 or n_qt % 2 == 0

    def q_cols(j):
        return (j, j) if paired_q else (2 * j, 2 * j + 1)

    def following(b, j):
        g = jnp.minimum(2 * b + j + 1, 2 * bsz - 1)
        return g // 2, g % 2

    in_specs = [pl.BlockSpec((None, sq, LANES), lambda b, j, h=h: (b, 0, q_cols(j)[h])) for h in range(2)]
    in_specs.append(pl.BlockSpec((None, sq, LANES), lambda b, j: (b, 0, j)))
    args = [q, q, gate]
    total = 0
    for k, vt in sources:
        n = k.shape[1]
        total += n
        in_specs += [
            pl.BlockSpec((None, n, LANES), lambda b, j: (b, 0, 2 * j)),
            pl.BlockSpec((None, n, LANES), lambda b, j: (b, 0, 2 * j + 1)),
            pl.BlockSpec((None, LANES, n), lambda b, j: (b, j, 0)),
        ]
        args += [k, k, vt]
    scratch = [pltpu.VMEM((V_ROWS, total), BF16), pltpu.VMEM((V_ROWS, total), BF16)]
    scratch += [pltpu.VMEM((total, tq), F32) for _ in range(4)]
    if n_qt > 1:
        def next_q(b, j, h):
            bn, jn = following(b, j)
            return bn, 0, q_cols(jn)[h]

        def next_k(b, j, h):
            bn, jn = following(b, j)
            return bn, 0, 2 * jn + h

        in_specs += [pl.BlockSpec((None, tq, LANES), functools.partial(next_q, h=h)) for h in range(2)]
        args += [q, q]
        for k, _ in sources:
            in_specs += [pl.BlockSpec((None, k.shape[1], LANES), functools.partial(next_k, h=h))
                         for h in range(2)]
            args += [k, k]
        scratch += [pltpu.VMEM((1, tq), F32), pltpu.VMEM((1, tq), F32)]
    return pl.pallas_call(
        functools.partial(_attn_kernel, n_src=len(sources), tq=tq, n_qt=n_qt),
        grid=(bsz, 2),
        in_specs=in_specs,
        out_specs=pl.BlockSpec((None, sq, LANES), lambda b, j: (b, 0, j)),
        out_shape=jax.ShapeDtypeStruct((bsz, sq, BRANCH_W), BF16),
        scratch_shapes=scratch,
        compiler_params=pltpu.CompilerParams(
            dimension_semantics=("arbitrary", "arbitrary"), vmem_limit_bytes=VMEM_LIMIT),
        name="attention",
    )(*args)


def _fourier_kernel(f_ref, g_ref, cm_ref, bc_ref, bs_ref, wf_ref, o_ref, *, seq):
    f = f_ref[...]
    g_cos = _dot(f, bc_ref[...]).astype(BF16)
    g_sin = _dot(f, bs_ref[...]).astype(BF16)
    y = _dot(cm_ref[:, 0:seq], g_cos) + _dot(cm_ref[:, seq:2 * seq], g_sin)
    y = (y * ((seq * FNET_GDIM) ** -0.5)).astype(BF16)
    o_ref[...] = (_dot(y, wf_ref[...]) * g_ref[...]).astype(BF16)


def _fourier(f, gate, dft, lw):
    bsz, seq, _ = f.shape
    cm, bc, bs = dft

    def const(a):
        return pl.BlockSpec(a.shape, lambda b: (0,) * a.ndim, pipeline_mode=pl.Buffered(1))

    return pl.pallas_call(
        functools.partial(_fourier_kernel, seq=seq),
        grid=(bsz,),
        in_specs=[
            pl.BlockSpec((None, seq, BRANCH_W), lambda b: (b, 0, 0)),
            pl.BlockSpec((None, seq, BRANCH_W), lambda b: (b, 0, 0)),
            const(cm), const(bc), const(bs), const(lw["fnet_w"]),
        ],
        out_specs=pl.BlockSpec((None, seq, BRANCH_W), lambda b: (b, 0, 0)),
        out_shape=jax.ShapeDtypeStruct((bsz, seq, BRANCH_W), BF16),
        compiler_params=pltpu.CompilerParams(
            dimension_semantics=("arbitrary",), vmem_limit_bytes=VMEM_LIMIT),
        name="fourier",
    )(f, gate, cm, bc, bs, lw["fnet_w"])


def _merge_kernel(x_ref, mod_ref, ya_ref, yb_ref, yc_ref, yd_ref, w_ref, o_ref):
    cat = jnp.concatenate([ya_ref[...], yb_ref[...], yc_ref[...], yd_ref[...]], axis=1)
    o_ref[...] = x_ref[...] + mod_ref[2:3, :] * _dot(cat, w_ref[...])


def _merge(xs, mod, ya, yb, yc, yd, w_out):
    bsz, seq, _ = xs.shape
    ts = min(MERGE_TILE, seq)
    per_sample = mod.shape[0] == bsz
    branch = pl.BlockSpec((None, ts, BRANCH_W), lambda b, i: (b, i, 0))
    return pl.pallas_call(
        _merge_kernel,
        grid=(bsz, seq // ts),
        in_specs=[
            pl.BlockSpec((None, ts, D_MODEL), lambda b, i: (b, i, 0)),
            pl.BlockSpec((None, 3, D_MODEL), (lambda b, i: (b, 0, 0)) if per_sample else (lambda b, i: (0, 0, 0))),
            branch, branch, branch, branch,
            pl.BlockSpec(w_out.shape, lambda b, i: (0, 0)),
        ],
        out_specs=pl.BlockSpec((None, ts, D_MODEL), lambda b, i: (b, i, 0)),
        out_shape=jax.ShapeDtypeStruct(xs.shape, F32),
        compiler_params=pltpu.CompilerParams(
            dimension_semantics=("arbitrary", "arbitrary"), vmem_limit_bytes=VMEM_LIMIT),
        name="merge",
    )(xs, mod, ya, yb, yc, yd, w_out)


def _rope_table(seq, segments, half):
    pos = jnp.arange(seq, dtype=jnp.int32)
    row = (pos // GRID_W).astype(F32)
    col = (pos % GRID_W).astype(F32)
    idx = np.zeros(LANES, np.int32)
    active = np.zeros(LANES, bool)
    use_col = np.zeros(LANES, bool)
    first = np.zeros(LANES, bool)
    dim = segments[0][1]
    for start, d, kind in segments:
        assert d == dim and d // 2 == half
        for t in range(d):
            idx[start + t] = t % (d // 2)
            active[start + t] = True
            use_col[start + t] = kind == 1
            first[start + t] = t < d // 2
    inv = ROPE_THETA ** (-jnp.arange(0, dim, 2, dtype=F32) / dim)
    inv_lane = inv[idx][None, :]
    ang = jnp.where(use_col[None, :], col[:, None], row[:, None]) * inv_lane
    act = active[None, :]
    cos = jnp.where(act, jnp.cos(ang), 1.0)
    sin = jnp.sin(ang)
    sin_a = jnp.where(act & first[None, :], -sin, 0.0)
    sin_b = jnp.where(act & ~first[None, :], sin, 0.0)
    return jnp.stack([cos, sin_a, sin_b]).astype(F32)


def _dft_tables(seq):
    def trig(prod, period):
        ang = (prod % period).astype(F32) * (2.0 * np.pi / period)
        return jnp.cos(ang), jnp.sin(ang)

    k = jnp.arange(seq, dtype=jnp.int32)
    ca, sa = trig(k[:, None] * (jnp.arange(seq // GRID_W, dtype=jnp.int32) * GRID_W)[None, :], seq)
    cb, sb = trig(k[:, None] * jnp.arange(GRID_W, dtype=jnp.int32)[None, :], seq)
    ca, sa, cb, sb = ca[:, :, None], sa[:, :, None], cb[:, None, :], sb[:, None, :]
    cos = (ca * cb - sa * sb).reshape(seq, seq)
    sin = (sa * cb + ca * sb).reshape(seq, seq)
    cm = jnp.concatenate([cos, -sin], axis=1).astype(BF16)
    c = jnp.arange(BRANCH_W, dtype=jnp.int32)
    same = (c[:, None] // FNET_GDIM) == (c[None, :] // FNET_GDIM)
    cc, cs = trig((c[:, None] % FNET_GDIM) * (c[None, :] % FNET_GDIM), FNET_GDIM)
    bc = jnp.where(same, cc, 0.0).astype(BF16)
    bs = jnp.where(same, cs, 0.0).astype(BF16)
    return cm, bc, bs


def _pad_cols(a, width):
    return jnp.pad(a, ((0, 0), (0, width - a.shape[1])))


def _layer_weights(l, norm_g, w_in, mla_q_norm, mla_w_uq, mla_kv_norm, mla_w_ukv, mla_qn, mla_kn,
                   gqa_qn, gqa_kn, cm_ln_g, cm_ln_b, cm_w_s, cm_b_s, fnet_w, w_out):
    w = w_in[l]
    sp = np.cumsum([0, 192, 128, 32, 256, 256, 128, 128, 256, 256, 256, 256, 256, 256])
    nat = [w[:, sp[i]:sp[i + 1]] for i in range(13)]
    cq, ckv, kr, ga, q2, k2, v2, gb, u, vc, gc, f, gd = nat
    kr_p = jnp.pad(kr, ((0, 0), (MLA_NOPE_DIM, LANES - MLA_NOPE_DIM - MLA_ROPE_DIM)))
    w_in_p = jnp.concatenate(
        [_pad_cols(cq, 256), ckv, kr_p, k2, v2, gb, ga, q2, u, vc, gc, f, gd], axis=1).astype(BF16)
    assert w_in_p.shape[1] == C_END

    seg = np.arange(MLA_ROPE_DIM) % 16
    partner = MLA_NOPE_DIM + (np.arange(MLA_ROPE_DIM) // 16) * 16 + (seg + 8) % 16
    perm = np.concatenate([np.arange(MLA_NOPE_DIM), partner])
    wuq = mla_w_uq[l].reshape(MLA_Q_RANK, MLA_HEADS, MLA_QK_DIM)
    wuq = jnp.concatenate([wuq, wuq[:, :, perm]], axis=1)
    wuq = jnp.pad(wuq, ((0, 256 - MLA_Q_RANK), (0, 0), (0, LANES - MLA_QK_DIM)))
    wuq = wuq.reshape(256, 2 * MLA_HEADS * LANES).astype(BF16)
    wukv = mla_w_ukv[l].reshape(MLA_KV_RANK, MLA_HEADS, MLA_NOPE_DIM + MLA_V_DIM)
    wk = jnp.pad(wukv[:, :, :MLA_NOPE_DIM], ((0, 0), (0, 0), (0, LANES - MLA_NOPE_DIM)))
    wv = wukv[:, :, MLA_NOPE_DIM:]
    wukv_p = jnp.concatenate(
        [wk.reshape(MLA_KV_RANK, MLA_HEADS * LANES), wv.reshape(MLA_KV_RANK, MLA_HEADS * MLA_V_DIM)],
        axis=1).astype(BF16)

    def row(v):
        return _pad_cols(v[None, :].astype(F32), 256)

    pvec = jnp.concatenate([
        row(mla_q_norm[l]), row(mla_kv_norm[l]), row(mla_qn[l]), row(mla_kn[l]),
        row(jnp.tile(gqa_qn[l], 2)), row(jnp.tile(gqa_kn[l], 2)), row(cm_ln_g[l]), row(cm_ln_b[l]),
        row(mla_qn[l][perm])] + [row(jnp.zeros((1,), F32))] * 7, axis=0)

    w_cm = jnp.transpose(cm_w_s[l], (1, 0, 2)).reshape(CHUNK, CM_GROUPS * CHUNK).astype(BF16)
    b_cm = jnp.broadcast_to(cm_b_s[l].T[:, :, None], (CHUNK, CM_GROUPS, BRANCH_W // CM_GROUPS))
    b_cm = b_cm.reshape(CHUNK, BRANCH_W).astype(F32)
    return dict(norm_g=norm_g[l][None, :], w_in=w_in_p, pvec=pvec, w_uq=wuq, w_ukv=wukv_p,
                w_cm=w_cm, b_cm=b_cm, fnet_w=fnet_w[l].astype(BF16), w_out=w_out[l].astype(BF16))


def kernel(x, c, ctx, c_ctx, norm_g, w_mod, b_mod, w_in, mla_q_norm, mla_w_uq, mla_kv_norm, mla_w_ukv,
           mla_qn, mla_kn, gqa_qn, gqa_kn, cm_ln_g, cm_ln_b, cm_w_s, cm_b_s, fnet_w, w_out):
    bsz, seq, _ = x.shape
    ctx_len = ctx.shape[1]
    depth = w_in.shape[0]

    rows = -(-(bsz + 1) // 8) * 8
    cc = jnp.concatenate([c, c_ctx[None, :], jnp.zeros((rows - bsz - 1, D_MODEL), F32)], axis=0)
    mod_all = _modulation(cc, w_mod, b_mod)

    rope_a = _rope_table(seq, [(MLA_NOPE_DIM, 16, 0), (MLA_NOPE_DIM + 16, 16, 1)], 8)
    rope_b = _rope_table(seq, [(0, 32, 0), (32, 32, 1), (64, 32, 0), (96, 32, 1)], 16)
    dft_x = _dft_tables(seq)
    dft_c = _dft_tables(ctx_len)

    for l in range(depth):
        lw = _layer_weights(l, norm_g, w_in, mla_q_norm, mla_w_uq, mla_kv_norm, mla_w_ukv, mla_qn,
                            mla_kn, gqa_qn, gqa_kn, cm_ln_g, cm_ln_b, cm_w_s, cm_b_s, fnet_w, w_out)
        mod_x = mod_all[l, :bsz].reshape(bsz, 3, D_MODEL)
        mod_c = mod_all[l, bsz:bsz + 1].reshape(1, 3, D_MODEL)
        update_ctx = l < depth - 1

        (q1, k1, v1t, q2, k2, v2t, sga, sgb, yc, f, sgd) = _projection(
            x, mod_x, lw, (rope_a, rope_b), kv_only=False)
        if update_ctx:
            (q1c, k1c, v1tc, q2c, k2c, v2tc, sgac, sgbc, ycc, fc, sgdc) = _projection(
                ctx, mod_c, lw, None, kv_only=False)
        else:
            k1c, v1tc, k2c, v2tc = _projection(ctx, mod_c, lw, None, kv_only=True)

        ya = _attention(q1, sga, [(k1, v1t), (k1c, v1tc)], paired_q=False)
        yb = _attention(q2, sgb, [(k2, v2t), (k2c, v2tc)], paired_q=True)
        yd = _fourier(f, sgd, dft_x, lw)
        x = _merge(x, mod_x, ya, yb, yc, yd, lw["w_out"])
        if update_ctx:
            yac = _attention(q1c, sgac, [(k1c, v1tc)], paired_q=False)
            ybc = _attention(q2c, sgbc, [(k2c, v2tc)], paired_q=True)
            ydc = _fourier(fc, sgdc, dft_c, lw)
            ctx = _merge(ctx, mod_c, yac, ybc, ycc, ydc, lw["w_out"])
    return x
```

```python
import functools

import numpy as np
import jax
import jax.numpy as jnp
from jax import lax
from jax.experimental import pallas as pl
from jax.experimental.pallas import tpu as pltpu

D_MODEL = 1024
GRID_W = 64
BRANCH_W = 256
MLA_HEADS = 4
MLA_NOPE_DIM = 64
MLA_ROPE_DIM = 32
MLA_QK_DIM = 96
MLA_V_DIM = 64
MLA_Q_RANK = 192
MLA_KV_RANK = 128
GQA_HEADS = 4
GQA_KV_HEADS = 2
GQA_HEAD_DIM = 64
CHUNK = 128
CM_GROUPS = 4
FNET_GROUPS = 4
FNET_GDIM = BRANCH_W // FNET_GROUPS
ROPE_THETA = 10000.0
EPS = 1e-6
LOG2_E = 1.4426950408889634

LANES = 128
V_ROWS = MLA_V_DIM + 16
PROJ_TILE = 512
MERGE_TILE = 1024
DFT_BLOCK = 256
ATTN_TILE = 256
SCORE_ROWS = (256, 256)
EXP_ROWS = (256, 256)
VMEM_LIMIT = 48 * 1024 * 1024

F32 = jnp.float32
BF16 = jnp.bfloat16

C_CQ, C_CKV, C_KR, C_K2, C_V2, C_GB, C_GA, C_Q2, C_U, C_VC, C_GC, C_F, C_GD, C_END = (
    0, 256, 384, 512, 640, 768, 1024, 1280, 1536, 1792, 2048, 2304, 2560, 2816)


def _dot(a, b):
    return jnp.dot(a, b, preferred_element_type=F32)


def _dot_nt(a, b):
    return lax.dot_general(a, b, (((1,), (1,)), ((), ())), preferred_element_type=F32)


def _silu(x):
    return x * jax.nn.sigmoid(x)


def _lane_id(shape):
    return lax.broadcasted_iota(jnp.int32, shape, len(shape) - 1)


def _mod_kernel(c_ref, w_ref, b_ref, o_ref):
    c = c_ref[...]
    o_ref[...] = _dot(_silu(c).astype(BF16), w_ref[...].astype(BF16)) + b_ref[...]


def _modulation(cc, w_mod, b_mod):
    n_layers = w_mod.shape[0]
    rows = cc.shape[0]
    return pl.pallas_call(
        _mod_kernel,
        grid=(n_layers, 3),
        in_specs=[
            pl.BlockSpec((rows, D_MODEL), lambda l, j: (0, 0)),
            pl.BlockSpec((None, D_MODEL, D_MODEL), lambda l, j: (l, 0, j)),
            pl.BlockSpec((None, 1, D_MODEL), lambda l, j: (l, 0, j)),
        ],
        out_specs=pl.BlockSpec((None, rows, D_MODEL), lambda l, j: (l, 0, j)),
        out_shape=jax.ShapeDtypeStruct((n_layers, rows, 3 * D_MODEL), F32),
        compiler_params=pltpu.CompilerParams(vmem_limit_bytes=VMEM_LIMIT),
        name="modulation",
    )(cc, w_mod, b_mod.reshape(n_layers, 1, 3 * D_MODEL))


def _rope(y, tab_ref, half):
    width = y.shape[1]
    reps = width // LANES
    cos, sin_a, sin_b = tab_ref[0], tab_ref[1], tab_ref[2]
    if reps > 1:
        cos = jnp.concatenate([cos] * reps, axis=1)
        sin_a = jnp.concatenate([sin_a] * reps, axis=1)
        sin_b = jnp.concatenate([sin_b] * reps, axis=1)
    return (y * cos + pltpu.roll(y, width - half, 1) * sin_a + pltpu.roll(y, half, 1) * sin_b)


def _head_rms_64(x, gain):
    outs = []
    for g in range(x.shape[1] // LANES):
        xg = x[:, g * LANES:(g + 1) * LANES]
        lo = _lane_id(xg.shape) < GQA_HEAD_DIM
        x2 = xg * xg
        s_lo = jnp.sum(jnp.where(lo, x2, 0.0), axis=-1, keepdims=True)
        s_hi = jnp.sum(jnp.where(lo, 0.0, x2), axis=-1, keepdims=True)
        ms = jnp.where(lo, s_lo, s_hi) * (1.0 / GQA_HEAD_DIM)
        outs.append(xg * lax.rsqrt(ms + EPS) * gain)
    return jnp.concatenate(outs, axis=1) if len(outs) > 1 else outs[0]


def _proj_kernel(*refs, rope, kv_only):
    it = iter(refs)
    x_ref, mod_ref, ng_ref, w_ref, pv_ref, wuq_ref, wukv_ref, wcm_ref, bcm_ref = (
        next(it) for _ in range(9))
    ra_ref = rb_ref = None
    if rope:
        ra_ref, rb_ref = next(it), next(it)
    if kv_only:
        k1_o, v1t_o, k2_o, v2t_o = (next(it) for _ in range(4))
    else:
        (q1_o, k1_o, v1t_o, q2_o, k2_o, v2t_o, sga_o, sgb_o, yc_o, f_o, sgd_o) = (
            next(it) for _ in range(11))

    x = x_ref[...]
    shift, scale = mod_ref[0:1, :], mod_ref[1:2, :]
    ms = jnp.mean(x * x, axis=-1, keepdims=True)
    h = (x * lax.rsqrt(ms + EPS) * ng_ref[...]) * (1.0 + scale) + shift
    hb = h.astype(BF16)

    def proj(lo, hi):
        return _dot(hb, w_ref[:, lo:hi])

    g_q, g_kv = pv_ref[0:1, :], pv_ref[1:2, 0:LANES]
    g_qn, g_kn = pv_ref[2:3, 0:LANES], pv_ref[3:4, 0:LANES]
    g_q2, g_k2 = pv_ref[4:5, 0:LANES], pv_ref[5:6, 0:LANES]
    ln_g, ln_b = pv_ref[6:7, :], pv_ref[7:8, :]
    g_qn_perm = pv_ref[8:9, 0:LANES]

    d0 = proj(C_CQ, C_K2)
    cq = d0[:, 0:2 * LANES]
    ckv, kr = d0[:, 2 * LANES:3 * LANES], d0[:, 3 * LANES:4 * LANES]
    msk = jnp.mean(ckv * ckv, axis=-1, keepdims=True)
    ckvn = (ckv * lax.rsqrt(msk + EPS) * g_kv).astype(BF16)
    kv = _dot(ckvn, wukv_ref[...])
    krg = kr * g_kn
    if rope:
        krg = _rope(krg, ra_ref, MLA_ROPE_DIM // 4)
    kr_sq = jnp.sum(kr * kr, axis=-1, keepdims=True)
    for hh in range(MLA_HEADS):
        kn = kv[:, hh * LANES:(hh + 1) * LANES]
        msh = (jnp.sum(kn * kn, axis=-1, keepdims=True) + kr_sq) * (1.0 / MLA_QK_DIM)
        k1_o[:, hh * LANES:(hh + 1) * LANES] = ((kn * g_kn + krg) * lax.rsqrt(msh + EPS)).astype(BF16)
    v1t_o[...] = jnp.transpose(kv[:, 4 * LANES:]).astype(BF16)

    d1 = proj(C_K2, C_GA)
    k2 = _head_rms_64(d1[:, 0:LANES], g_k2)
    if rope:
        k2 = _rope(k2, rb_ref, GQA_HEAD_DIM // 4)
    lo = _lane_id(k2.shape) < GQA_HEAD_DIM
    k2r = pltpu.roll(k2, GQA_HEAD_DIM, 1)
    zero = jnp.zeros_like(k2)
    k2_o[:, 0 * LANES:1 * LANES] = jnp.where(lo, k2, zero).astype(BF16)
    k2_o[:, 1 * LANES:2 * LANES] = jnp.where(lo, zero, k2r).astype(BF16)
    k2_o[:, 2 * LANES:3 * LANES] = jnp.where(lo, k2r, zero).astype(BF16)
    k2_o[:, 3 * LANES:4 * LANES] = jnp.where(lo, zero, k2).astype(BF16)
    v2t = jnp.transpose(d1[:, LANES:2 * LANES]).astype(BF16)
    hd = GQA_HEAD_DIM
    v2t_o[0 * hd:1 * hd, :] = v2t[0:hd]
    v2t_o[1 * hd:2 * hd, :] = v2t[0:hd]
    v2t_o[2 * hd:3 * hd, :] = v2t[hd:2 * hd]
    v2t_o[3 * hd:4 * hd, :] = v2t[hd:2 * hd]
    if kv_only:
        return
    sgb_o[...] = _silu(d1[:, 2 * LANES:]).astype(BF16)

    msq = jnp.sum(cq * cq, axis=-1, keepdims=True) * (1.0 / MLA_Q_RANK)
    cqn = (cq * lax.rsqrt(msq + EPS) * g_q).astype(BF16)
    qq = _dot(cqn, wuq_ref[...])
    for hh in range(MLA_HEADS):
        qs = qq[:, hh * LANES:(hh + 1) * LANES]
        r = lax.rsqrt(jnp.sum(qs * qs, axis=-1, keepdims=True) * (1.0 / MLA_QK_DIM) + EPS)
        r = r * (LOG2_E * MLA_QK_DIM ** -0.5)
        y = qs * g_qn
        if rope:
            qp = qq[:, (MLA_HEADS + hh) * LANES:(MLA_HEADS + hh + 1) * LANES]
            y = y * ra_ref[0] + (qp * g_qn_perm) * (ra_ref[1] + ra_ref[2])
        q1_o[:, hh * LANES:(hh + 1) * LANES] = (y * r).astype(BF16)

    d2 = proj(C_GA, C_U)
    sga_o[...] = _silu(d2[:, 0:BRANCH_W]).astype(BF16)
    q2 = _head_rms_64(d2[:, BRANCH_W:], g_q2)
    if rope:
        q2 = _rope(q2, rb_ref, GQA_HEAD_DIM // 4)
    q2_o[...] = (q2 * (LOG2_E * GQA_HEAD_DIM ** -0.5)).astype(BF16)

    d3 = proj(C_U, C_GC)
    d4 = proj(C_GC, C_GD)
    u, vc = d3[:, 0:BRANCH_W], d3[:, BRANCH_W:]
    sgc = _silu(d4[:, 0:BRANCH_W])
    f_o[...] = d4[:, BRANCH_W:].astype(BF16)
    sgd_o[...] = _silu(proj(C_GD, C_END)).astype(BF16)
    mu = jnp.mean(vc, axis=-1, keepdims=True)
    var = jnp.mean(jnp.square(vc - mu), axis=-1, keepdims=True)
    vn = ((vc - mu) * lax.rsqrt(var + EPS) * ln_g + ln_b).astype(BF16)
    grp = lax.shift_right_logical(_lane_id((CHUNK, BRANCH_W)), 6)
    for c in range(x.shape[0] // CHUNK):
        cr = slice(c * CHUNK, (c + 1) * CHUNK)
        vnc = vn[cr, :]
        stacked = jnp.concatenate(
            [jnp.where(grp == g, vnc, jnp.zeros_like(vnc)) for g in range(CM_GROUPS)], axis=0)
        s = _dot(wcm_ref[...], stacked) + bcm_ref[...]
        yc_o[cr, :] = (u[cr, :] * s * sgc[cr, :]).astype(BF16)


def _projection(xs, mod, lw, rope_tabs, *, kv_only):
    bsz, seq, _ = xs.shape
    ts = min(PROJ_TILE, seq)
    rope = rope_tabs is not None
    per_sample = mod.shape[0] == bsz

    def full(a):
        nd = a.ndim
        return pl.BlockSpec(a.shape, lambda b, i: (0,) * nd)

    in_specs = [
        pl.BlockSpec((None, ts, D_MODEL), lambda b, i: (b, i, 0)),
        pl.BlockSpec((None, 3, D_MODEL), (lambda b, i: (b, 0, 0)) if per_sample else (lambda b, i: (0, 0, 0))),
        full(lw["norm_g"]), full(lw["w_in"]), full(lw["pvec"]), full(lw["w_uq"]), full(lw["w_ukv"]),
        full(lw["w_cm"]), full(lw["b_cm"]),
    ]
    args = [xs, mod, lw["norm_g"], lw["w_in"], lw["pvec"], lw["w_uq"], lw["w_ukv"], lw["w_cm"], lw["b_cm"]]
    if rope:
        for t in rope_tabs:
            in_specs.append(pl.BlockSpec((3, ts, LANES), lambda b, i: (0, i, 0)))
            args.append(t)

    def rows_out(width):
        return (pl.BlockSpec((None, ts, width), lambda b, i: (b, i, 0)),
                jax.ShapeDtypeStruct((bsz, seq, width), BF16))

    def cols_out(height):
        return (pl.BlockSpec((None, height, ts), lambda b, i: (b, 0, i)),
                jax.ShapeDtypeStruct((bsz, height, seq), BF16))

    if kv_only:
        outs = [rows_out(512), cols_out(256), rows_out(512), cols_out(256)]
    else:
        outs = [rows_out(512), rows_out(512), cols_out(256), rows_out(256), rows_out(512), cols_out(256),
                rows_out(256), rows_out(256), rows_out(256), rows_out(256), rows_out(256)]
    return pl.pallas_call(
        functools.partial(_proj_kernel, rope=rope, kv_only=kv_only),
        grid=(bsz, seq // ts),
        in_specs=in_specs,
        out_specs=[o[0] for o in outs],
        out_shape=[o[1] for o in outs],
        compiler_params=pltpu.CompilerParams(
            dimension_semantics=("arbitrary", "arbitrary"), vmem_limit_bytes=VMEM_LIMIT),
        name="projection",
    )(*args)


def _interleave(major, minor):
    out, j = [], 0
    for i, th in enumerate(major):
        out.append(th)
        want = ((i + 1) * len(minor)) // len(major)
        out.extend(minor[j:want])
        j = want
    return out


def _attn_kernel(*refs, n_src, tq, n_qt):
    carry = n_qt > 1
    it = iter(refs)
    q_refs = (next(it), next(it))
    g_ref = next(it)
    srcs = [(next(it), next(it), next(it)) for _ in range(n_src)]
    if carry:
        qn_refs = (next(it), next(it))
        ksrcs_next = [(next(it), next(it)) for _ in range(n_src)]
    o_ref = next(it)
    vt_s = (next(it), next(it))
    s_bufs = ((next(it), next(it)), (next(it), next(it)))
    m_s = (next(it), next(it)) if carry else None
    ksrcs = [(ke, ko) for ke, ko, _ in srcs]

    off = 0
    for _, _, vt in srcs:
        n = vt.shape[1]
        vt_s[0][0:MLA_V_DIM, off:off + n] = vt[0:MLA_V_DIM, :]
        vt_s[1][0:MLA_V_DIM, off:off + n] = vt[MLA_V_DIM:2 * MLA_V_DIM, :]
        off += n
    n_keys = off
    ones = jnp.ones((V_ROWS - MLA_V_DIM, n_keys), BF16)
    vt_s[0][MLA_V_DIM:V_ROWS, :] = ones
    vt_s[1][MLA_V_DIM:V_ROWS, :] = ones

    def rows(t):
        if isinstance(t, int):
            return slice(t * tq, (t + 1) * tq)
        return pl.ds(pl.multiple_of(t * tq, tq), tq)

    def score_thunks(q_of, keys, slot, m_out):
        per_head = ([], [])
        for h in range(2):
            off = 0
            for pair in keys:
                n = pair[h].shape[0]
                for lo in range(0, n, SCORE_ROWS[h]):
                    hi = min(lo + SCORE_ROWS[h], n)

                    def th(h=h, k_ref=pair[h], lo=lo, hi=hi, off=off):
                        s = _dot_nt(k_ref[lo:hi, :], q_of(h))
                        s_bufs[slot][h][off + lo:off + hi, :] = s
                        mx = jnp.max(s, axis=0, keepdims=True)
                        m_out[h] = mx if m_out[h] is None else jnp.maximum(m_out[h], mx)
                    per_head[h].append(th)
                off += n
        return _interleave(per_head[0], per_head[1])

    def own_scores(t, slot, m_out):
        return score_thunks(lambda h: q_refs[h][rows(t), :], ksrcs, slot, m_out)

    def value_thunks(t, slot, m_in):
        acc = [None, None]
        per_head = ([], [])
        for h in range(2):
            for c in range(0, n_keys, EXP_ROWS[h]):
                def th(h=h, c=c, e=min(c + EXP_ROWS[h], n_keys)):
                    p = jnp.exp2(s_bufs[slot][h][c:e, :] - m_in[h]).astype(BF16)
                    part = _dot(vt_s[h][:, c:e], p)
                    acc[h] = part if acc[h] is None else acc[h] + part
                per_head[h].append(th)
        thunks = _interleave(per_head[0], per_head[1])

        def fin():
            outs = [a[0:MLA_V_DIM, :] / a[MLA_V_DIM:MLA_V_DIM + 1, :] for a in acc]
            att = jnp.transpose(jnp.concatenate(outs, axis=0))
            o_ref[rows(t), :] = (att * g_ref[rows(t), :]).astype(BF16)
        thunks.append(fin)
        return thunks

    def run(thunks):
        for th in thunks:
            th()

    def stage(scores, t_val, slot_val, m_val):
        m_new = [None, None]
        run(_interleave(value_thunks(t_val, slot_val, m_val), scores(m_new)))
        return m_new

    if not carry:
        m0 = [None, None]
        run(own_scores(0, 0, m0))
        run(value_thunks(0, 0, m0))
        return

    @pl.when((pl.program_id(0) == 0) & (pl.program_id(1) == 0))
    def _():
        m_first = [None, None]
        run(own_scores(0, 0, m_first))
        for h in range(2):
            m_s[h][...] = m_first[h]

    def body(u, m_in):
        m_a = stage(lambda m: own_scores(2 * u + 1, 1, m), 2 * u, 0, list(m_in))
        m_b = stage(lambda m: own_scores(2 * u + 2, 0, m), 2 * u + 1, 1, m_a)
        return tuple(m_b)
    m_even = lax.fori_loop(0, n_qt // 2 - 1, body, (m_s[0][...], m_s[1][...]))
    m_last = stage(lambda m: own_scores(n_qt - 1, 1, m), n_qt - 2, 0, list(m_even))
    m_next = stage(lambda m: score_thunks(lambda h: qn_refs[h][...], ksrcs_next, 0, m), n_qt - 1, 1, m_last)
    for h in range(2):
        m_s[h][...] = m_next[h]


def _attention(q, gate, sources, *, paired_q):
    bsz, sq, _ = q.shape
    tq = min(ATTN_TILE, sq)
    n_qt = sq // tq
    assert n_qt == 1 or n_qt % 2 == 0

    def q_cols(j):
        return (j, j) if paired_q else (2 * j, 2 * j + 1)

    def following(b, j):
        g = jnp.minimum(2 * b + j + 1, 2 * bsz - 1)
        return g // 2, g % 2

    in_specs = [pl.BlockSpec((None, sq, LANES), lambda b, j, h=h: (b, 0, q_cols(j)[h])) for h in range(2)]
    in_specs.append(pl.BlockSpec((None, sq, LANES), lambda b, j: (b, 0, j)))
    args = [q, q, gate]
    total = 0
    for k, vt in sources:
        n = k.shape[1]
        total += n
        in_specs += [
            pl.BlockSpec((None, n, LANES), lambda b, j: (b, 0, 2 * j)),
            pl.BlockSpec((None, n, LANES), lambda b, j: (b, 0, 2 * j + 1)),
            pl.BlockSpec((None, LANES, n), lambda b, j: (b, j, 0)),
        ]
        args += [k, k, vt]
    scratch = [pltpu.VMEM((V_ROWS, total), BF16), pltpu.VMEM((V_ROWS, total), BF16)]
    scratch += [pltpu.VMEM((total, tq), F32) for _ in range(4)]
    if n_qt > 1:
        def next_q(b, j, h):
            bn, jn = following(b, j)
            return bn, 0, q_cols(jn)[h]

        def next_k(b, j, h):
            bn, jn = following(b, j)
            return bn, 0, 2 * jn + h

        in_specs += [pl.BlockSpec((None, tq, LANES), functools.partial(next_q, h=h)) for h in range(2)]
        args += [q, q]
        for k, _ in sources:
            in_specs += [pl.BlockSpec((None, k.shape[1], LANES), functools.partial(next_k, h=h))
                         for h in range(2)]
            args += [k, k]
        scratch += [pltpu.VMEM((1, tq), F32), pltpu.VMEM((1, tq), F32)]
    return pl.pallas_call(
        functools.partial(_attn_kernel, n_src=len(sources), tq=tq, n_qt=n_qt),
        grid=(bsz, 2),
        in_specs=in_specs,
        out_specs=pl.BlockSpec((None, sq, LANES), lambda b, j: (b, 0, j)),
        out_shape=jax.ShapeDtypeStruct((bsz, sq, BRANCH_W), BF16),
        scratch_shapes=scratch,
        compiler_params=pltpu.CompilerParams(
            dimension_semantics=("arbitrary", "arbitrary"), vmem_limit_bytes=VMEM_LIMIT),
        name="attention",
    )(*args)


def _fourier_kernel(f_ref, g_ref, cm_ref, alt_ref, rev_ref, bc_ref, bs_ref, wf_ref, o_ref, *, seq):
    half = seq // 2
    nblk = seq // DFT_BLOCK
    f = f_ref[...]
    g_cos = _dot(f, bc_ref[...]).astype(BF16)
    g_sin = _dot(f, bs_ref[...]).astype(BF16)
    y_cos = y_sin = None
    for a in range(nblk):
        rs = slice(a * DFT_BLOCK, (a + 1) * DFT_BLOCK)
        pc = _dot(cm_ref[a], g_cos[rs, :])
        ps = _dot(cm_ref[nblk + a], g_sin[rs, :])
        y_cos = pc if y_cos is None else y_cos + pc
        y_sin = ps if y_sin is None else y_sin + ps
    norm = (seq * FNET_GDIM) ** -0.5
    low = ((y_cos + y_sin) * norm).astype(BF16)
    diff = ((y_cos - y_sin) * norm).astype(BF16)
    y_mid = _dot(alt_ref[...], g_cos)[0:1, :] * norm
    high = _dot(rev_ref[...], diff)
    first_row = lax.broadcasted_iota(jnp.int32, high.shape, 0) == 0
    high = jnp.where(first_row, y_mid, high).astype(BF16)
    wf = wf_ref[...]
    o_ref[0:half, :] = (_dot(low, wf) * g_ref[0:half, :]).astype(BF16)
    o_ref[half:seq, :] = (_dot(high, wf) * g_ref[half:seq, :]).astype(BF16)


def _fourier(f, gate, dft, lw):
    bsz, seq, _ = f.shape

    def const(a):
        return pl.BlockSpec(a.shape, lambda b: (0,) * a.ndim, pipeline_mode=pl.Buffered(1))

    consts = list(dft) + [lw["fnet_w"]]
    return pl.pallas_call(
        functools.partial(_fourier_kernel, seq=seq),
        grid=(bsz,),
        in_specs=[
            pl.BlockSpec((None, seq, BRANCH_W), lambda b: (b, 0, 0)),
            pl.BlockSpec((None, seq, BRANCH_W), lambda b: (b, 0, 0)),
        ] + [const(a) for a in consts],
        out_specs=pl.BlockSpec((None, seq, BRANCH_W), lambda b: (b, 0, 0)),
        out_shape=jax.ShapeDtypeStruct((bsz, seq, BRANCH_W), BF16),
        compiler_params=pltpu.CompilerParams(
            dimension_semantics=("arbitrary",), vmem_limit_bytes=VMEM_LIMIT),
        name="fourier",
    )(f, gate, *consts)


def _merge_kernel(x_ref, mod_ref, ya_ref, yb_ref, yc_ref, yd_ref, w_ref, o_ref):
    cat = jnp.concatenate([ya_ref[...], yb_ref[...], yc_ref[...], yd_ref[...]], axis=1)
    o_ref[...] = x_ref[...] + mod_ref[2:3, :] * _dot(cat, w_ref[...])


def _merge(xs, mod, ya, yb, yc, yd, w_out):
    bsz, seq, _ = xs.shape
    ts = min(MERGE_TILE, seq)
    per_sample = mod.shape[0] == bsz
    branch = pl.BlockSpec((None, ts, BRANCH_W), lambda b, i: (b, i, 0))
    return pl.pallas_call(
        _merge_kernel,
        grid=(bsz, seq // ts),
        in_specs=[
            pl.BlockSpec((None, ts, D_MODEL), lambda b, i: (b, i, 0)),
            pl.BlockSpec((None, 3, D_MODEL), (lambda b, i: (b, 0, 0)) if per_sample else (lambda b, i: (0, 0, 0))),
            branch, branch, branch, branch,
            pl.BlockSpec(w_out.shape, lambda b, i: (0, 0)),
        ],
        out_specs=pl.BlockSpec((None, ts, D_MODEL), lambda b, i: (b, i, 0)),
        out_shape=jax.ShapeDtypeStruct(xs.shape, F32),
        compiler_params=pltpu.CompilerParams(
            dimension_semantics=("arbitrary", "arbitrary"), vmem_limit_bytes=VMEM_LIMIT),
        name="merge",
    )(xs, mod, ya, yb, yc, yd, w_out)


def _rope_table(seq, segments, half):
    pos = jnp.arange(seq, dtype=jnp.int32)
    row = (pos // GRID_W).astype(F32)
    col = (pos % GRID_W).astype(F32)
    idx = np.zeros(LANES, np.int32)
    active = np.zeros(LANES, bool)
    use_col = np.zeros(LANES, bool)
    first = np.zeros(LANES, bool)
    dim = segments[0][1]
    for start, d, kind in segments:
        assert d == dim and d // 2 == half
        for t in range(d):
            idx[start + t] = t % (d // 2)
            active[start + t] = True
            use_col[start + t] = kind == 1
            first[start + t] = t < d // 2
    inv = ROPE_THETA ** (-jnp.arange(0, dim, 2, dtype=F32) / dim)
    inv_lane = inv[idx][None, :]
    ang = jnp.where(use_col[None, :], col[:, None], row[:, None]) * inv_lane
    act = active[None, :]
    cos = jnp.where(act, jnp.cos(ang), 1.0)
    sin = jnp.sin(ang)
    sin_a = jnp.where(act & first[None, :], -sin, 0.0)
    sin_b = jnp.where(act & ~first[None, :], sin, 0.0)
    return jnp.stack([cos, sin_a, sin_b]).astype(F32)


def _dft_tables(seq):
    def trig(prod, period):
        ang = (prod % period).astype(F32) * (2.0 * np.pi / period)
        return jnp.cos(ang), jnp.sin(ang)

    half = seq // 2
    nblk = seq // DFT_BLOCK
    k = jnp.arange(half, dtype=jnp.int32)
    ca, sa = trig((jnp.arange(nblk, dtype=jnp.int32) * DFT_BLOCK)[:, None] * k[None, :], seq)
    cb, sb = trig(k[:, None] * jnp.arange(DFT_BLOCK, dtype=jnp.int32)[None, :], seq)
    ca, sa, cb, sb = ca[:, :, None], sa[:, :, None], cb[None], sb[None]
    cm = jnp.concatenate([ca * cb - sa * sb, -(sa * cb + ca * sb)], axis=0).astype(BF16)
    pos = jnp.arange(seq, dtype=jnp.int32)
    alt = jnp.where(jnp.arange(8, dtype=jnp.int32)[:, None] == 0,
                    (1 - 2 * (pos % 2)).astype(F32)[None, :], 0.0).astype(BF16)
    rev = ((k[:, None] + k[None, :] == half) & (k[:, None] >= 1)).astype(BF16)
    c = jnp.arange(BRANCH_W, dtype=jnp.int32)
    same = (c[:, None] // FNET_GDIM) == (c[None, :] // FNET_GDIM)
    cc, cs = trig((c[:, None] % FNET_GDIM) * (c[None, :] % FNET_GDIM), FNET_GDIM)
    bc = jnp.where(same, cc, 0.0).astype(BF16)
    bs = jnp.where(same, cs, 0.0).astype(BF16)
    return cm, alt, rev, bc, bs


def _pad_cols(a, width):
    return jnp.pad(a, ((0, 0), (0, width - a.shape[1])))


def _layer_weights(l, norm_g, w_in, mla_q_norm, mla_w_uq, mla_kv_norm, mla_w_ukv, mla_qn, mla_kn,
                   gqa_qn, gqa_kn, cm_ln_g, cm_ln_b, cm_w_s, cm_b_s, fnet_w, w_out):
    w = w_in[l]
    sp = np.cumsum([0, 192, 128, 32, 256, 256, 128, 128, 256, 256, 256, 256, 256, 256])
    nat = [w[:, sp[i]:sp[i + 1]] for i in range(13)]
    cq, ckv, kr, ga, q2, k2, v2, gb, u, vc, gc, f, gd = nat
    def zeros(n):
        return jnp.zeros((D_MODEL, n), w.dtype)

    w_in_p = jnp.concatenate(
        [cq, zeros(256 - MLA_Q_RANK), ckv,
         zeros(MLA_NOPE_DIM), kr, zeros(LANES - MLA_NOPE_DIM - MLA_ROPE_DIM),
         k2, v2, gb, ga, q2, u, vc, gc, f, gd], axis=1).astype(BF16)
    assert w_in_p.shape[1] == C_END

    seg = np.arange(MLA_ROPE_DIM) % 16
    partner = MLA_NOPE_DIM + (np.arange(MLA_ROPE_DIM) // 16) * 16 + (seg + 8) % 16
    perm = np.concatenate([np.arange(MLA_NOPE_DIM), partner])
    wuq = mla_w_uq[l].reshape(MLA_Q_RANK, MLA_HEADS, MLA_QK_DIM)
    wuq = jnp.concatenate([wuq, wuq[:, :, perm]], axis=1)
    wuq = jnp.pad(wuq, ((0, 256 - MLA_Q_RANK), (0, 0), (0, LANES - MLA_QK_DIM)))
    wuq = wuq.reshape(256, 2 * MLA_HEADS * LANES).astype(BF16)
    wukv = mla_w_ukv[l].reshape(MLA_KV_RANK, MLA_HEADS, MLA_NOPE_DIM + MLA_V_DIM)
    wk = jnp.pad(wukv[:, :, :MLA_NOPE_DIM], ((0, 0), (0, 0), (0, LANES - MLA_NOPE_DIM)))
    wv = wukv[:, :, MLA_NOPE_DIM:]
    wukv_p = jnp.concatenate(
        [wk.reshape(MLA_KV_RANK, MLA_HEADS * LANES), wv.reshape(MLA_KV_RANK, MLA_HEADS * MLA_V_DIM)],
        axis=1).astype(BF16)

    def row(v):
        return _pad_cols(v[None, :].astype(F32), 256)

    pvec = jnp.concatenate([
        row(mla_q_norm[l]), row(mla_kv_norm[l]), row(mla_qn[l]), row(mla_kn[l]),
        row(jnp.tile(gqa_qn[l], 2)), row(jnp.tile(gqa_kn[l], 2)), row(cm_ln_g[l]), row(cm_ln_b[l]),
        row(mla_qn[l][perm])] + [row(jnp.zeros((1,), F32))] * 7, axis=0)

    w_cm = jnp.transpose(cm_w_s[l], (1, 0, 2)).reshape(CHUNK, CM_GROUPS * CHUNK).astype(BF16)
    b_cm = jnp.broadcast_to(cm_b_s[l].T[:, :, None], (CHUNK, CM_GROUPS, BRANCH_W // CM_GROUPS))
    b_cm = b_cm.reshape(CHUNK, BRANCH_W).astype(F32)
    return dict(norm_g=norm_g[l][None, :], w_in=w_in_p, pvec=pvec, w_uq=wuq, w_ukv=wukv_p,
                w_cm=w_cm, b_cm=b_cm, fnet_w=fnet_w[l].astype(BF16), w_out=w_out[l].astype(BF16))


def kernel(x, c, ctx, c_ctx, norm_g, w_mod, b_mod, w_in, mla_q_norm, mla_w_uq, mla_kv_norm, mla_w_ukv,
           mla_qn, mla_kn, gqa_qn, gqa_kn, cm_ln_g, cm_ln_b, cm_w_s, cm_b_s, fnet_w, w_out):
    bsz, seq, _ = x.shape
    ctx_len = ctx.shape[1]
    depth = w_in.shape[0]

    rows = -(-(bsz + 1) // 8) * 8
    cc = jnp.concatenate([c, c_ctx[None, :], jnp.zeros((rows - bsz - 1, D_MODEL), F32)], axis=0)
    mod_all = _modulation(cc, w_mod, b_mod)

    rope_a = _rope_table(seq, [(MLA_NOPE_DIM, 16, 0), (MLA_NOPE_DIM + 16, 16, 1)], 8)
    rope_b = _rope_table(seq, [(0, 32, 0), (32, 32, 1), (64, 32, 0), (96, 32, 1)], 16)
    dft_x = _dft_tables(seq)
    dft_c = _dft_tables(ctx_len)

    for l in range(depth):
        lw = _layer_weights(l, norm_g, w_in, mla_q_norm, mla_w_uq, mla_kv_norm, mla_w_ukv, mla_qn,
                            mla_kn, gqa_qn, gqa_kn, cm_ln_g, cm_ln_b, cm_w_s, cm_b_s, fnet_w, w_out)
        mod_x = mod_all[l, :bsz].reshape(bsz, 3, D_MODEL)
        mod_c = mod_all[l, bsz:bsz + 1].reshape(1, 3, D_MODEL)
        update_ctx = l < depth - 1

        (q1, k1, v1t, q2, k2, v2t, sga, sgb, yc, f, sgd) = _projection(
            x, mod_x, lw, (rope_a, rope_b), kv_only=False)
        if update_ctx:
            (q1c, k1c, v1tc, q2c, k2c, v2tc, sgac, sgbc, ycc, fc, sgdc) = _projection(
                ctx, mod_c, lw, None, kv_only=False)
        else:
            k1c, v1tc, k2c, v2tc = _projection(ctx, mod_c, lw, None, kv_only=True)

        ya = _attention(q1, sga, [(k1, v1t), (k1c, v1tc)], paired_q=False)
        yb = _attention(q2, sgb, [(k2, v2t), (k2c, v2tc)], paired_q=True)
        yd = _fourier(f, sgd, dft_x, lw)
        x = _merge(x, mod_x, ya, yb, yc, yd, lw["w_out"])
        if update_ctx:
            yac = _attention(q1c, sgac, [(k1c, v1tc)], paired_q=False)
            ybc = _attention(q2c, sgbc, [(k2c, v2tc)], paired_q=True)
            ydc = _fourier(fc, sgdc, dft_c, lw)
            ctx = _merge(ctx, mod_c, yac, ybc, ycc, ydc, lw["w_out"])
    return x
```

```python
import functools

import numpy as np
import jax
import jax.numpy as jnp
from jax import lax
from jax.experimental import pallas as pl
from jax.experimental.pallas import tpu as pltpu

D_MODEL = 1024
GRID_W = 64
BRANCH_W = 256
MLA_HEADS = 4
MLA_NOPE_DIM = 64
MLA_ROPE_DIM = 32
MLA_QK_DIM = 96
MLA_V_DIM = 64
MLA_Q_RANK = 192
MLA_KV_RANK = 128
GQA_HEADS = 4
GQA_KV_HEADS = 2
GQA_HEAD_DIM = 64
CHUNK = 128
CM_GROUPS = 4
FNET_GROUPS = 4
FNET_GDIM = BRANCH_W // FNET_GROUPS
ROPE_THETA = 10000.0
EPS = 1e-6
LOG2_E = 1.4426950408889634

LANES = 128
V_ROWS = MLA_V_DIM + 16
PROJ_TILE = 512
MERGE_TILE = 1024
DFT_BLOCK = 256
ATTN_TILE = 256
SCORE_ROWS = (256, 256)
EXP_ROWS = (256, 256)
VMEM_LIMIT = 48 * 1024 * 1024

F32 = jnp.float32
BF16 = jnp.bfloat16

C_CQ, C_CKV, C_KR, C_K2, C_V2, C_GB, C_GA, C_Q2, C_U, C_VC, C_GC, C_F, C_GD, C_END = (
    0, 256, 384, 512, 640, 768, 1024, 1280, 1536, 1792, 2048, 2304, 2560, 2816)


def _dot(a, b):
    return jnp.dot(a, b, preferred_element_type=F32)


def _dot_nt(a, b):
    return lax.dot_general(a, b, (((1,), (1,)), ((), ())), preferred_element_type=F32)


def _silu(x):
    return x * jax.nn.sigmoid(x)


def _lane_id(shape):
    return lax.broadcasted_iota(jnp.int32, shape, len(shape) - 1)


def _mod_kernel(c_ref, w_ref, b_ref, o_ref):
    c = c_ref[...]
    o_ref[...] = _dot(_silu(c).astype(BF16), w_ref[...].astype(BF16)) + b_ref[...]


def _modulation(cc, w_mod, b_mod):
    n_layers = w_mod.shape[0]
    rows = cc.shape[0]
    return pl.pallas_call(
        _mod_kernel,
        grid=(n_layers, 3),
        in_specs=[
            pl.BlockSpec((rows, D_MODEL), lambda l, j: (0, 0)),
            pl.BlockSpec((None, D_MODEL, D_MODEL), lambda l, j: (l, 0, j)),
            pl.BlockSpec((None, 1, D_MODEL), lambda l, j: (l, 0, j)),
        ],
        out_specs=pl.BlockSpec((None, rows, D_MODEL), lambda l, j: (l, 0, j)),
        out_shape=jax.ShapeDtypeStruct((n_layers, rows, 3 * D_MODEL), F32),
        compiler_params=pltpu.CompilerParams(vmem_limit_bytes=VMEM_LIMIT),
        name="modulation",
    )(cc, w_mod, b_mod.reshape(n_layers, 1, 3 * D_MODEL))


def _rope(y, tab_ref, rs, half):
    width = y.shape[1]
    reps = width // LANES
    cos, sin_a, sin_b = tab_ref[0, rs, :], tab_ref[1, rs, :], tab_ref[2, rs, :]
    if reps > 1:
        cos = jnp.concatenate([cos] * reps, axis=1)
        sin_a = jnp.concatenate([sin_a] * reps, axis=1)
        sin_b = jnp.concatenate([sin_b] * reps, axis=1)
    return (y * cos + pltpu.roll(y, width - half, 1) * sin_a + pltpu.roll(y, half, 1) * sin_b)


def _head_rms_64(x, gain):
    outs = []
    for g in range(x.shape[1] // LANES):
        xg = x[:, g * LANES:(g + 1) * LANES]
        lo = _lane_id(xg.shape) < GQA_HEAD_DIM
        x2 = xg * xg
        s_lo = jnp.sum(jnp.where(lo, x2, 0.0), axis=-1, keepdims=True)
        s_hi = jnp.sum(jnp.where(lo, 0.0, x2), axis=-1, keepdims=True)
        ms = jnp.where(lo, s_lo, s_hi) * (1.0 / GQA_HEAD_DIM)
        outs.append(xg * lax.rsqrt(ms + EPS) * gain)
    return jnp.concatenate(outs, axis=1) if len(outs) > 1 else outs[0]


def _proj_kernel(*refs, rope, kv_only):
    it = iter(refs)
    x_ref, mod_ref, ng_ref, w_ref, pv_ref, wuq_ref, wukv_ref, wcm_ref, bcm_ref = (
        next(it) for _ in range(9))
    ra_ref = rb_ref = None
    if rope:
        ra_ref, rb_ref = next(it), next(it)
    if kv_only:
        k1_o, v1t_o, k2_o, v2t_o = (next(it) for _ in range(4))
    else:
        (q1_o, k1_o, v1t_o, q2_o, k2_o, v2t_o, sga_o, sgb_o, yc_o, f_o, sgd_o) = (
            next(it) for _ in range(11))

    x = x_ref[...]
    shift, scale = mod_ref[0:1, :], mod_ref[1:2, :]
    ms = jnp.mean(x * x, axis=-1, keepdims=True)
    h = (x * lax.rsqrt(ms + EPS) * ng_ref[...]) * (1.0 + scale) + shift
    hb = h.astype(BF16)

    def proj(lo, hi):
        return _dot(hb, w_ref[:, lo:hi])

    g_q, g_kv = pv_ref[0:1, :], pv_ref[1:2, 0:LANES]
    g_qn, g_kn = pv_ref[2:3, 0:LANES], pv_ref[3:4, 0:LANES]
    g_q2, g_k2 = pv_ref[4:5, 0:LANES], pv_ref[5:6, 0:LANES]
    ln_g, ln_b = pv_ref[6:7, :], pv_ref[7:8, :]
    g_qn_perm = pv_ref[8:9, 0:LANES]

    n_rows = x.shape[0]
    halves = [slice(r, r + n_rows // 2) for r in (0, n_rows // 2)] if n_rows >= 2 * CHUNK else [slice(0, n_rows)]
    firsts = []
    for rs in halves:
        d0 = _dot(hb[rs, :], w_ref[:, C_CQ:C_K2])
        cq = d0[:, 0:2 * LANES]
        ckv, kr = d0[:, 2 * LANES:3 * LANES], d0[:, 3 * LANES:4 * LANES]
        msk = jnp.mean(ckv * ckv, axis=-1, keepdims=True)
        ckvn = (ckv * lax.rsqrt(msk + EPS) * g_kv).astype(BF16)
        cqn = None
        if not kv_only:
            msq = jnp.sum(cq * cq, axis=-1, keepdims=True) * (1.0 / MLA_Q_RANK)
            cqn = (cq * lax.rsqrt(msq + EPS) * g_q).astype(BF16)
        firsts.append((ckvn, kr, cqn))
    for rs, (ckvn, kr, _) in zip(halves, firsts):
        kv = _dot(ckvn, wukv_ref[...])
        krg = kr * g_kn
        if rope:
            krg = _rope(krg, ra_ref, rs, MLA_ROPE_DIM // 4)
        kr_sq = jnp.sum(kr * kr, axis=-1, keepdims=True)
        for hh in range(MLA_HEADS):
            kn = kv[:, hh * LANES:(hh + 1) * LANES]
            msh = (jnp.sum(kn * kn, axis=-1, keepdims=True) + kr_sq) * (1.0 / MLA_QK_DIM)
            k1_o[rs, hh * LANES:(hh + 1) * LANES] = ((kn * g_kn + krg) * lax.rsqrt(msh + EPS)).astype(BF16)
        v1t_o[:, rs] = jnp.transpose(kv[:, 4 * LANES:]).astype(BF16)

    d1 = proj(C_K2, C_GA)
    k2 = _head_rms_64(d1[:, 0:LANES], g_k2)
    if rope:
        k2 = _rope(k2, rb_ref, slice(None), GQA_HEAD_DIM // 4)
    lo = _lane_id(k2.shape) < GQA_HEAD_DIM
    k2r = pltpu.roll(k2, GQA_HEAD_DIM, 1)
    zero = jnp.zeros_like(k2)
    k2_o[:, 0 * LANES:1 * LANES] = jnp.where(lo, k2, zero).astype(BF16)
    k2_o[:, 1 * LANES:2 * LANES] = jnp.where(lo, zero, k2r).astype(BF16)
    k2_o[:, 2 * LANES:3 * LANES] = jnp.where(lo, k2r, zero).astype(BF16)
    k2_o[:, 3 * LANES:4 * LANES] = jnp.where(lo, zero, k2).astype(BF16)
    v2t = jnp.transpose(d1[:, LANES:2 * LANES]).astype(BF16)
    hd = GQA_HEAD_DIM
    v2t_o[0 * hd:1 * hd, :] = v2t[0:hd]
    v2t_o[1 * hd:2 * hd, :] = v2t[0:hd]
    v2t_o[2 * hd:3 * hd, :] = v2t[hd:2 * hd]
    v2t_o[3 * hd:4 * hd, :] = v2t[hd:2 * hd]
    if kv_only:
        return
    sgb_o[...] = _silu(d1[:, 2 * LANES:]).astype(BF16)

    for rs, (_, _, cqn) in zip(halves, firsts):
        qq = _dot(cqn, wuq_ref[...])
        for hh in range(MLA_HEADS):
            qs = qq[:, hh * LANES:(hh + 1) * LANES]
            r = lax.rsqrt(jnp.sum(qs * qs, axis=-1, keepdims=True) * (1.0 / MLA_QK_DIM) + EPS)
            r = r * (LOG2_E * MLA_QK_DIM ** -0.5)
            y = qs * g_qn
            if rope:
                qp = qq[:, (MLA_HEADS + hh) * LANES:(MLA_HEADS + hh + 1) * LANES]
                y = y * ra_ref[0, rs, :] + (qp * g_qn_perm) * (ra_ref[1, rs, :] + ra_ref[2, rs, :])
            q1_o[rs, hh * LANES:(hh + 1) * LANES] = (y * r).astype(BF16)

    d2 = proj(C_GA, C_U)
    sga_o[...] = _silu(d2[:, 0:BRANCH_W]).astype(BF16)
    q2 = _head_rms_64(d2[:, BRANCH_W:], g_q2)
    if rope:
        q2 = _rope(q2, rb_ref, slice(None), GQA_HEAD_DIM // 4)
    q2_o[...] = (q2 * (LOG2_E * GQA_HEAD_DIM ** -0.5)).astype(BF16)

    d3 = proj(C_U, C_GC)
    d4 = proj(C_GC, C_GD)
    u, vc = d3[:, 0:BRANCH_W], d3[:, BRANCH_W:]
    sgc = _silu(d4[:, 0:BRANCH_W])
    f_o[...] = d4[:, BRANCH_W:].astype(BF16)
    sgd_o[...] = _silu(proj(C_GD, C_END)).astype(BF16)
    mu = jnp.mean(vc, axis=-1, keepdims=True)
    var = jnp.mean(jnp.square(vc - mu), axis=-1, keepdims=True)
    vn = ((vc - mu) * lax.rsqrt(var + EPS) * ln_g + ln_b).astype(BF16)
    grp = lax.shift_right_logical(_lane_id((CHUNK, BRANCH_W)), 6)
    for c in range(x.shape[0] // CHUNK):
        cr = slice(c * CHUNK, (c + 1) * CHUNK)
        vnc = vn[cr, :]
        stacked = jnp.concatenate(
            [jnp.where(grp == g, vnc, jnp.zeros_like(vnc)) for g in range(CM_GROUPS)], axis=0)
        s = _dot(wcm_ref[...], stacked) + bcm_ref[...]
        yc_o[cr, :] = (u[cr, :] * s * sgc[cr, :]).astype(BF16)


def _projection(xs, mod, lw, rope_tabs, *, kv_only):
    bsz, seq, _ = xs.shape
    ts = min(PROJ_TILE, seq)
    rope = rope_tabs is not None
    per_sample = mod.shape[0] == bsz

    def full(a):
        nd = a.ndim
        return pl.BlockSpec(a.shape, lambda b, i: (0,) * nd)

    in_specs = [
        pl.BlockSpec((None, ts, D_MODEL), lambda b, i: (b, i, 0)),
        pl.BlockSpec((None, 3, D_MODEL), (lambda b, i: (b, 0, 0)) if per_sample else (lambda b, i: (0, 0, 0))),
        full(lw["norm_g"]), full(lw["w_in"]), full(lw["pvec"]), full(lw["w_uq"]), full(lw["w_ukv"]),
        full(lw["w_cm"]), full(lw["b_cm"]),
    ]
    args = [xs, mod, lw["norm_g"], lw["w_in"], lw["pvec"], lw["w_uq"], lw["w_ukv"], lw["w_cm"], lw["b_cm"]]
    if rope:
        for t in rope_tabs:
            in_specs.append(pl.BlockSpec((3, ts, LANES), lambda b, i: (0, i, 0)))
            args.append(t)

    def rows_out(width):
        return (pl.BlockSpec((None, ts, width), lambda b, i: (b, i, 0)),
                jax.ShapeDtypeStruct((bsz, seq, width), BF16))

    def cols_out(height):
        return (pl.BlockSpec((None, height, ts), lambda b, i: (b, 0, i)),
                jax.ShapeDtypeStruct((bsz, height, seq), BF16))

    if kv_only:
        outs = [rows_out(512), cols_out(256), rows_out(512), cols_out(256)]
    else:
        outs = [rows_out(512), rows_out(512), cols_out(256), rows_out(256), rows_out(512), cols_out(256),
                rows_out(256), rows_out(256), rows_out(256), rows_out(256), rows_out(256)]
    return pl.pallas_call(
        functools.partial(_proj_kernel, rope=rope, kv_only=kv_only),
        grid=(bsz, seq // ts),
        in_specs=in_specs,
        out_specs=[o[0] for o in outs],
        out_shape=[o[1] for o in outs],
        compiler_params=pltpu.CompilerParams(
            dimension_semantics=("arbitrary", "arbitrary"), vmem_limit_bytes=VMEM_LIMIT),
        name="projection",
    )(*args)


def _interleave(major, minor):
    out, j = [], 0
    for i, th in enumerate(major):
        out.append(th)
        want = ((i + 1) * len(minor)) // len(major)
        out.extend(minor[j:want])
        j = want
    return out


def _attn_kernel(*refs, n_src, tq, n_qt):
    carry = n_qt > 1
    it = iter(refs)
    q_refs = (next(it), next(it))
    g_ref = next(it)
    srcs = [(next(it), next(it), next(it)) for _ in range(n_src)]
    if carry:
        qn_refs = (next(it), next(it))
        ksrcs_next = [(next(it), next(it)) for _ in range(n_src)]
    o_ref = next(it)
    vt_s = (next(it), next(it))
    s_bufs = ((next(it), next(it)), (next(it), next(it)))
    m_s = (next(it), next(it)) if carry else None
    ksrcs = [(ke, ko) for ke, ko, _ in srcs]

    off = 0
    for _, _, vt in srcs:
        n = vt.shape[1]
        vt_s[0][0:MLA_V_DIM, off:off + n] = vt[0:MLA_V_DIM, :]
        vt_s[1][0:MLA_V_DIM, off:off + n] = vt[MLA_V_DIM:2 * MLA_V_DIM, :]
        off += n
    n_keys = off
    ones = jnp.ones((V_ROWS - MLA_V_DIM, n_keys), BF16)
    vt_s[0][MLA_V_DIM:V_ROWS, :] = ones
    vt_s[1][MLA_V_DIM:V_ROWS, :] = ones

    def rows(t):
        if isinstance(t, int):
            return slice(t * tq, (t + 1) * tq)
        return pl.ds(pl.multiple_of(t * tq, tq), tq)

    def score_thunks(q_of, keys, slot, m_out):
        per_head = ([], [])
        for h in range(2):
            off = 0
            for pair in keys:
                n = pair[h].shape[0]
                for lo in range(0, n, SCORE_ROWS[h]):
                    hi = min(lo + SCORE_ROWS[h], n)

                    def th(h=h, k_ref=pair[h], lo=lo, hi=hi, off=off):
                        s = _dot_nt(k_ref[lo:hi, :], q_of(h))
                        s_bufs[slot][h][off + lo:off + hi, :] = s
                        mx = jnp.max(s, axis=0, keepdims=True)
                        m_out[h] = mx if m_out[h] is None else jnp.maximum(m_out[h], mx)
                    per_head[h].append(th)
                off += n
        return _interleave(per_head[0], per_head[1])

    def own_scores(t, slot, m_out):
        return score_thunks(lambda h: q_refs[h][rows(t), :], ksrcs, slot, m_out)

    def value_thunks(t, slot, m_in):
        acc = [None, None]
        per_head = ([], [])
        for h in range(2):
            for c in range(0, n_keys, EXP_ROWS[h]):
                def th(h=h, c=c, e=min(c + EXP_ROWS[h], n_keys)):
                    p = jnp.exp2(s_bufs[slot][h][c:e, :] - m_in[h]).astype(BF16)
                    part = _dot(vt_s[h][:, c:e], p)
                    acc[h] = part if acc[h] is None else acc[h] + part
                per_head[h].append(th)
        thunks = _interleave(per_head[0], per_head[1])

        def fin():
            outs = [a[0:MLA_V_DIM, :] / a[MLA_V_DIM:MLA_V_DIM + 1, :] for a in acc]
            att = jnp.transpose(jnp.concatenate(outs, axis=0))
            o_ref[rows(t), :] = (att * g_ref[rows(t), :]).astype(BF16)
        thunks.append(fin)
        return thunks

    def run(thunks):
        for th in thunks:
            th()

    def stage(scores, t_val, slot_val, m_val):
        m_new = [None, None]
        run(_interleave(value_thunks(t_val, slot_val, m_val), scores(m_new)))
        return m_new

    if not carry:
        m0 = [None, None]
        run(own_scores(0, 0, m0))
        run(value_thunks(0, 0, m0))
        return

    @pl.when((pl.program_id(0) == 0) & (pl.program_id(1) == 0))
    def _():
        m_first = [None, None]
        run(own_scores(0, 0, m_first))
        for h in range(2):
            m_s[h][...] = m_first[h]

    def body(u, m_in):
        m_a = stage(lambda m: own_scores(2 * u + 1, 1, m), 2 * u, 0, list(m_in))
        m_b = stage(lambda m: own_scores(2 * u + 2, 0, m), 2 * u + 1, 1, m_a)
        return tuple(m_b)
    m_even = lax.fori_loop(0, n_qt // 2 - 1, body, (m_s[0][...], m_s[1][...]))
    m_last = stage(lambda m: own_scores(n_qt - 1, 1, m), n_qt - 2, 0, list(m_even))
    m_next = stage(lambda m: score_thunks(lambda h: qn_refs[h][...], ksrcs_next, 0, m), n_qt - 1, 1, m_last)
    for h in range(2):
        m_s[h][...] = m_next[h]


def _attention(q, gate, sources, *, paired_q):
    bsz, sq, _ = q.shape
    tq = min(ATTN_TILE, sq)
    n_qt = sq // tq
    assert n_qt == 1 or n_qt % 2 == 0

    def q_cols(j):
        return (j, j) if paired_q else (2 * j, 2 * j + 1)

    def following(b, j):
        g = jnp.minimum(2 * b + j + 1, 2 * bsz - 1)
        return g // 2, g % 2

    in_specs = [pl.BlockSpec((None, sq, LANES), lambda b, j, h=h: (b, 0, q_cols(j)[h])) for h in range(2)]
    in_specs.append(pl.BlockSpec((None, sq, LANES), lambda b, j: (b, 0, j)))
    args = [q, q, gate]
    total = 0
    for k, vt in sources:
        n = k.shape[1]
        total += n
        in_specs += [
            pl.BlockSpec((None, n, LANES), lambda b, j: (b, 0, 2 * j)),
            pl.BlockSpec((None, n, LANES), lambda b, j: (b, 0, 2 * j + 1)),
            pl.BlockSpec((None, LANES, n), lambda b, j: (b, j, 0)),
        ]
        args += [k, k, vt]
    scratch = [pltpu.VMEM((V_ROWS, total), BF16), pltpu.VMEM((V_ROWS, total), BF16)]
    scratch += [pltpu.VMEM((total, tq), F32) for _ in range(4)]
    if n_qt > 1:
        def next_q(b, j, h):
            bn, jn = following(b, j)
            return bn, 0, q_cols(jn)[h]

        def next_k(b, j, h):
            bn, jn = following(b, j)
            return bn, 0, 2 * jn + h

        in_specs += [pl.BlockSpec((None, tq, LANES), functools.partial(next_q, h=h)) for h in range(2)]
        args += [q, q]
        for k, _ in sources:
            in_specs += [pl.BlockSpec((None, k.shape[1], LANES), functools.partial(next_k, h=h))
                         for h in range(2)]
            args += [k, k]
        scratch += [pltpu.VMEM((1, tq), F32), pltpu.VMEM((1, tq), F32)]
    return pl.pallas_call(
        functools.partial(_attn_kernel, n_src=len(sources), tq=tq, n_qt=n_qt),
        grid=(bsz, 2),
        in_specs=in_specs,
        out_specs=pl.BlockSpec((None, sq, LANES), lambda b, j: (b, 0, j)),
        out_shape=jax.ShapeDtypeStruct((bsz, sq, BRANCH_W), BF16),
        scratch_shapes=scratch,
        compiler_params=pltpu.CompilerParams(
            dimension_semantics=("arbitrary", "arbitrary"), vmem_limit_bytes=VMEM_LIMIT),
        name="attention",
    )(*args)


def _fourier_kernel(f_ref, g_ref, cm_ref, alt_ref, rev_ref, bc_ref, bs_ref, wf_ref, o_ref, *, seq):
    half = seq // 2
    nblk = seq // DFT_BLOCK
    f = f_ref[...]
    g_cos = _dot(f, bc_ref[...]).astype(BF16)
    g_sin = _dot(f, bs_ref[...]).astype(BF16)
    y_cos = y_sin = None
    for a in range(nblk):
        rs = slice(a * DFT_BLOCK, (a + 1) * DFT_BLOCK)
        pc = _dot(cm_ref[a], g_cos[rs, :])
        ps = _dot(cm_ref[nblk + a], g_sin[rs, :])
        y_cos = pc if y_cos is None else y_cos + pc
        y_sin = ps if y_sin is None else y_sin + ps
    norm = (seq * FNET_GDIM) ** -0.5
    low = ((y_cos + y_sin) * norm).astype(BF16)
    diff = ((y_cos - y_sin) * norm).astype(BF16)
    y_mid = _dot(alt_ref[...], g_cos)[0:1, :] * norm
    high = _dot(rev_ref[...], diff)
    first_row = lax.broadcasted_iota(jnp.int32, high.shape, 0) == 0
    high = jnp.where(first_row, y_mid, high).astype(BF16)
    wf = wf_ref[...]
    o_ref[0:half, :] = (_dot(low, wf) * g_ref[0:half, :]).astype(BF16)
    o_ref[half:seq, :] = (_dot(high, wf) * g_ref[half:seq, :]).astype(BF16)


def _fourier(f, gate, dft, lw):
    bsz, seq, _ = f.shape

    def const(a):
        return pl.BlockSpec(a.shape, lambda b: (0,) * a.ndim, pipeline_mode=pl.Buffered(1))

    consts = list(dft) + [lw["fnet_w"]]
    return pl.pallas_call(
        functools.partial(_fourier_kernel, seq=seq),
        grid=(bsz,),
        in_specs=[
            pl.BlockSpec((None, seq, BRANCH_W), lambda b: (b, 0, 0)),
            pl.BlockSpec((None, seq, BRANCH_W), lambda b: (b, 0, 0)),
        ] + [const(a) for a in consts],
        out_specs=pl.BlockSpec((None, seq, BRANCH_W), lambda b: (b, 0, 0)),
        out_shape=jax.ShapeDtypeStruct((bsz, seq, BRANCH_W), BF16),
        compiler_params=pltpu.CompilerParams(
            dimension_semantics=("arbitrary",), vmem_limit_bytes=VMEM_LIMIT),
        name="fourier",
    )(f, gate, *consts)


def _merge_kernel(x_ref, mod_ref, ya_ref, yb_ref, yc_ref, yd_ref, w_ref, o_ref):
    cat = jnp.concatenate([ya_ref[...], yb_ref[...], yc_ref[...], yd_ref[...]], axis=1)
    o_ref[...] = x_ref[...] + mod_ref[2:3, :] * _dot(cat, w_ref[...])


def _merge(xs, mod, ya, yb, yc, yd, w_out):
    bsz, seq, _ = xs.shape
    ts = min(MERGE_TILE, seq)
    per_sample = mod.shape[0] == bsz
    branch = pl.BlockSpec((None, ts, BRANCH_W), lambda b, i: (b, i, 0))
    return pl.pallas_call(
        _merge_kernel,
        grid=(bsz, seq // ts),
        in_specs=[
            pl.BlockSpec((None, ts, D_MODEL), lambda b, i: (b, i, 0)),
            pl.BlockSpec((None, 3, D_MODEL), (lambda b, i: (b, 0, 0)) if per_sample else (lambda b, i: (0, 0, 0))),
            branch, branch, branch, branch,
            pl.BlockSpec(w_out.shape, lambda b, i: (0, 0)),
        ],
        out_specs=pl.BlockSpec((None, ts, D_MODEL), lambda b, i: (b, i, 0)),
        out_shape=jax.ShapeDtypeStruct(xs.shape, F32),
        compiler_params=pltpu.CompilerParams(
            dimension_semantics=("arbitrary", "arbitrary"), vmem_limit_bytes=VMEM_LIMIT),
        name="merge",
    )(xs, mod, ya, yb, yc, yd, w_out)


def _rope_table(seq, segments, half):
    pos = jnp.arange(seq, dtype=jnp.int32)
    row = (pos // GRID_W).astype(F32)
    col = (pos % GRID_W).astype(F32)
    idx = np.zeros(LANES, np.int32)
    active = np.zeros(LANES, bool)
    use_col = np.zeros(LANES, bool)
    first = np.zeros(LANES, bool)
    dim = segments[0][1]
    for start, d, kind in segments:
        assert d == dim and d // 2 == half
        for t in range(d):
            idx[start + t] = t % (d // 2)
            active[start + t] = True
            use_col[start + t] = kind == 1
            first[start + t] = t < d // 2
    inv = ROPE_THETA ** (-jnp.arange(0, dim, 2, dtype=F32) / dim)
    inv_lane = inv[idx][None, :]
    ang = jnp.where(use_col[None, :], col[:, None], row[:, None]) * inv_lane
    act = active[None, :]
    cos = jnp.where(act, jnp.cos(ang), 1.0)
    sin = jnp.sin(ang)
    sin_a = jnp.where(act & first[None, :], -sin, 0.0)
    sin_b = jnp.where(act & ~first[None, :], sin, 0.0)
    return jnp.stack([cos, sin_a, sin_b]).astype(F32)


def _dft_tables(seq):
    def trig(prod, period):
        ang = (prod % period).astype(F32) * (2.0 * np.pi / period)
        return jnp.cos(ang), jnp.sin(ang)

    half = seq // 2
    nblk = seq // DFT_BLOCK
    k = jnp.arange(half, dtype=jnp.int32)
    ca, sa = trig((jnp.arange(nblk, dtype=jnp.int32) * DFT_BLOCK)[:, None] * k[None, :], seq)
    cb, sb = trig(k[:, None] * jnp.arange(DFT_BLOCK, dtype=jnp.int32)[None, :], seq)
    ca, sa, cb, sb = ca[:, :, None], sa[:, :, None], cb[None], sb[None]
    cm = jnp.concatenate([ca * cb - sa * sb, -(sa * cb + ca * sb)], axis=0).astype(BF16)
    pos = jnp.arange(seq, dtype=jnp.int32)
    alt = jnp.where(jnp.arange(8, dtype=jnp.int32)[:, None] == 0,
                    (1 - 2 * (pos % 2)).astype(F32)[None, :], 0.0).astype(BF16)
    rev = ((k[:, None] + k[None, :] == half) & (k[:, None] >= 1)).astype(BF16)
    c = jnp.arange(BRANCH_W, dtype=jnp.int32)
    same = (c[:, None] // FNET_GDIM) == (c[None, :] // FNET_GDIM)
    cc, cs = trig((c[:, None] % FNET_GDIM) * (c[None, :] % FNET_GDIM), FNET_GDIM)
    bc = jnp.where(same, cc, 0.0).astype(BF16)
    bs = jnp.where(same, cs, 0.0).astype(BF16)
    return cm, alt, rev, bc, bs


def _pad_cols(a, width):
    return jnp.pad(a, ((0, 0), (0, width - a.shape[1])))


def _layer_weights(l, norm_g, w_in, mla_q_norm, mla_w_uq, mla_kv_norm, mla_w_ukv, mla_qn, mla_kn,
                   gqa_qn, gqa_kn, cm_ln_g, cm_ln_b, cm_w_s, cm_b_s, fnet_w, w_out):
    w = w_in[l]
    sp = np.cumsum([0, 192, 128, 32, 256, 256, 128, 128, 256, 256, 256, 256, 256, 256])
    nat = [w[:, sp[i]:sp[i + 1]] for i in range(13)]
    cq, ckv, kr, ga, q2, k2, v2, gb, u, vc, gc, f, gd = nat
    def zeros(n):
        return jnp.zeros((D_MODEL, n), w.dtype)

    w_in_p = jnp.concatenate(
        [cq, zeros(256 - MLA_Q_RANK), ckv,
         zeros(MLA_NOPE_DIM), kr, zeros(LANES - MLA_NOPE_DIM - MLA_ROPE_DIM),
         k2, v2, gb, ga, q2, u, vc, gc, f, gd], axis=1).astype(BF16)
    assert w_in_p.shape[1] == C_END

    seg = np.arange(MLA_ROPE_DIM) % 16
    partner = MLA_NOPE_DIM + (np.arange(MLA_ROPE_DIM) // 16) * 16 + (seg + 8) % 16
    perm = np.concatenate([np.arange(MLA_NOPE_DIM), partner])
    wuq = mla_w_uq[l].reshape(MLA_Q_RANK, MLA_HEADS, MLA_QK_DIM)
    wuq = jnp.concatenate([wuq, wuq[:, :, perm]], axis=1)
    wuq = jnp.pad(wuq, ((0, 256 - MLA_Q_RANK), (0, 0), (0, LANES - MLA_QK_DIM)))
    wuq = wuq.reshape(256, 2 * MLA_HEADS * LANES).astype(BF16)
    wukv = mla_w_ukv[l].reshape(MLA_KV_RANK, MLA_HEADS, MLA_NOPE_DIM + MLA_V_DIM)
    wk = jnp.pad(wukv[:, :, :MLA_NOPE_DIM], ((0, 0), (0, 0), (0, LANES - MLA_NOPE_DIM)))
    wv = wukv[:, :, MLA_NOPE_DIM:]
    wukv_p = jnp.concatenate(
        [wk.reshape(MLA_KV_RANK, MLA_HEADS * LANES), wv.reshape(MLA_KV_RANK, MLA_HEADS * MLA_V_DIM)],
        axis=1).astype(BF16)

    def row(v):
        return _pad_cols(v[None, :].astype(F32), 256)

    pvec = jnp.concatenate([
        row(mla_q_norm[l]), row(mla_kv_norm[l]), row(mla_qn[l]), row(mla_kn[l]),
        row(jnp.tile(gqa_qn[l], 2)), row(jnp.tile(gqa_kn[l], 2)), row(cm_ln_g[l]), row(cm_ln_b[l]),
        row(mla_qn[l][perm])] + [row(jnp.zeros((1,), F32))] * 7, axis=0)

    w_cm = jnp.transpose(cm_w_s[l], (1, 0, 2)).reshape(CHUNK, CM_GROUPS * CHUNK).astype(BF16)
    b_cm = jnp.broadcast_to(cm_b_s[l].T[:, :, None], (CHUNK, CM_GROUPS, BRANCH_W // CM_GROUPS))
    b_cm = b_cm.reshape(CHUNK, BRANCH_W).astype(F32)
    return dict(norm_g=norm_g[l][None, :], w_in=w_in_p, pvec=pvec, w_uq=wuq, w_ukv=wukv_p,
                w_cm=w_cm, b_cm=b_cm, fnet_w=fnet_w[l].astype(BF16), w_out=w_out[l].astype(BF16))


def kernel(x, c, ctx, c_ctx, norm_g, w_mod, b_mod, w_in, mla_q_norm, mla_w_uq, mla_kv_norm, mla_w_ukv,
           mla_qn, mla_kn, gqa_qn, gqa_kn, cm_ln_g, cm_ln_b, cm_w_s, cm_b_s, fnet_w, w_out):
    bsz, seq, _ = x.shape
    ctx_len = ctx.shape[1]
    depth = w_in.shape[0]

    rows = -(-(bsz + 1) // 8) * 8
    cc = jnp.concatenate([c, c_ctx[None, :], jnp.zeros((rows - bsz - 1, D_MODEL), F32)], axis=0)
    mod_all = _modulation(cc, w_mod, b_mod)

    rope_a = _rope_table(seq, [(MLA_NOPE_DIM, 16, 0), (MLA_NOPE_DIM + 16, 16, 1)], 8)
    rope_b = _rope_table(seq, [(0, 32, 0), (32, 32, 1), (64, 32, 0), (96, 32, 1)], 16)
    dft_x = _dft_tables(seq)
    dft_c = _dft_tables(ctx_len)

    for l in range(depth):
        lw = _layer_weights(l, norm_g, w_in, mla_q_norm, mla_w_uq, mla_kv_norm, mla_w_ukv, mla_qn,
                            mla_kn, gqa_qn, gqa_kn, cm_ln_g, cm_ln_b, cm_w_s, cm_b_s, fnet_w, w_out)
        mod_x = mod_all[l, :bsz].reshape(bsz, 3, D_MODEL)
        mod_c = mod_all[l, bsz:bsz + 1].reshape(1, 3, D_MODEL)
        update_ctx = l < depth - 1

        (q1, k1, v1t, q2, k2, v2t, sga, sgb, yc, f, sgd) = _projection(
            x, mod_x, lw, (rope_a, rope_b), kv_only=False)
        if update_ctx:
            (q1c, k1c, v1tc, q2c, k2c, v2tc, sgac, sgbc, ycc, fc, sgdc) = _projection(
                ctx, mod_c, lw, None, kv_only=False)
        else:
            k1c, v1tc, k2c, v2tc = _projection(ctx, mod_c, lw, None, kv_only=True)

        ya = _attention(q1, sga, [(k1, v1t), (k1c, v1tc)], paired_q=False)
        yb = _attention(q2, sgb, [(k2, v2t), (k2c, v2tc)], paired_q=True)
        yd = _fourier(f, sgd, dft_x, lw)
        x = _merge(x, mod_x, ya, yb, yc, yd, lw["w_out"])
        if update_ctx:
            yac = _attention(q1c, sgac, [(k1c, v1tc)], paired_q=False)
            ybc = _attention(q2c, sgbc, [(k2c, v2tc)], paired_q=True)
            ydc = _fourier(fc, sgdc, dft_c, lw)
            ctx = _merge(ctx, mod_c, yac, ybc, ycc, ydc, lw["w_out"])
    return x
```

```python
import functools

import numpy as np
import jax
import jax.numpy as jnp
from jax import lax
from jax.experimental import pallas as pl
from jax.experimental.pallas import tpu as pltpu

D_MODEL = 1024
GRID_W = 64
BRANCH_W = 256
MLA_HEADS = 4
MLA_NOPE_DIM = 64
MLA_ROPE_DIM = 32
MLA_QK_DIM = 96
MLA_V_DIM = 64
MLA_Q_RANK = 192
MLA_KV_RANK = 128
GQA_HEADS = 4
GQA_KV_HEADS = 2
GQA_HEAD_DIM = 64
CHUNK = 128
CM_GROUPS = 4
FNET_GROUPS = 4
FNET_GDIM = BRANCH_W // FNET_GROUPS
ROPE_THETA = 10000.0
EPS = 1e-6
LOG2_E = 1.4426950408889634

LANES = 128
V_ROWS = MLA_V_DIM + 16
PROJ_TILE = 512
MERGE_TILE = 1024
DFT_BLOCK = 256
ATTN_TILE = 256
ATTN_ITEMS = 8
SCORE_ROWS = (256, 256)
EXP_ROWS = (256, 256)
VMEM_LIMIT = 48 * 1024 * 1024

F32 = jnp.float32
BF16 = jnp.bfloat16

C_CQ, C_CKV, C_KR, C_K2, C_V2, C_GB, C_GA, C_Q2, C_U, C_VC, C_GC, C_F, C_GD, C_END = (
    0, 256, 384, 512, 640, 768, 1024, 1280, 1536, 1792, 2048, 2304, 2560, 2816)


def _dot(a, b):
    return jnp.dot(a, b, preferred_element_type=F32)


def _dot_nt(a, b):
    return lax.dot_general(a, b, (((1,), (1,)), ((), ())), preferred_element_type=F32)


def _silu(x):
    return x * jax.nn.sigmoid(x)


def _lane_id(shape):
    return lax.broadcasted_iota(jnp.int32, shape, len(shape) - 1)


def _mod_kernel(c_ref, w_ref, b_ref, o_ref):
    c = c_ref[...]
    o_ref[...] = _dot(_silu(c).astype(BF16), w_ref[...].astype(BF16)) + b_ref[...]


def _modulation(cc, w_mod, b_mod):
    n_layers = w_mod.shape[0]
    rows = cc.shape[0]
    return pl.pallas_call(
        _mod_kernel,
        grid=(n_layers, 3),
        in_specs=[
            pl.BlockSpec((rows, D_MODEL), lambda l, j: (0, 0)),
            pl.BlockSpec((None, D_MODEL, D_MODEL), lambda l, j: (l, 0, j)),
            pl.BlockSpec((None, 1, D_MODEL), lambda l, j: (l, 0, j)),
        ],
        out_specs=pl.BlockSpec((None, rows, D_MODEL), lambda l, j: (l, 0, j)),
        out_shape=jax.ShapeDtypeStruct((n_layers, rows, 3 * D_MODEL), F32),
        compiler_params=pltpu.CompilerParams(vmem_limit_bytes=VMEM_LIMIT),
        name="modulation",
    )(cc, w_mod, b_mod.reshape(n_layers, 1, 3 * D_MODEL))


def _rope(y, tab_ref, rs, half):
    width = y.shape[1]
    reps = width // LANES
    cos, sin_a, sin_b = tab_ref[0, rs, :], tab_ref[1, rs, :], tab_ref[2, rs, :]
    if reps > 1:
        cos = jnp.concatenate([cos] * reps, axis=1)
        sin_a = jnp.concatenate([sin_a] * reps, axis=1)
        sin_b = jnp.concatenate([sin_b] * reps, axis=1)
    return (y * cos + pltpu.roll(y, width - half, 1) * sin_a + pltpu.roll(y, half, 1) * sin_b)


def _head_rms_64(x, gain):
    outs = []
    for g in range(x.shape[1] // LANES):
        xg = x[:, g * LANES:(g + 1) * LANES]
        lo = _lane_id(xg.shape) < GQA_HEAD_DIM
        x2 = xg * xg
        s_lo = jnp.sum(jnp.where(lo, x2, 0.0), axis=-1, keepdims=True)
        s_hi = jnp.sum(jnp.where(lo, 0.0, x2), axis=-1, keepdims=True)
        ms = jnp.where(lo, s_lo, s_hi) * (1.0 / GQA_HEAD_DIM)
        outs.append(xg * lax.rsqrt(ms + EPS) * gain)
    return jnp.concatenate(outs, axis=1) if len(outs) > 1 else outs[0]


def _proj_kernel(*refs, rope, kv_only, merge, emit_x):
    it = iter(refs)
    x_ref, mod_ref, ng_ref, w_ref, pv_ref, wuq_ref, wukv_ref, wcm_ref, bcm_ref = (
        next(it) for _ in range(9))
    ra_ref = rb_ref = None
    if rope:
        ra_ref, rb_ref = next(it), next(it)
    if merge:
        pmod_ref, ya_ref, yb_ref, yc_ref, yd_ref, wout_ref = (next(it) for _ in range(6))
    x_o = next(it) if emit_x else None
    if kv_only:
        k1_o, v1t_o, k2_o, v2t_o = (next(it) for _ in range(4))
    else:
        (q1_o, k1_o, v1t_o, q2_o, k2_o, v2t_o, sga_o, sgb_o, yc_o, f_o, sgd_o) = (
            next(it) for _ in range(11))

    x = x_ref[...]
    if merge:
        cat = jnp.concatenate([ya_ref[...], yb_ref[...], yc_ref[...], yd_ref[...]], axis=1)
        x = x + pmod_ref[2:3, :] * _dot(cat, wout_ref[...])
    if emit_x:
        x_o[...] = x
    shift, scale = mod_ref[0:1, :], mod_ref[1:2, :]
    ms = jnp.mean(x * x, axis=-1, keepdims=True)
    h = (x * lax.rsqrt(ms + EPS) * ng_ref[...]) * (1.0 + scale) + shift
    hb = h.astype(BF16)

    def proj(lo, hi):
        return _dot(hb, w_ref[:, lo:hi])

    g_q, g_kv = pv_ref[0:1, :], pv_ref[1:2, 0:LANES]
    g_qn, g_kn = pv_ref[2:3, 0:LANES], pv_ref[3:4, 0:LANES]
    g_q2, g_k2 = pv_ref[4:5, 0:LANES], pv_ref[5:6, 0:LANES]
    ln_g, ln_b = pv_ref[6:7, :], pv_ref[7:8, :]
    g_qn_perm = pv_ref[8:9, 0:LANES]

    n_rows = x.shape[0]
    halves = [slice(r, r + n_rows // 2) for r in (0, n_rows // 2)] if n_rows >= 2 * CHUNK else [slice(0, n_rows)]
    firsts = []
    for rs in halves:
        d0 = _dot(hb[rs, :], w_ref[:, C_CQ:C_K2])
        cq = d0[:, 0:2 * LANES]
        ckv, kr = d0[:, 2 * LANES:3 * LANES], d0[:, 3 * LANES:4 * LANES]
        msk = jnp.mean(ckv * ckv, axis=-1, keepdims=True)
        ckvn = (ckv * lax.rsqrt(msk + EPS) * g_kv).astype(BF16)
        cqn = None
        if not kv_only:
            msq = jnp.sum(cq * cq, axis=-1, keepdims=True) * (1.0 / MLA_Q_RANK)
            cqn = (cq * lax.rsqrt(msq + EPS) * g_q).astype(BF16)
        firsts.append((ckvn, kr, cqn))
    for rs, (ckvn, kr, _) in zip(halves, firsts):
        kv = _dot(ckvn, wukv_ref[...])
        krg = kr * g_kn
        if rope:
            krg = _rope(krg, ra_ref, rs, MLA_ROPE_DIM // 4)
        kr_sq = jnp.sum(kr * kr, axis=-1, keepdims=True)
        for hh in range(MLA_HEADS):
            kn = kv[:, hh * LANES:(hh + 1) * LANES]
            msh = (jnp.sum(kn * kn, axis=-1, keepdims=True) + kr_sq) * (1.0 / MLA_QK_DIM)
            k1_o[rs, hh * LANES:(hh + 1) * LANES] = ((kn * g_kn + krg) * lax.rsqrt(msh + EPS)).astype(BF16)
        v1t_o[:, rs] = jnp.transpose(kv[:, 4 * LANES:]).astype(BF16)

    d1 = proj(C_K2, C_GA)
    k2 = _head_rms_64(d1[:, 0:LANES], g_k2)
    if rope:
        k2 = _rope(k2, rb_ref, slice(None), GQA_HEAD_DIM // 4)
    lo = _lane_id(k2.shape) < GQA_HEAD_DIM
    k2r = pltpu.roll(k2, GQA_HEAD_DIM, 1)
    zero = jnp.zeros_like(k2)
    k2_o[:, 0 * LANES:1 * LANES] = jnp.where(lo, k2, zero).astype(BF16)
    k2_o[:, 1 * LANES:2 * LANES] = jnp.where(lo, zero, k2r).astype(BF16)
    k2_o[:, 2 * LANES:3 * LANES] = jnp.where(lo, k2r, zero).astype(BF16)
    k2_o[:, 3 * LANES:4 * LANES] = jnp.where(lo, zero, k2).astype(BF16)
    v2t = jnp.transpose(d1[:, LANES:2 * LANES]).astype(BF16)
    hd = GQA_HEAD_DIM
    v2t_o[0 * hd:1 * hd, :] = v2t[0:hd]
    v2t_o[1 * hd:2 * hd, :] = v2t[0:hd]
    v2t_o[2 * hd:3 * hd, :] = v2t[hd:2 * hd]
    v2t_o[3 * hd:4 * hd, :] = v2t[hd:2 * hd]
    if kv_only:
        return
    sgb_o[...] = _silu(d1[:, 2 * LANES:]).astype(BF16)

    for rs, (_, _, cqn) in zip(halves, firsts):
        qq = _dot(cqn, wuq_ref[...])
        for hh in range(MLA_HEADS):
            qs = qq[:, hh * LANES:(hh + 1) * LANES]
            r = lax.rsqrt(jnp.sum(qs * qs, axis=-1, keepdims=True) * (1.0 / MLA_QK_DIM) + EPS)
            r = r * (LOG2_E * MLA_QK_DIM ** -0.5)
            y = qs * g_qn
            if rope:
                qp = qq[:, (MLA_HEADS + hh) * LANES:(MLA_HEADS + hh + 1) * LANES]
                y = y * ra_ref[0, rs, :] + (qp * g_qn_perm) * (ra_ref[1, rs, :] + ra_ref[2, rs, :])
            q1_o[rs, hh * LANES:(hh + 1) * LANES] = (y * r).astype(BF16)

    d2 = proj(C_GA, C_U)
    sga_o[...] = _silu(d2[:, 0:BRANCH_W]).astype(BF16)
    q2 = _head_rms_64(d2[:, BRANCH_W:], g_q2)
    if rope:
        q2 = _rope(q2, rb_ref, slice(None), GQA_HEAD_DIM // 4)
    q2_o[...] = (q2 * (LOG2_E * GQA_HEAD_DIM ** -0.5)).astype(BF16)

    d3 = proj(C_U, C_GC)
    d4 = proj(C_GC, C_GD)
    u, vc = d3[:, 0:BRANCH_W], d3[:, BRANCH_W:]
    sgc = _silu(d4[:, 0:BRANCH_W])
    f_o[...] = d4[:, BRANCH_W:].astype(BF16)
    sgd_o[...] = _silu(proj(C_GD, C_END)).astype(BF16)
    mu = jnp.mean(vc, axis=-1, keepdims=True)
    var = jnp.mean(jnp.square(vc - mu), axis=-1, keepdims=True)
    vn = ((vc - mu) * lax.rsqrt(var + EPS) * ln_g + ln_b).astype(BF16)
    grp = lax.shift_right_logical(_lane_id((CHUNK, BRANCH_W)), 6)
    for c in range(x.shape[0] // CHUNK):
        cr = slice(c * CHUNK, (c + 1) * CHUNK)
        vnc = vn[cr, :]
        stacked = jnp.concatenate(
            [jnp.where(grp == g, vnc, jnp.zeros_like(vnc)) for g in range(CM_GROUPS)], axis=0)
        s = _dot(wcm_ref[...], stacked) + bcm_ref[...]
        yc_o[cr, :] = (u[cr, :] * s * sgc[cr, :]).astype(BF16)


def _projection(xs, mod, lw, rope_tabs, *, kv_only, merge_in=None, emit_x=False):
    bsz, seq, _ = xs.shape
    ts = min(PROJ_TILE, seq)
    rope = rope_tabs is not None

    def full(a):
        nd = a.ndim
        return pl.BlockSpec(a.shape, lambda b, i: (0,) * nd)

    def mod_spec(m):
        per_sample = m.shape[0] == bsz
        return pl.BlockSpec((None, 3, D_MODEL), (lambda b, i: (b, 0, 0)) if per_sample else (lambda b, i: (0, 0, 0)))

    in_specs = [
        pl.BlockSpec((None, ts, D_MODEL), lambda b, i: (b, i, 0)), mod_spec(mod),
        full(lw["norm_g"]), full(lw["w_in"]), full(lw["pvec"]), full(lw["w_uq"]), full(lw["w_ukv"]),
        full(lw["w_cm"]), full(lw["b_cm"]),
    ]
    args = [xs, mod, lw["norm_g"], lw["w_in"], lw["pvec"], lw["w_uq"], lw["w_ukv"], lw["w_cm"], lw["b_cm"]]
    if rope:
        for t in rope_tabs:
            in_specs.append(pl.BlockSpec((3, ts, LANES), lambda b, i: (0, i, 0)))
            args.append(t)
    if merge_in is not None:
        pmod, ya, yb, yc, yd, w_out = merge_in
        branch = pl.BlockSpec((None, ts, BRANCH_W), lambda b, i: (b, i, 0))
        in_specs += [mod_spec(pmod), branch, branch, branch, branch, full(w_out)]
        args += [pmod, ya, yb, yc, yd, w_out]

    def rows_out(width):
        return (pl.BlockSpec((None, ts, width), lambda b, i: (b, i, 0)),
                jax.ShapeDtypeStruct((bsz, seq, width), BF16))

    def cols_out(height):
        return (pl.BlockSpec((None, height, ts), lambda b, i: (b, 0, i)),
                jax.ShapeDtypeStruct((bsz, height, seq), BF16))

    if kv_only:
        outs = [rows_out(512), cols_out(256), rows_out(512), cols_out(256)]
    else:
        outs = [rows_out(512), rows_out(512), cols_out(256), rows_out(256), rows_out(512), cols_out(256),
                rows_out(256), rows_out(256), rows_out(256), rows_out(256), rows_out(256)]
    if emit_x:
        outs.insert(0, (pl.BlockSpec((None, ts, D_MODEL), lambda b, i: (b, i, 0)),
                        jax.ShapeDtypeStruct(xs.shape, F32)))
    return pl.pallas_call(
        functools.partial(_proj_kernel, rope=rope, kv_only=kv_only, merge=merge_in is not None,
                          emit_x=emit_x),
        grid=(bsz, seq // ts),
        in_specs=in_specs,
        out_specs=[o[0] for o in outs],
        out_shape=[o[1] for o in outs],
        compiler_params=pltpu.CompilerParams(
            dimension_semantics=("arbitrary", "arbitrary"), vmem_limit_bytes=VMEM_LIMIT),
        name="projection",
    )(*args)


def _interleave(major, minor):
    out, j = [], 0
    for i, th in enumerate(major):
        out.append(th)
        want = ((i + 1) * len(minor)) // len(major)
        out.extend(minor[j:want])
        j = want
    return out


def _attn_kernel(*refs, n_src, tq, n_qt, nb):
    carry = n_qt > 1
    it = iter(refs)
    q_refs = (next(it), next(it))
    g_ref = next(it)
    srcs = [(next(it), next(it), next(it)) for _ in range(n_src)]
    nxt = None
    if carry:
        nxt = ((next(it), next(it)), [(next(it), next(it)) for _ in range(n_src)])
    o_ref = next(it)
    vt_s = (next(it), next(it))
    s_bufs = ((next(it), next(it)), (next(it), next(it)))
    m_s = (next(it), next(it)) if carry else None
    if nb == 1:
        _attn_body(q_refs, g_ref, srcs, nxt, o_ref, vt_s, s_bufs, m_s, tq=tq, n_qt=n_qt)
        return
    for i in range(nb):
        def at(r):
            return r.at[i]
        _attn_body(tuple(map(at, q_refs)), at(g_ref), [tuple(map(at, s)) for s in srcs], None,
                   at(o_ref), tuple(map(at, vt_s)), tuple(tuple(map(at, sl)) for sl in s_bufs), None,
                   tq=tq, n_qt=n_qt)


def _attn_body(q_refs, g_ref, srcs, nxt, o_ref, vt_s, s_bufs, m_s, *, tq, n_qt):
    carry = n_qt > 1
    if carry:
        qn_refs, ksrcs_next = nxt
    ksrcs = [(ke, ko) for ke, ko, _ in srcs]

    off = 0
    for _, _, vt in srcs:
        n = vt.shape[1]
        vt_s[0][0:MLA_V_DIM, off:off + n] = vt[0:MLA_V_DIM, :]
        vt_s[1][0:MLA_V_DIM, off:off + n] = vt[MLA_V_DIM:2 * MLA_V_DIM, :]
        off += n
    n_keys = off
    ones = jnp.ones((V_ROWS - MLA_V_DIM, n_keys), BF16)
    vt_s[0][MLA_V_DIM:V_ROWS, :] = ones
    vt_s[1][MLA_V_DIM:V_ROWS, :] = ones

    def rows(t):
        if isinstance(t, int):
            return slice(t * tq, (t + 1) * tq)
        return pl.ds(pl.multiple_of(t * tq, tq), tq)

    def score_thunks(q_of, keys, slot, m_out):
        per_head = ([], [])
        for h in range(2):
            off = 0
            for pair in keys:
                n = pair[h].shape[0]
                for lo in range(0, n, SCORE_ROWS[h]):
                    hi = min(lo + SCORE_ROWS[h], n)

                    def th(h=h, k_ref=pair[h], lo=lo, hi=hi, off=off):
                        s = _dot_nt(k_ref[lo:hi, :], q_of(h))
                        s_bufs[slot][h][off + lo:off + hi, :] = s
                        mx = jnp.max(s, axis=0, keepdims=True)
                        m_out[h] = mx if m_out[h] is None else jnp.maximum(m_out[h], mx)
                    per_head[h].append(th)
                off += n
        return _interleave(per_head[0], per_head[1])

    def own_scores(t, slot, m_out):
        return score_thunks(lambda h: q_refs[h][rows(t), :], ksrcs, slot, m_out)

    def value_thunks(t, slot, m_in):
        acc = [None, None]
        per_head = ([], [])
        for h in range(2):
            for c in range(0, n_keys, EXP_ROWS[h]):
                def th(h=h, c=c, e=min(c + EXP_ROWS[h], n_keys)):
                    p = jnp.exp2(s_bufs[slot][h][c:e, :] - m_in[h]).astype(BF16)
                    part = _dot(vt_s[h][:, c:e], p)
                    acc[h] = part if acc[h] is None else acc[h] + part
                per_head[h].append(th)
        thunks = _interleave(per_head[0], per_head[1])

        def fin():
            outs = [a[0:MLA_V_DIM, :] / a[MLA_V_DIM:MLA_V_DIM + 1, :] for a in acc]
            att = jnp.transpose(jnp.concatenate(outs, axis=0))
            o_ref[rows(t), :] = (att * g_ref[rows(t), :]).astype(BF16)
        thunks.append(fin)
        return thunks

    def run(thunks):
        for th in thunks:
            th()

    def stage(scores, t_val, slot_val, m_val):
        m_new = [None, None]
        run(_interleave(value_thunks(t_val, slot_val, m_val), scores(m_new)))
        return m_new

    if not carry:
        m0 = [None, None]
        run(own_scores(0, 0, m0))
        run(value_thunks(0, 0, m0))
        return

    @pl.when((pl.program_id(0) == 0) & (pl.program_id(1) == 0))
    def _():
        m_first = [None, None]
        run(own_scores(0, 0, m_first))
        for h in range(2):
            m_s[h][...] = m_first[h]

    def body(u, m_in):
        m_a = stage(lambda m: own_scores(2 * u + 1, 1, m), 2 * u, 0, list(m_in))
        m_b = stage(lambda m: own_scores(2 * u + 2, 0, m), 2 * u + 1, 1, m_a)
        return tuple(m_b)
    m_even = lax.fori_loop(0, n_qt // 2 - 1, body, (m_s[0][...], m_s[1][...]))
    m_last = stage(lambda m: own_scores(n_qt - 1, 1, m), n_qt - 2, 0, list(m_even))
    m_next = stage(lambda m: score_thunks(lambda h: qn_refs[h][...], ksrcs_next, 0, m), n_qt - 1, 1, m_last)
    for h in range(2):
        m_s[h][...] = m_next[h]


def _attention(q, gate, sources, *, paired_q):
    bsz, sq, _ = q.shape
    tq = min(ATTN_TILE, sq)
    n_qt = sq // tq
    assert n_qt == 1 or n_qt % 2 == 0

    def q_cols(j):
        return (j, j) if paired_q else (2 * j, 2 * j + 1)

    def following(b, j):
        g = jnp.minimum(2 * b + j + 1, 2 * bsz - 1)
        return g // 2, g % 2

    nb = 1 if n_qt > 1 else ATTN_ITEMS
    assert bsz % nb == 0
    lead = None if nb == 1 else nb

    def scr(shape, dtype):
        return pltpu.VMEM(shape if nb == 1 else (nb,) + shape, dtype)

    in_specs = [pl.BlockSpec((lead, sq, LANES), lambda b, j, h=h: (b, 0, q_cols(j)[h])) for h in range(2)]
    in_specs.append(pl.BlockSpec((lead, sq, LANES), lambda b, j: (b, 0, j)))
    args = [q, q, gate]
    total = 0
    for k, vt in sources:
        n = k.shape[1]
        total += n
        in_specs += [
            pl.BlockSpec((lead, n, LANES), lambda b, j: (b, 0, 2 * j)),
            pl.BlockSpec((lead, n, LANES), lambda b, j: (b, 0, 2 * j + 1)),
            pl.BlockSpec((lead, LANES, n), lambda b, j: (b, j, 0)),
        ]
        args += [k, k, vt]
    scratch = [scr((V_ROWS, total), BF16), scr((V_ROWS, total), BF16)]
    scratch += [scr((total, tq), F32) for _ in range(4)]
    if n_qt > 1:
        def next_q(b, j, h):
            bn, jn = following(b, j)
            return bn, 0, q_cols(jn)[h]

        def next_k(b, j, h):
            bn, jn = following(b, j)
            return bn, 0, 2 * jn + h

        in_specs += [pl.BlockSpec((None, tq, LANES), functools.partial(next_q, h=h)) for h in range(2)]
        args += [q, q]
        for k, _ in sources:
            in_specs += [pl.BlockSpec((None, k.shape[1], LANES), functools.partial(next_k, h=h))
                         for h in range(2)]
            args += [k, k]
        scratch += [pltpu.VMEM((1, tq), F32), pltpu.VMEM((1, tq), F32)]
    return pl.pallas_call(
        functools.partial(_attn_kernel, n_src=len(sources), tq=tq, n_qt=n_qt, nb=nb),
        grid=(bsz // nb, 2),
        in_specs=in_specs,
        out_specs=pl.BlockSpec((lead, sq, LANES), lambda b, j: (b, 0, j)),
        out_shape=jax.ShapeDtypeStruct((bsz, sq, BRANCH_W), BF16),
        scratch_shapes=scratch,
        compiler_params=pltpu.CompilerParams(
            dimension_semantics=("arbitrary", "arbitrary"), vmem_limit_bytes=VMEM_LIMIT),
        name="attention",
    )(*args)


def _fourier_kernel(f_ref, g_ref, cm_ref, alt_ref, rev_ref, bc_ref, bs_ref, wf_ref, o_ref, *, seq):
    half = seq // 2
    nblk = seq // DFT_BLOCK
    f = f_ref[...]
    g_cos = _dot(f, bc_ref[...]).astype(BF16)
    g_sin = _dot(f, bs_ref[...]).astype(BF16)
    y_cos = y_sin = None
    for a in range(nblk):
        rs = slice(a * DFT_BLOCK, (a + 1) * DFT_BLOCK)
        pc = _dot(cm_ref[a], g_cos[rs, :])
        ps = _dot(cm_ref[nblk + a], g_sin[rs, :])
        y_cos = pc if y_cos is None else y_cos + pc
        y_sin = ps if y_sin is None else y_sin + ps
    norm = (seq * FNET_GDIM) ** -0.5
    low = ((y_cos + y_sin) * norm).astype(BF16)
    diff = ((y_cos - y_sin) * norm).astype(BF16)
    y_mid = _dot(alt_ref[...], g_cos)[0:1, :] * norm
    high = _dot(rev_ref[...], diff)
    first_row = lax.broadcasted_iota(jnp.int32, high.shape, 0) == 0
    high = jnp.where(first_row, y_mid, high).astype(BF16)
    wf = wf_ref[...]
    o_ref[0:half, :] = (_dot(low, wf) * g_ref[0:half, :]).astype(BF16)
    o_ref[half:seq, :] = (_dot(high, wf) * g_ref[half:seq, :]).astype(BF16)


def _fourier(f, gate, dft, lw):
    bsz, seq, _ = f.shape

    def const(a):
        return pl.BlockSpec(a.shape, lambda b: (0,) * a.ndim, pipeline_mode=pl.Buffered(1))

    consts = list(dft) + [lw["fnet_w"]]
    return pl.pallas_call(
        functools.partial(_fourier_kernel, seq=seq),
        grid=(bsz,),
        in_specs=[
            pl.BlockSpec((None, seq, BRANCH_W), lambda b: (b, 0, 0)),
            pl.BlockSpec((None, seq, BRANCH_W), lambda b: (b, 0, 0)),
        ] + [const(a) for a in consts],
        out_specs=pl.BlockSpec((None, seq, BRANCH_W), lambda b: (b, 0, 0)),
        out_shape=jax.ShapeDtypeStruct((bsz, seq, BRANCH_W), BF16),
        compiler_params=pltpu.CompilerParams(
            dimension_semantics=("arbitrary",), vmem_limit_bytes=VMEM_LIMIT),
        name="fourier",
    )(f, gate, *consts)


def _merge_kernel(x_ref, mod_ref, ya_ref, yb_ref, yc_ref, yd_ref, w_ref, o_ref):
    cat = jnp.concatenate([ya_ref[...], yb_ref[...], yc_ref[...], yd_ref[...]], axis=1)
    o_ref[...] = x_ref[...] + mod_ref[2:3, :] * _dot(cat, w_ref[...])


def _merge(xs, mod, ya, yb, yc, yd, w_out):
    bsz, seq, _ = xs.shape
    ts = min(MERGE_TILE, seq)
    per_sample = mod.shape[0] == bsz
    branch = pl.BlockSpec((None, ts, BRANCH_W), lambda b, i: (b, i, 0))
    return pl.pallas_call(
        _merge_kernel,
        grid=(bsz, seq // ts),
        in_specs=[
            pl.BlockSpec((None, ts, D_MODEL), lambda b, i: (b, i, 0)),
            pl.BlockSpec((None, 3, D_MODEL), (lambda b, i: (b, 0, 0)) if per_sample else (lambda b, i: (0, 0, 0))),
            branch, branch, branch, branch,
            pl.BlockSpec(w_out.shape, lambda b, i: (0, 0)),
        ],
        out_specs=pl.BlockSpec((None, ts, D_MODEL), lambda b, i: (b, i, 0)),
        out_shape=jax.ShapeDtypeStruct(xs.shape, F32),
        compiler_params=pltpu.CompilerParams(
            dimension_semantics=("arbitrary", "arbitrary"), vmem_limit_bytes=VMEM_LIMIT),
        name="merge",
    )(xs, mod, ya, yb, yc, yd, w_out)


def _rope_table(seq, segments, half):
    pos = jnp.arange(seq, dtype=jnp.int32)
    row = (pos // GRID_W).astype(F32)
    col = (pos % GRID_W).astype(F32)
    idx = np.zeros(LANES, np.int32)
    active = np.zeros(LANES, bool)
    use_col = np.zeros(LANES, bool)
    first = np.zeros(LANES, bool)
    dim = segments[0][1]
    for start, d, kind in segments:
        assert d == dim and d // 2 == half
        for t in range(d):
            idx[start + t] = t % (d // 2)
            active[start + t] = True
            use_col[start + t] = kind == 1
            first[start + t] = t < d // 2
    inv = ROPE_THETA ** (-jnp.arange(0, dim, 2, dtype=F32) / dim)
    inv_lane = inv[idx][None, :]
    ang = jnp.where(use_col[None, :], col[:, None], row[:, None]) * inv_lane
    act = active[None, :]
    cos = jnp.where(act, jnp.cos(ang), 1.0)
    sin = jnp.sin(ang)
    sin_a = jnp.where(act & first[None, :], -sin, 0.0)
    sin_b = jnp.where(act & ~first[None, :], sin, 0.0)
    return jnp.stack([cos, sin_a, sin_b]).astype(F32)


def _dft_tables(seq):
    def trig(prod, period):
        ang = (prod % period).astype(F32) * (2.0 * np.pi / period)
        return jnp.cos(ang), jnp.sin(ang)

    half = seq // 2
    nblk = seq // DFT_BLOCK
    k = jnp.arange(half, dtype=jnp.int32)
    ca, sa = trig((jnp.arange(nblk, dtype=jnp.int32) * DFT_BLOCK)[:, None] * k[None, :], seq)
    cb, sb = trig(k[:, None] * jnp.arange(DFT_BLOCK, dtype=jnp.int32)[None, :], seq)
    ca, sa, cb, sb = ca[:, :, None], sa[:, :, None], cb[None], sb[None]
    cm = jnp.concatenate([ca * cb - sa * sb, -(sa * cb + ca * sb)], axis=0).astype(BF16)
    pos = jnp.arange(seq, dtype=jnp.int32)
    alt = jnp.where(jnp.arange(8, dtype=jnp.int32)[:, None] == 0,
                    (1 - 2 * (pos % 2)).astype(F32)[None, :], 0.0).astype(BF16)
    rev = ((k[:, None] + k[None, :] == half) & (k[:, None] >= 1)).astype(BF16)
    c = jnp.arange(BRANCH_W, dtype=jnp.int32)
    same = (c[:, None] // FNET_GDIM) == (c[None, :] // FNET_GDIM)
    cc, cs = trig((c[:, None] % FNET_GDIM) * (c[None, :] % FNET_GDIM), FNET_GDIM)
    bc = jnp.where(same, cc, 0.0).astype(BF16)
    bs = jnp.where(same, cs, 0.0).astype(BF16)
    return cm, alt, rev, bc, bs


def _pad_cols(a, width):
    return jnp.pad(a, ((0, 0), (0, width - a.shape[1])))


def _layer_weights(l, norm_g, w_in, mla_q_norm, mla_w_uq, mla_kv_norm, mla_w_ukv, mla_qn, mla_kn,
                   gqa_qn, gqa_kn, cm_ln_g, cm_ln_b, cm_w_s, cm_b_s, fnet_w, w_out):
    w = w_in[l]
    sp = np.cumsum([0, 192, 128, 32, 256, 256, 128, 128, 256, 256, 256, 256, 256, 256])
    nat = [w[:, sp[i]:sp[i + 1]] for i in range(13)]
    cq, ckv, kr, ga, q2, k2, v2, gb, u, vc, gc, f, gd = nat
    def zeros(n):
        return jnp.zeros((D_MODEL, n), w.dtype)

    w_in_p = jnp.concatenate(
        [cq, zeros(256 - MLA_Q_RANK), ckv,
         zeros(MLA_NOPE_DIM), kr, zeros(LANES - MLA_NOPE_DIM - MLA_ROPE_DIM),
         k2, v2, gb, ga, q2, u, vc, gc, f, gd], axis=1).astype(BF16)
    assert w_in_p.shape[1] == C_END

    seg = np.arange(MLA_ROPE_DIM) % 16
    partner = MLA_NOPE_DIM + (np.arange(MLA_ROPE_DIM) // 16) * 16 + (seg + 8) % 16
    perm = np.concatenate([np.arange(MLA_NOPE_DIM), partner])
    wuq = mla_w_uq[l].reshape(MLA_Q_RANK, MLA_HEADS, MLA_QK_DIM)
    wuq = jnp.concatenate([wuq, wuq[:, :, perm]], axis=1)
    wuq = jnp.pad(wuq, ((0, 256 - MLA_Q_RANK), (0, 0), (0, LANES - MLA_QK_DIM)))
    wuq = wuq.reshape(256, 2 * MLA_HEADS * LANES).astype(BF16)
    wukv = mla_w_ukv[l].reshape(MLA_KV_RANK, MLA_HEADS, MLA_NOPE_DIM + MLA_V_DIM)
    wk = jnp.pad(wukv[:, :, :MLA_NOPE_DIM], ((0, 0), (0, 0), (0, LANES - MLA_NOPE_DIM)))
    wv = wukv[:, :, MLA_NOPE_DIM:]
    wukv_p = jnp.concatenate(
        [wk.reshape(MLA_KV_RANK, MLA_HEADS * LANES), wv.reshape(MLA_KV_RANK, MLA_HEADS * MLA_V_DIM)],
        axis=1).astype(BF16)

    def row(v):
        return _pad_cols(v[None, :].astype(F32), 256)

    pvec = jnp.concatenate([
        row(mla_q_norm[l]), row(mla_kv_norm[l]), row(mla_qn[l]), row(mla_kn[l]),
        row(jnp.tile(gqa_qn[l], 2)), row(jnp.tile(gqa_kn[l], 2)), row(cm_ln_g[l]), row(cm_ln_b[l]),
        row(mla_qn[l][perm])] + [row(jnp.zeros((1,), F32))] * 7, axis=0)

    w_cm = jnp.transpose(cm_w_s[l], (1, 0, 2)).reshape(CHUNK, CM_GROUPS * CHUNK).astype(BF16)
    b_cm = jnp.broadcast_to(cm_b_s[l].T[:, :, None], (CHUNK, CM_GROUPS, BRANCH_W // CM_GROUPS))
    b_cm = b_cm.reshape(CHUNK, BRANCH_W).astype(F32)
    return dict(norm_g=norm_g[l][None, :], w_in=w_in_p, pvec=pvec, w_uq=wuq, w_ukv=wukv_p,
                w_cm=w_cm, b_cm=b_cm, fnet_w=fnet_w[l].astype(BF16), w_out=w_out[l].astype(BF16))


def kernel(x, c, ctx, c_ctx, norm_g, w_mod, b_mod, w_in, mla_q_norm, mla_w_uq, mla_kv_norm, mla_w_ukv,
           mla_qn, mla_kn, gqa_qn, gqa_kn, cm_ln_g, cm_ln_b, cm_w_s, cm_b_s, fnet_w, w_out):
    bsz, seq, _ = x.shape
    ctx_len = ctx.shape[1]
    depth = w_in.shape[0]

    rows = -(-(bsz + 1) // 8) * 8
    cc = jnp.concatenate([c, c_ctx[None, :], jnp.zeros((rows - bsz - 1, D_MODEL), F32)], axis=0)
    mod_all = _modulation(cc, w_mod, b_mod)

    rope_a = _rope_table(seq, [(MLA_NOPE_DIM, 16, 0), (MLA_NOPE_DIM + 16, 16, 1)], 8)
    rope_b = _rope_table(seq, [(0, 32, 0), (32, 32, 1), (64, 32, 0), (96, 32, 1)], 16)
    dft_x = _dft_tables(seq)
    dft_c = _dft_tables(ctx_len)

    pending_x = pending_c = None
    for l in range(depth):
        lw = _layer_weights(l, norm_g, w_in, mla_q_norm, mla_w_uq, mla_kv_norm, mla_w_ukv, mla_qn,
                            mla_kn, gqa_qn, gqa_kn, cm_ln_g, cm_ln_b, cm_w_s, cm_b_s, fnet_w, w_out)
        mod_x = mod_all[l, :bsz].reshape(bsz, 3, D_MODEL)
        mod_c = mod_all[l, bsz:bsz + 1].reshape(1, 3, D_MODEL)
        update_ctx = l < depth - 1

        outs = _projection(x, mod_x, lw, (rope_a, rope_b), kv_only=False,
                           merge_in=pending_x, emit_x=pending_x is not None)
        if pending_x is not None:
            x, outs = outs[0], outs[1:]
        (q1, k1, v1t, q2, k2, v2t, sga, sgb, yc, f, sgd) = outs
        outs_c = _projection(ctx, mod_c, lw, None, kv_only=not update_ctx,
                             merge_in=pending_c, emit_x=pending_c is not None and update_ctx)
        if pending_c is not None and update_ctx:
            ctx, outs_c = outs_c[0], outs_c[1:]
        if update_ctx:
            (q1c, k1c, v1tc, q2c, k2c, v2tc, sgac, sgbc, ycc, fc, sgdc) = outs_c
        else:
            k1c, v1tc, k2c, v2tc = outs_c

        ya = _attention(q1, sga, [(k1, v1t), (k1c, v1tc)], paired_q=False)
        yb = _attention(q2, sgb, [(k2, v2t), (k2c, v2tc)], paired_q=True)
        yd = _fourier(f, sgd, dft_x, lw)
        pending_x = (mod_x, ya, yb, yc, yd, lw["w_out"])
        pending_c = None
        if update_ctx:
            yac = _attention(q1c, sgac, [(k1c, v1tc)], paired_q=False)
            ybc = _attention(q2c, sgbc, [(k2c, v2tc)], paired_q=True)
            ydc = _fourier(fc, sgdc, dft_c, lw)
            pending_c = (mod_c, yac, ybc, ycc, ydc, lw["w_out"])
    return _merge(x, *pending_x)
```

```python
import functools

import numpy as np
import jax
import jax.numpy as jnp
from jax import lax
from jax.experimental import pallas as pl
from jax.experimental.pallas import tpu as pltpu

D_MODEL = 1024
GRID_W = 64
BRANCH_W = 256
MLA_HEADS = 4
MLA_NOPE_DIM = 64
MLA_ROPE_DIM = 32
MLA_QK_DIM = 96
MLA_V_DIM = 64
MLA_Q_RANK = 192
MLA_KV_RANK = 128
GQA_HEADS = 4
GQA_KV_HEADS = 2
GQA_HEAD_DIM = 64
CHUNK = 128
CM_GROUPS = 4
FNET_GROUPS = 4
FNET_GDIM = BRANCH_W // FNET_GROUPS
ROPE_THETA = 10000.0
EPS = 1e-6
LOG2_E = 1.4426950408889634

LANES = 128
V_ROWS = MLA_V_DIM + 16
PROJ_TILE = 512
MERGE_TILE = 1024
DFT_BLOCK = 256
ATTN_TILE = 256
ATTN_ITEMS = 8
SCORE_ROWS = (256, 256)
EXP_ROWS = (256, 256)
VMEM_LIMIT = 48 * 1024 * 1024

F32 = jnp.float32
BF16 = jnp.bfloat16

C_CQ, C_CKV, C_KR, C_K2, C_V2, C_GB, C_GA, C_Q2, C_U, C_VC, C_GC, C_F, C_GD, C_END = (
    0, 256, 384, 512, 640, 768, 1024, 1280, 1536, 1792, 2048, 2304, 2560, 2816)


def _dot(a, b):
    return jnp.dot(a, b, preferred_element_type=F32)


def _dot_nt(a, b):
    return lax.dot_general(a, b, (((1,), (1,)), ((), ())), preferred_element_type=F32)


def _silu(x):
    return x * jax.nn.sigmoid(x)


def _lane_id(shape):
    return lax.broadcasted_iota(jnp.int32, shape, len(shape) - 1)


def _mod_kernel(c_ref, w_ref, b_ref, o_ref):
    c = c_ref[...]
    o_ref[...] = _dot(_silu(c).astype(BF16), w_ref[...].astype(BF16)) + b_ref[...]


def _modulation(cc, w_mod, b_mod):
    n_layers = w_mod.shape[0]
    rows = cc.shape[0]
    return pl.pallas_call(
        _mod_kernel,
        grid=(n_layers, 3),
        in_specs=[
            pl.BlockSpec((rows, D_MODEL), lambda l, j: (0, 0)),
            pl.BlockSpec((None, D_MODEL, D_MODEL), lambda l, j: (l, 0, j)),
            pl.BlockSpec((None, 1, D_MODEL), lambda l, j: (l, 0, j)),
        ],
        out_specs=pl.BlockSpec((None, rows, D_MODEL), lambda l, j: (l, 0, j)),
        out_shape=jax.ShapeDtypeStruct((n_layers, rows, 3 * D_MODEL), F32),
        compiler_params=pltpu.CompilerParams(vmem_limit_bytes=VMEM_LIMIT),
        name="modulation",
    )(cc, w_mod, b_mod.reshape(n_layers, 1, 3 * D_MODEL))


def _rope(y, tab_ref, rs, half):
    width = y.shape[1]
    reps = width // LANES
    cos, sin_a, sin_b = tab_ref[0, rs, :], tab_ref[1, rs, :], tab_ref[2, rs, :]
    if reps > 1:
        cos = jnp.concatenate([cos] * reps, axis=1)
        sin_a = jnp.concatenate([sin_a] * reps, axis=1)
        sin_b = jnp.concatenate([sin_b] * reps, axis=1)
    return (y * cos + pltpu.roll(y, width - half, 1) * sin_a + pltpu.roll(y, half, 1) * sin_b)


def _head_rms_64(x, gain):
    outs = []
    for g in range(x.shape[1] // LANES):
        xg = x[:, g * LANES:(g + 1) * LANES]
        lo = _lane_id(xg.shape) < GQA_HEAD_DIM
        x2 = xg * xg
        s_lo = jnp.sum(jnp.where(lo, x2, 0.0), axis=-1, keepdims=True)
        s_hi = jnp.sum(jnp.where(lo, 0.0, x2), axis=-1, keepdims=True)
        ms = jnp.where(lo, s_lo, s_hi) * (1.0 / GQA_HEAD_DIM)
        outs.append(xg * lax.rsqrt(ms + EPS) * gain)
    return jnp.concatenate(outs, axis=1) if len(outs) > 1 else outs[0]


def _proj_kernel(*refs, rope, kv_only, merge, emit_x):
    it = iter(refs)
    x_ref, mod_ref, ng_ref, w_ref, pv_ref, wuq_ref, wukv_ref, wcm_ref, bcm_ref = (
        next(it) for _ in range(9))
    ra_ref = rb_ref = None
    if rope:
        ra_ref, rb_ref = next(it), next(it)
    if merge:
        pmod_ref, ya_ref, yb_ref, yc_ref, yd_ref, wout_ref = (next(it) for _ in range(6))
    x_o = next(it) if emit_x else None
    if kv_only:
        k1_o, v1t_o, k2_o, v2t_o = (next(it) for _ in range(4))
    else:
        (q1_o, k1_o, v1t_o, q2_o, k2_o, v2t_o, sga_o, sgb_o, yc_o, f_o, sgd_o) = (
            next(it) for _ in range(11))

    n_rows = x_ref.shape[0]
    halves = [slice(r, r + n_rows // 2) for r in (0, n_rows // 2)] if n_rows >= 2 * CHUNK else [slice(0, n_rows)]
    shift, scale = mod_ref[0:1, :], mod_ref[1:2, :]
    hb_parts = []
    for rs in halves:
        x = x_ref[rs, :]
        if merge:
            cat = jnp.concatenate([ya_ref[rs, :], yb_ref[rs, :], yc_ref[rs, :], yd_ref[rs, :]], axis=1)
            x = x + pmod_ref[2:3, :] * _dot(cat, wout_ref[...])
        if emit_x:
            x_o[rs, :] = x
        ms = jnp.mean(x * x, axis=-1, keepdims=True)
        h = (x * lax.rsqrt(ms + EPS) * ng_ref[...]) * (1.0 + scale) + shift
        hb_parts.append(h.astype(BF16))
    hb = jnp.concatenate(hb_parts, axis=0) if len(hb_parts) > 1 else hb_parts[0]

    def proj(lo, hi):
        return _dot(hb, w_ref[:, lo:hi])

    g_q, g_kv = pv_ref[0:1, :], pv_ref[1:2, 0:LANES]
    g_qn, g_kn = pv_ref[2:3, 0:LANES], pv_ref[3:4, 0:LANES]
    g_q2, g_k2 = pv_ref[4:5, 0:LANES], pv_ref[5:6, 0:LANES]
    ln_g, ln_b = pv_ref[6:7, :], pv_ref[7:8, :]
    g_qn_perm = pv_ref[8:9, 0:LANES]

    firsts = []
    for rs, hb_half in zip(halves, hb_parts):
        d0 = _dot(hb_half, w_ref[:, C_CQ:C_K2])
        cq = d0[:, 0:2 * LANES]
        ckv, kr = d0[:, 2 * LANES:3 * LANES], d0[:, 3 * LANES:4 * LANES]
        msk = jnp.mean(ckv * ckv, axis=-1, keepdims=True)
        ckvn = (ckv * lax.rsqrt(msk + EPS) * g_kv).astype(BF16)
        cqn = None
        if not kv_only:
            msq = jnp.sum(cq * cq, axis=-1, keepdims=True) * (1.0 / MLA_Q_RANK)
            cqn = (cq * lax.rsqrt(msq + EPS) * g_q).astype(BF16)
        firsts.append((ckvn, kr, cqn))
    for rs, (ckvn, kr, _) in zip(halves, firsts):
        kv = _dot(ckvn, wukv_ref[...])
        krg = kr * g_kn
        if rope:
            krg = _rope(krg, ra_ref, rs, MLA_ROPE_DIM // 4)
        kr_sq = jnp.sum(kr * kr, axis=-1, keepdims=True)
        for hh in range(MLA_HEADS):
            kn = kv[:, hh * LANES:(hh + 1) * LANES]
            msh = (jnp.sum(kn * kn, axis=-1, keepdims=True) + kr_sq) * (1.0 / MLA_QK_DIM)
            k1_o[rs, hh * LANES:(hh + 1) * LANES] = ((kn * g_kn + krg) * lax.rsqrt(msh + EPS)).astype(BF16)
        v1t_o[:, rs] = jnp.transpose(kv[:, 4 * LANES:]).astype(BF16)

    d1 = proj(C_K2, C_GA)
    k2 = _head_rms_64(d1[:, 0:LANES], g_k2)
    if rope:
        k2 = _rope(k2, rb_ref, slice(None), GQA_HEAD_DIM // 4)
    lo = _lane_id(k2.shape) < GQA_HEAD_DIM
    k2r = pltpu.roll(k2, GQA_HEAD_DIM, 1)
    zero = jnp.zeros_like(k2)
    k2_o[:, 0 * LANES:1 * LANES] = jnp.where(lo, k2, zero).astype(BF16)
    k2_o[:, 1 * LANES:2 * LANES] = jnp.where(lo, zero, k2r).astype(BF16)
    k2_o[:, 2 * LANES:3 * LANES] = jnp.where(lo, k2r, zero).astype(BF16)
    k2_o[:, 3 * LANES:4 * LANES] = jnp.where(lo, zero, k2).astype(BF16)
    v2t = jnp.transpose(d1[:, LANES:2 * LANES]).astype(BF16)
    hd = GQA_HEAD_DIM
    v2t_o[0 * hd:1 * hd, :] = v2t[0:hd]
    v2t_o[1 * hd:2 * hd, :] = v2t[0:hd]
    v2t_o[2 * hd:3 * hd, :] = v2t[hd:2 * hd]
    v2t_o[3 * hd:4 * hd, :] = v2t[hd:2 * hd]
    if kv_only:
        return
    sgb_o[...] = _silu(d1[:, 2 * LANES:]).astype(BF16)

    for rs, (_, _, cqn) in zip(halves, firsts):
        qq = _dot(cqn, wuq_ref[...])
        for hh in range(MLA_HEADS):
            qs = qq[:, hh * LANES:(hh + 1) * LANES]
            r = lax.rsqrt(jnp.sum(qs * qs, axis=-1, keepdims=True) * (1.0 / MLA_QK_DIM) + EPS)
            r = r * (LOG2_E * MLA_QK_DIM ** -0.5)
            y = qs * g_qn
            if rope:
                qp = qq[:, (MLA_HEADS + hh) * LANES:(MLA_HEADS + hh + 1) * LANES]
                y = y * ra_ref[0, rs, :] + (qp * g_qn_perm) * (ra_ref[1, rs, :] + ra_ref[2, rs, :])
            q1_o[rs, hh * LANES:(hh + 1) * LANES] = (y * r).astype(BF16)

    d2 = proj(C_GA, C_U)
    sga_o[...] = _silu(d2[:, 0:BRANCH_W]).astype(BF16)
    q2 = _head_rms_64(d2[:, BRANCH_W:], g_q2)
    if rope:
        q2 = _rope(q2, rb_ref, slice(None), GQA_HEAD_DIM // 4)
    q2_o[...] = (q2 * (LOG2_E * GQA_HEAD_DIM ** -0.5)).astype(BF16)

    d3 = proj(C_U, C_GC)
    d4 = proj(C_GC, C_GD)
    u, vc = d3[:, 0:BRANCH_W], d3[:, BRANCH_W:]
    sgc = _silu(d4[:, 0:BRANCH_W])
    f_o[...] = d4[:, BRANCH_W:].astype(BF16)
    sgd_o[...] = _silu(proj(C_GD, C_END)).astype(BF16)
    mu = jnp.mean(vc, axis=-1, keepdims=True)
    var = jnp.mean(jnp.square(vc - mu), axis=-1, keepdims=True)
    vn = ((vc - mu) * lax.rsqrt(var + EPS) * ln_g + ln_b).astype(BF16)
    grp = lax.shift_right_logical(_lane_id((CHUNK, BRANCH_W)), 6)
    for c in range(n_rows // CHUNK):
        cr = slice(c * CHUNK, (c + 1) * CHUNK)
        vnc = vn[cr, :]
        stacked = jnp.concatenate(
            [jnp.where(grp == g, vnc, jnp.zeros_like(vnc)) for g in range(CM_GROUPS)], axis=0)
        s = _dot(wcm_ref[...], stacked) + bcm_ref[...]
        yc_o[cr, :] = (u[cr, :] * s * sgc[cr, :]).astype(BF16)


def _projection(xs, mod, lw, rope_tabs, *, kv_only, merge_in=None, emit_x=False):
    bsz, seq, _ = xs.shape
    ts = min(PROJ_TILE, seq)
    rope = rope_tabs is not None

    def full(a):
        nd = a.ndim
        return pl.BlockSpec(a.shape, lambda b, i: (0,) * nd)

    def mod_spec(m):
        per_sample = m.shape[0] == bsz
        return pl.BlockSpec((None, 3, D_MODEL), (lambda b, i: (b, 0, 0)) if per_sample else (lambda b, i: (0, 0, 0)))

    in_specs = [
        pl.BlockSpec((None, ts, D_MODEL), lambda b, i: (b, i, 0)), mod_spec(mod),
        full(lw["norm_g"]), full(lw["w_in"]), full(lw["pvec"]), full(lw["w_uq"]), full(lw["w_ukv"]),
        full(lw["w_cm"]), full(lw["b_cm"]),
    ]
    args = [xs, mod, lw["norm_g"], lw["w_in"], lw["pvec"], lw["w_uq"], lw["w_ukv"], lw["w_cm"], lw["b_cm"]]
    if rope:
        for t in rope_tabs:
            in_specs.append(pl.BlockSpec((3, ts, LANES), lambda b, i: (0, i, 0)))
            args.append(t)
    if merge_in is not None:
        pmod, ya, yb, yc, yd, w_out = merge_in
        branch = pl.BlockSpec((None, ts, BRANCH_W), lambda b, i: (b, i, 0))
        in_specs += [mod_spec(pmod), branch, branch, branch, branch, full(w_out)]
        args += [pmod, ya, yb, yc, yd, w_out]

    def rows_out(width):
        return (pl.BlockSpec((None, ts, width), lambda b, i: (b, i, 0)),
                jax.ShapeDtypeStruct((bsz, seq, width), BF16))

    def cols_out(height):
        return (pl.BlockSpec((None, height, ts), lambda b, i: (b, 0, i)),
                jax.ShapeDtypeStruct((bsz, height, seq), BF16))

    if kv_only:
        outs = [rows_out(512), cols_out(256), rows_out(512), cols_out(256)]
    else:
        outs = [rows_out(512), rows_out(512), cols_out(256), rows_out(256), rows_out(512), cols_out(256),
                rows_out(256), rows_out(256), rows_out(256), rows_out(256), rows_out(256)]
    if emit_x:
        outs.insert(0, (pl.BlockSpec((None, ts, D_MODEL), lambda b, i: (b, i, 0)),
                        jax.ShapeDtypeStruct(xs.shape, F32)))
    return pl.pallas_call(
        functools.partial(_proj_kernel, rope=rope, kv_only=kv_only, merge=merge_in is not None,
                          emit_x=emit_x),
        grid=(bsz, seq // ts),
        in_specs=in_specs,
        out_specs=[o[0] for o in outs],
        out_shape=[o[1] for o in outs],
        compiler_params=pltpu.CompilerParams(
            dimension_semantics=("arbitrary", "arbitrary"), vmem_limit_bytes=VMEM_LIMIT),
        name="projection",
    )(*args)


def _interleave(major, minor):
    out, j = [], 0
    for i, th in enumerate(major):
        out.append(th)
        want = ((i + 1) * len(minor)) // len(major)
        out.extend(minor[j:want])
        j = want
    return out


def _attn_kernel(*refs, n_src, tq, n_qt, nb):
    carry = n_qt > 1
    it = iter(refs)
    q_refs = (next(it), next(it))
    g_ref = next(it)
    srcs = [(next(it), next(it), next(it)) for _ in range(n_src)]
    nxt = None
    if carry:
        nxt = ((next(it), next(it)), [(next(it), next(it)) for _ in range(n_src)])
    o_ref = next(it)
    vt_s = (next(it), next(it))
    s_bufs = ((next(it), next(it)), (next(it), next(it)))
    m_s = (next(it), next(it)) if carry else None
    if nb == 1:
        _attn_body(q_refs, g_ref, srcs, nxt, o_ref, vt_s, s_bufs, m_s, tq=tq, n_qt=n_qt)
        return
    for i in range(nb):
        def at(r):
            return r.at[i]
        _attn_body(tuple(map(at, q_refs)), at(g_ref), [tuple(map(at, s)) for s in srcs], None,
                   at(o_ref), tuple(map(at, vt_s)), tuple(tuple(map(at, sl)) for sl in s_bufs), None,
                   tq=tq, n_qt=n_qt)


def _attn_body(q_refs, g_ref, srcs, nxt, o_ref, vt_s, s_bufs, m_s, *, tq, n_qt):
    carry = n_qt > 1
    if carry:
        qn_refs, ksrcs_next = nxt
    ksrcs = [(ke, ko) for ke, ko, _ in srcs]

    off = 0
    for _, _, vt in srcs:
        n = vt.shape[1]
        vt_s[0][0:MLA_V_DIM, off:off + n] = vt[0:MLA_V_DIM, :]
        vt_s[1][0:MLA_V_DIM, off:off + n] = vt[MLA_V_DIM:2 * MLA_V_DIM, :]
        off += n
    n_keys = off
    ones = jnp.ones((V_ROWS - MLA_V_DIM, n_keys), BF16)
    vt_s[0][MLA_V_DIM:V_ROWS, :] = ones
    vt_s[1][MLA_V_DIM:V_ROWS, :] = ones

    def rows(t):
        if isinstance(t, int):
            return slice(t * tq, (t + 1) * tq)
        return pl.ds(pl.multiple_of(t * tq, tq), tq)

    def score_thunks(q_of, keys, slot, m_out):
        per_head = ([], [])
        for h in range(2):
            off = 0
            for pair in keys:
                n = pair[h].shape[0]
                for lo in range(0, n, SCORE_ROWS[h]):
                    hi = min(lo + SCORE_ROWS[h], n)

                    def th(h=h, k_ref=pair[h], lo=lo, hi=hi, off=off):
                        s = _dot_nt(k_ref[lo:hi, :], q_of(h))
                        s_bufs[slot][h][off + lo:off + hi, :] = s
                        mx = jnp.max(s, axis=0, keepdims=True)
                        m_out[h] = mx if m_out[h] is None else jnp.maximum(m_out[h], mx)
                    per_head[h].append(th)
                off += n
        return _interleave(per_head[0], per_head[1])

    def own_scores(t, slot, m_out):
        return score_thunks(lambda h: q_refs[h][rows(t), :], ksrcs, slot, m_out)

    def value_thunks(t, slot, m_in):
        acc = [None, None]
        per_head = ([], [])
        for h in range(2):
            for c in range(0, n_keys, EXP_ROWS[h]):
                def th(h=h, c=c, e=min(c + EXP_ROWS[h], n_keys)):
                    p = jnp.exp2(s_bufs[slot][h][c:e, :] - m_in[h]).astype(BF16)
                    part = _dot(vt_s[h][:, c:e], p)
                    acc[h] = part if acc[h] is None else acc[h] + part
                per_head[h].append(th)
        thunks = _interleave(per_head[0], per_head[1])

        def fin():
            outs = [a[0:MLA_V_DIM, :] / a[MLA_V_DIM:MLA_V_DIM + 1, :] for a in acc]
            att = jnp.transpose(jnp.concatenate(outs, axis=0))
            o_ref[rows(t), :] = (att * g_ref[rows(t), :]).astype(BF16)
        thunks.append(fin)
        return thunks

    def run(thunks):
        for th in thunks:
            th()

    def stage(scores, t_val, slot_val, m_val):
        m_new = [None, None]
        run(_interleave(value_thunks(t_val, slot_val, m_val), scores(m_new)))
        return m_new

    if not carry:
        m0 = [None, None]
        run(own_scores(0, 0, m0))
        run(value_thunks(0, 0, m0))
        return

    @pl.when((pl.program_id(0) == 0) & (pl.program_id(1) == 0))
    def _():
        m_first = [None, None]
        run(own_scores(0, 0, m_first))
        for h in range(2):
            m_s[h][...] = m_first[h]

    def body(u, m_in):
        m_a = stage(lambda m: own_scores(2 * u + 1, 1, m), 2 * u, 0, list(m_in))
        m_b = stage(lambda m: own_scores(2 * u + 2, 0, m), 2 * u + 1, 1, m_a)
        return tuple(m_b)
    m_even = lax.fori_loop(0, n_qt // 2 - 1, body, (m_s[0][...], m_s[1][...]))
    m_last = stage(lambda m: own_scores(n_qt - 1, 1, m), n_qt - 2, 0, list(m_even))
    m_next = stage(lambda m: score_thunks(lambda h: qn_refs[h][...], ksrcs_next, 0, m), n_qt - 1, 1, m_last)
    for h in range(2):
        m_s[h][...] = m_next[h]


def _attention(q, gate, sources, *, paired_q):
    bsz, sq, _ = q.shape
    tq = min(ATTN_TILE, sq)
    n_qt = sq // tq
    assert n_qt == 1 or n_qt % 2 == 0

    def q_cols(j):
        return (j, j) if paired_q else (2 * j, 2 * j + 1)

    def following(b, j):
        g = jnp.minimum(2 * b + j + 1, 2 * bsz - 1)
        return g // 2, g % 2

    nb = 1 if n_qt > 1 else ATTN_ITEMS
    assert bsz % nb == 0
    lead = None if nb == 1 else nb

    def scr(shape, dtype):
        return pltpu.VMEM(shape if nb == 1 else (nb,) + shape, dtype)

    in_specs = [pl.BlockSpec((lead, sq, LANES), lambda b, j, h=h: (b, 0, q_cols(j)[h])) for h in range(2)]
    in_specs.append(pl.BlockSpec((lead, sq, LANES), lambda b, j: (b, 0, j)))
    args = [q, q, gate]
    total = 0
    for k, vt in sources:
        n = k.shape[1]
        total += n
        in_specs += [
            pl.BlockSpec((lead, n, LANES), lambda b, j: (b, 0, 2 * j)),
            pl.BlockSpec((lead, n, LANES), lambda b, j: (b, 0, 2 * j + 1)),
            pl.BlockSpec((lead, LANES, n), lambda b, j: (b, j, 0)),
        ]
        args += [k, k, vt]
    scratch = [scr((V_ROWS, total), BF16), scr((V_ROWS, total), BF16)]
    scratch += [scr((total, tq), F32) for _ in range(4)]
    if n_qt > 1:
        def next_q(b, j, h):
            bn, jn = following(b, j)
            return bn, 0, q_cols(jn)[h]

        def next_k(b, j, h):
            bn, jn = following(b, j)
            return bn, 0, 2 * jn + h

        in_specs += [pl.BlockSpec((None, tq, LANES), functools.partial(next_q, h=h)) for h in range(2)]
        args += [q, q]
        for k, _ in sources:
            in_specs += [pl.BlockSpec((None, k.shape[1], LANES), functools.partial(next_k, h=h))
                         for h in range(2)]
            args += [k, k]
        scratch += [pltpu.VMEM((1, tq), F32), pltpu.VMEM((1, tq), F32)]
    return pl.pallas_call(
        functools.partial(_attn_kernel, n_src=len(sources), tq=tq, n_qt=n_qt, nb=nb),
        grid=(bsz // nb, 2),
        in_specs=in_specs,
        out_specs=pl.BlockSpec((lead, sq, LANES), lambda b, j: (b, 0, j)),
        out_shape=jax.ShapeDtypeStruct((bsz, sq, BRANCH_W), BF16),
        scratch_shapes=scratch,
        compiler_params=pltpu.CompilerParams(
            dimension_semantics=("arbitrary", "arbitrary"), vmem_limit_bytes=VMEM_LIMIT),
        name="attention",
    )(*args)


def _fourier_kernel(f_ref, g_ref, cm_ref, alt_ref, rev_ref, bc_ref, bs_ref, wf_ref, o_ref, *, seq):
    half = seq // 2
    nblk = seq // DFT_BLOCK
    f = f_ref[...]
    g_cos = _dot(f, bc_ref[...]).astype(BF16)
    g_sin = _dot(f, bs_ref[...]).astype(BF16)
    y_cos = y_sin = None
    for a in range(nblk):
        rs = slice(a * DFT_BLOCK, (a + 1) * DFT_BLOCK)
        pc = _dot(cm_ref[a], g_cos[rs, :])
        ps = _dot(cm_ref[nblk + a], g_sin[rs, :])
        y_cos = pc if y_cos is None else y_cos + pc
        y_sin = ps if y_sin is None else y_sin + ps
    norm = (seq * FNET_GDIM) ** -0.5
    low = ((y_cos + y_sin) * norm).astype(BF16)
    diff = ((y_cos - y_sin) * norm).astype(BF16)
    y_mid = _dot(alt_ref[...], g_cos)[0:1, :] * norm
    high = _dot(rev_ref[...], diff)
    first_row = lax.broadcasted_iota(jnp.int32, high.shape, 0) == 0
    high = jnp.where(first_row, y_mid, high).astype(BF16)
    wf = wf_ref[...]
    o_ref[0:half, :] = (_dot(low, wf) * g_ref[0:half, :]).astype(BF16)
    o_ref[half:seq, :] = (_dot(high, wf) * g_ref[half:seq, :]).astype(BF16)


def _fourier(f, gate, dft, lw):
    bsz, seq, _ = f.shape

    def const(a):
        return pl.BlockSpec(a.shape, lambda b: (0,) * a.ndim, pipeline_mode=pl.Buffered(1))

    consts = list(dft) + [lw["fnet_w"]]
    return pl.pallas_call(
        functools.partial(_fourier_kernel, seq=seq),
        grid=(bsz,),
        in_specs=[
            pl.BlockSpec((None, seq, BRANCH_W), lambda b: (b, 0, 0)),
            pl.BlockSpec((None, seq, BRANCH_W), lambda b: (b, 0, 0)),
        ] + [const(a) for a in consts],
        out_specs=pl.BlockSpec((None, seq, BRANCH_W), lambda b: (b, 0, 0)),
        out_shape=jax.ShapeDtypeStruct((bsz, seq, BRANCH_W), BF16),
        compiler_params=pltpu.CompilerParams(
            dimension_semantics=("arbitrary",), vmem_limit_bytes=VMEM_LIMIT),
        name="fourier",
    )(f, gate, *consts)


def _merge_kernel(x_ref, mod_ref, ya_ref, yb_ref, yc_ref, yd_ref, w_ref, o_ref):
    cat = jnp.concatenate([ya_ref[...], yb_ref[...], yc_ref[...], yd_ref[...]], axis=1)
    o_ref[...] = x_ref[...] + mod_ref[2:3, :] * _dot(cat, w_ref[...])


def _merge(xs, mod, ya, yb, yc, yd, w_out):
    bsz, seq, _ = xs.shape
    ts = min(MERGE_TILE, seq)
    per_sample = mod.shape[0] == bsz
    branch = pl.BlockSpec((None, ts, BRANCH_W), lambda b, i: (b, i, 0))
    return pl.pallas_call(
        _merge_kernel,
        grid=(bsz, seq // ts),
        in_specs=[
            pl.BlockSpec((None, ts, D_MODEL), lambda b, i: (b, i, 0)),
            pl.BlockSpec((None, 3, D_MODEL), (lambda b, i: (b, 0, 0)) if per_sample else (lambda b, i: (0, 0, 0))),
            branch, branch, branch, branch,
            pl.BlockSpec(w_out.shape, lambda b, i: (0, 0)),
        ],
        out_specs=pl.BlockSpec((None, ts, D_MODEL), lambda b, i: (b, i, 0)),
        out_shape=jax.ShapeDtypeStruct(xs.shape, F32),
        compiler_params=pltpu.CompilerParams(
            dimension_semantics=("arbitrary", "arbitrary"), vmem_limit_bytes=VMEM_LIMIT),
        name="merge",
    )(xs, mod, ya, yb, yc, yd, w_out)


def _rope_table(seq, segments, half):
    idx = np.zeros(LANES, np.int32)
    active = np.zeros(LANES, bool)
    use_col = np.zeros(LANES, bool)
    first = np.zeros(LANES, bool)
    dim = segments[0][1]
    for start, d, kind in segments:
        assert d == dim and d // 2 == half
        for t in range(d):
            idx[start + t] = t % (d // 2)
            active[start + t] = True
            use_col[start + t] = kind == 1
            first[start + t] = t < d // 2
    inv = ROPE_THETA ** (-jnp.arange(0, dim, 2, dtype=F32) / dim)
    inv_lane = inv[idx][None, :]
    n_grid_rows = seq // GRID_W

    def expand(per_row, per_col):
        by_row = jnp.broadcast_to(per_row[:, None, :], (n_grid_rows, GRID_W, LANES)).reshape(seq, LANES)
        by_col = jnp.broadcast_to(per_col[None, :, :], (n_grid_rows, GRID_W, LANES)).reshape(seq, LANES)
        return jnp.where(use_col[None, :], by_col, by_row)

    ang_row = jnp.arange(n_grid_rows, dtype=jnp.int32).astype(F32)[:, None] * inv_lane
    ang_col = jnp.arange(GRID_W, dtype=jnp.int32).astype(F32)[:, None] * inv_lane
    act = active[None, :]
    cos = jnp.where(act, expand(jnp.cos(ang_row), jnp.cos(ang_col)), 1.0)
    sin = expand(jnp.sin(ang_row), jnp.sin(ang_col))
    sin_a = jnp.where(act & first[None, :], -sin, 0.0)
    sin_b = jnp.where(act & ~first[None, :], sin, 0.0)
    return jnp.stack([cos, sin_a, sin_b]).astype(F32)


def _dft_tables(seq):
    def trig(prod, period):
        ang = (prod % period).astype(F32) * (2.0 * np.pi / period)
        return jnp.cos(ang), jnp.sin(ang)

    half = seq // 2
    nblk = seq // DFT_BLOCK
    k = jnp.arange(half, dtype=jnp.int32)
    ca, sa = trig((jnp.arange(nblk, dtype=jnp.int32) * DFT_BLOCK)[:, None] * k[None, :], seq)
    cb, sb = trig(k[:, None] * jnp.arange(DFT_BLOCK, dtype=jnp.int32)[None, :], seq)
    ca, sa, cb, sb = ca[:, :, None], sa[:, :, None], cb[None], sb[None]
    cm = jnp.concatenate([ca * cb - sa * sb, -(sa * cb + ca * sb)], axis=0).astype(BF16)
    pos = jnp.arange(seq, dtype=jnp.int32)
    alt = jnp.where(jnp.arange(8, dtype=jnp.int32)[:, None] == 0,
                    (1 - 2 * (pos % 2)).astype(F32)[None, :], 0.0).astype(BF16)
    rev = ((k[:, None] + k[None, :] == half) & (k[:, None] >= 1)).astype(BF16)
    c = jnp.arange(BRANCH_W, dtype=jnp.int32)
    same = (c[:, None] // FNET_GDIM) == (c[None, :] // FNET_GDIM)
    cc, cs = trig((c[:, None] % FNET_GDIM) * (c[None, :] % FNET_GDIM), FNET_GDIM)
    bc = jnp.where(same, cc, 0.0).astype(BF16)
    bs = jnp.where(same, cs, 0.0).astype(BF16)
    return cm, alt, rev, bc, bs


def _pad_cols(a, width):
    return jnp.pad(a, ((0, 0), (0, width - a.shape[1])))


def _layer_weights(l, norm_g, w_in, mla_q_norm, mla_w_uq, mla_kv_norm, mla_w_ukv, mla_qn, mla_kn,
                   gqa_qn, gqa_kn, cm_ln_g, cm_ln_b, cm_w_s, cm_b_s, fnet_w, w_out):
    w = w_in[l].astype(BF16)
    sp = np.cumsum([0, 192, 128, 32, 256, 256, 128, 128, 256, 256, 256, 256, 256, 256])
    nat = [w[:, sp[i]:sp[i + 1]] for i in range(13)]
    cq, ckv, kr, ga, q2, k2, v2, gb, u, vc, gc, f, gd = nat
    def zeros(n):
        return jnp.zeros((D_MODEL, n), w.dtype)

    w_in_p = jnp.concatenate(
        [cq, zeros(256 - MLA_Q_RANK), ckv,
         zeros(MLA_NOPE_DIM), kr, zeros(LANES - MLA_NOPE_DIM - MLA_ROPE_DIM),
         k2, v2, gb, ga, q2, u, vc, gc, f, gd], axis=1).astype(BF16)
    assert w_in_p.shape[1] == C_END

    seg = np.arange(MLA_ROPE_DIM) % 16
    partner = MLA_NOPE_DIM + (np.arange(MLA_ROPE_DIM) // 16) * 16 + (seg + 8) % 16
    perm = np.concatenate([np.arange(MLA_NOPE_DIM), partner])
    wuq = mla_w_uq[l].reshape(MLA_Q_RANK, MLA_HEADS, MLA_QK_DIM)
    wuq = jnp.concatenate([wuq, wuq[:, :, perm]], axis=1)
    wuq = jnp.pad(wuq, ((0, 256 - MLA_Q_RANK), (0, 0), (0, LANES - MLA_QK_DIM)))
    wuq = wuq.reshape(256, 2 * MLA_HEADS * LANES).astype(BF16)
    wukv = mla_w_ukv[l].reshape(MLA_KV_RANK, MLA_HEADS, MLA_NOPE_DIM + MLA_V_DIM)
    wk = jnp.pad(wukv[:, :, :MLA_NOPE_DIM], ((0, 0), (0, 0), (0, LANES - MLA_NOPE_DIM)))
    wv = wukv[:, :, MLA_NOPE_DIM:]
    wukv_p = jnp.concatenate(
        [wk.reshape(MLA_KV_RANK, MLA_HEADS * LANES), wv.reshape(MLA_KV_RANK, MLA_HEADS * MLA_V_DIM)],
        axis=1).astype(BF16)

    def row(v):
        return _pad_cols(v[None, :].astype(F32), 256)

    pvec = jnp.concatenate([
        row(mla_q_norm[l]), row(mla_kv_norm[l]), row(mla_qn[l]), row(mla_kn[l]),
        row(jnp.tile(gqa_qn[l], 2)), row(jnp.tile(gqa_kn[l], 2)), row(cm_ln_g[l]), row(cm_ln_b[l]),
        row(mla_qn[l][perm])] + [row(jnp.zeros((1,), F32))] * 7, axis=0)

    w_cm = jnp.transpose(cm_w_s[l], (1, 0, 2)).reshape(CHUNK, CM_GROUPS * CHUNK).astype(BF16)
    b_cm = jnp.broadcast_to(cm_b_s[l].T[:, :, None], (CHUNK, CM_GROUPS, BRANCH_W // CM_GROUPS))
    b_cm = b_cm.reshape(CHUNK, BRANCH_W).astype(F32)
    return dict(norm_g=norm_g[l][None, :], w_in=w_in_p, pvec=pvec, w_uq=wuq, w_ukv=wukv_p,
                w_cm=w_cm, b_cm=b_cm, fnet_w=fnet_w[l].astype(BF16), w_out=w_out[l].astype(BF16))


def kernel(x, c, ctx, c_ctx, norm_g, w_mod, b_mod, w_in, mla_q_norm, mla_w_uq, mla_kv_norm, mla_w_ukv,
           mla_qn, mla_kn, gqa_qn, gqa_kn, cm_ln_g, cm_ln_b, cm_w_s, cm_b_s, fnet_w, w_out):
    bsz, seq, _ = x.shape
    ctx_len = ctx.shape[1]
    depth = w_in.shape[0]

    rows = -(-(bsz + 1) // 8) * 8
    cc = jnp.concatenate([c, c_ctx[None, :], jnp.zeros((rows - bsz - 1, D_MODEL), F32)], axis=0)
    mod_all = _modulation(cc, w_mod, b_mod)

    rope_a = _rope_table(seq, [(MLA_NOPE_DIM, 16, 0), (MLA_NOPE_DIM + 16, 16, 1)], 8)
    rope_b = _rope_table(seq, [(0, 32, 0), (32, 32, 1), (64, 32, 0), (96, 32, 1)], 16)
    dft_x = _dft_tables(seq)
    dft_c = _dft_tables(ctx_len)

    pending_x = pending_c = None
    for l in range(depth):
        lw = _layer_weights(l, norm_g, w_in, mla_q_norm, mla_w_uq, mla_kv_norm, mla_w_ukv, mla_qn,
                            mla_kn, gqa_qn, gqa_kn, cm_ln_g, cm_ln_b, cm_w_s, cm_b_s, fnet_w, w_out)
        mod_x = mod_all[l, :bsz].reshape(bsz, 3, D_MODEL)
        mod_c = mod_all[l, bsz:bsz + 1].reshape(1, 3, D_MODEL)
        update_ctx = l < depth - 1

        outs = _projection(x, mod_x, lw, (rope_a, rope_b), kv_only=False,
                           merge_in=pending_x, emit_x=pending_x is not None)
        if pending_x is not None:
            x, outs = outs[0], outs[1:]
        (q1, k1, v1t, q2, k2, v2t, sga, sgb, yc, f, sgd) = outs
        outs_c = _projection(ctx, mod_c, lw, None, kv_only=not update_ctx,
                             merge_in=pending_c, emit_x=pending_c is not None and update_ctx)
        if pending_c is not None and update_ctx:
            ctx, outs_c = outs_c[0], outs_c[1:]
        if update_ctx:
            (q1c, k1c, v1tc, q2c, k2c, v2tc, sgac, sgbc, ycc, fc, sgdc) = outs_c
        else:
            k1c, v1tc, k2c, v2tc = outs_c

        ya = _attention(q1, sga, [(k1, v1t), (k1c, v1tc)], paired_q=False)
        yb = _attention(q2, sgb, [(k2, v2t), (k2c, v2tc)], paired_q=True)
        yd = _fourier(f, sgd, dft_x, lw)
        pending_x = (mod_x, ya, yb, yc, yd, lw["w_out"])
        pending_c = None
        if update_ctx:
            yac = _attention(q1c, sgac, [(k1c, v1tc)], paired_q=False)
            ybc = _attention(q2c, sgbc, [(k2c, v2tc)], paired_q=True)
            ydc = _fourier(fc, sgdc, dft_c, lw)
            pending_c = (mod_c, yac, ybc, ycc, ydc, lw["w_out"])
    return _merge(x, *pending_x)
```

```python
import functools

import numpy as np
import jax
import jax.numpy as jnp
from jax import lax
from jax.experimental import pallas as pl
from jax.experimental.pallas import tpu as pltpu

D_MODEL = 1024
GRID_W = 64
BRANCH_W = 256
MLA_HEADS = 4
MLA_NOPE_DIM = 64
MLA_ROPE_DIM = 32
MLA_QK_DIM = 96
MLA_V_DIM = 64
MLA_Q_RANK = 192
MLA_KV_RANK = 128
GQA_HEADS = 4
GQA_KV_HEADS = 2
GQA_HEAD_DIM = 64
CHUNK = 128
CM_GROUPS = 4
FNET_GROUPS = 4
FNET_GDIM = BRANCH_W // FNET_GROUPS
ROPE_THETA = 10000.0
EPS = 1e-6
LOG2_E = 1.4426950408889634

LANES = 128
V_ROWS = MLA_V_DIM + 16
PROJ_TILE = 512
PROJ_PARTS = 2
MERGE_TILE = 1024
DFT_BLOCK = 256
ATTN_TILE = 256
ATTN_ITEMS = 8
SCORE_ROWS = (256, 256)
EXP_ROWS = (256, 256)
VMEM_LIMIT = 48 * 1024 * 1024

F32 = jnp.float32
BF16 = jnp.bfloat16

C_CQ, C_CKV, C_KR, C_K2, C_V2, C_GB, C_GA, C_Q2, C_U, C_VC, C_GC, C_F, C_GD, C_END = (
    0, 256, 384, 512, 640, 768, 1024, 1280, 1536, 1792, 2048, 2304, 2560, 2816)


def _dot(a, b):
    return jnp.dot(a, b, preferred_element_type=F32)


def _dot_nt(a, b):
    return lax.dot_general(a, b, (((1,), (1,)), ((), ())), preferred_element_type=F32)


def _silu(x):
    return x * jax.nn.sigmoid(x)


def _lane_id(shape):
    return lax.broadcasted_iota(jnp.int32, shape, len(shape) - 1)


def _mod_kernel(c_ref, w_ref, b_ref, o_ref):
    c = c_ref[...]
    o_ref[...] = _dot(_silu(c).astype(BF16), w_ref[...].astype(BF16)) + b_ref[...]


def _modulation(cc, w_mod, b_mod):
    n_layers = w_mod.shape[0]
    rows = cc.shape[0]
    return pl.pallas_call(
        _mod_kernel,
        grid=(n_layers, 3),
        in_specs=[
            pl.BlockSpec((rows, D_MODEL), lambda l, j: (0, 0)),
            pl.BlockSpec((None, D_MODEL, D_MODEL), lambda l, j: (l, 0, j)),
            pl.BlockSpec((None, 1, D_MODEL), lambda l, j: (l, 0, j)),
        ],
        out_specs=pl.BlockSpec((None, rows, D_MODEL), lambda l, j: (l, 0, j)),
        out_shape=jax.ShapeDtypeStruct((n_layers, rows, 3 * D_MODEL), F32),
        compiler_params=pltpu.CompilerParams(vmem_limit_bytes=VMEM_LIMIT),
        name="modulation",
    )(cc, w_mod, b_mod.reshape(n_layers, 1, 3 * D_MODEL))


def _rope(y, tab_ref, rs, half):
    width = y.shape[1]
    reps = width // LANES
    cos, sin_a, sin_b = tab_ref[0, rs, :], tab_ref[1, rs, :], tab_ref[2, rs, :]
    if reps > 1:
        cos = jnp.concatenate([cos] * reps, axis=1)
        sin_a = jnp.concatenate([sin_a] * reps, axis=1)
        sin_b = jnp.concatenate([sin_b] * reps, axis=1)
    return (y * cos + pltpu.roll(y, width - half, 1) * sin_a + pltpu.roll(y, half, 1) * sin_b)


def _head_rms_64(x, gain):
    outs = []
    for g in range(x.shape[1] // LANES):
        xg = x[:, g * LANES:(g + 1) * LANES]
        lo = _lane_id(xg.shape) < GQA_HEAD_DIM
        x2 = xg * xg
        s_lo = jnp.sum(jnp.where(lo, x2, 0.0), axis=-1, keepdims=True)
        s_hi = jnp.sum(jnp.where(lo, 0.0, x2), axis=-1, keepdims=True)
        ms = jnp.where(lo, s_lo, s_hi) * (1.0 / GQA_HEAD_DIM)
        outs.append(xg * lax.rsqrt(ms + EPS) * gain)
    return jnp.concatenate(outs, axis=1) if len(outs) > 1 else outs[0]


def _proj_kernel(*refs, rope, kv_only, merge, emit_x, split_cols):
    it = iter(refs)
    x_ref, mod_ref, ng_ref, w_ref, pv_ref, wuq_ref, wukv_ref, wcm_ref, bcm_ref = (
        next(it) for _ in range(9))
    ra_ref = rb_ref = None
    if rope:
        ra_ref, rb_ref = next(it), next(it)
    if merge:
        pmod_ref, ya_ref, yb_ref, yc_ref, yd_ref, wout_ref = (next(it) for _ in range(6))
    x_o = next(it) if emit_x else None
    if kv_only:
        k1_o, v1t_o, k2_o, v2t_o = (next(it) for _ in range(4))
    else:
        (q1_o, k1_o, v1t_o, q2_o, k2_o, v2t_o, sga_o, sgb_o, yc_o, f_o, sgd_o) = (
            next(it) for _ in range(11))

    n_rows = x_ref.shape[0]
    part = max(n_rows // PROJ_PARTS, CHUNK)
    halves = [slice(r, r + part) for r in range(0, n_rows, part)]
    shift, scale = mod_ref[0:1, :], mod_ref[1:2, :]
    hb_parts = []
    for rs in halves:
        x = x_ref[rs, :]
        if merge:
            cat = jnp.concatenate([ya_ref[rs, :], yb_ref[rs, :], yc_ref[rs, :], yd_ref[rs, :]], axis=1)
            x = x + pmod_ref[2:3, :] * _dot(cat, wout_ref[...])
        if emit_x:
            x_o[rs, :] = x
        ms = jnp.mean(x * x, axis=-1, keepdims=True)
        h = (x * lax.rsqrt(ms + EPS) * ng_ref[...]) * (1.0 + scale) + shift
        hb_parts.append(h.astype(BF16))
    hb = jnp.concatenate(hb_parts, axis=0) if len(hb_parts) > 1 else hb_parts[0]

    def proj(lo, hi):
        return _dot(hb, w_ref[:, lo:hi])

    def put_cols(ref, height, rs, value):
        if split_cols:
            ref[rs.start // part, height, :] = value
        else:
            ref[height, rs] = value

    g_q, g_kv = pv_ref[0:1, :], pv_ref[1:2, 0:LANES]
    g_qn, g_kn = pv_ref[2:3, 0:LANES], pv_ref[3:4, 0:LANES]
    g_q2, g_k2 = pv_ref[4:5, 0:LANES], pv_ref[5:6, 0:LANES]
    ln_g, ln_b = pv_ref[6:7, :], pv_ref[7:8, :]
    g_qn_perm = pv_ref[8:9, 0:LANES]

    firsts = []
    for rs, hb_half in zip(halves, hb_parts):
        d0 = _dot(hb_half, w_ref[:, (C_CKV if kv_only else C_CQ):C_K2])
        cq = None if kv_only else d0[:, 0:2 * LANES]
        ckv, kr = d0[:, -2 * LANES:-LANES], d0[:, -LANES:]
        msk = jnp.mean(ckv * ckv, axis=-1, keepdims=True)
        ckvn = (ckv * lax.rsqrt(msk + EPS) * g_kv).astype(BF16)
        cqn = None
        if not kv_only:
            msq = jnp.sum(cq * cq, axis=-1, keepdims=True) * (1.0 / MLA_Q_RANK)
            cqn = (cq * lax.rsqrt(msq + EPS) * g_q).astype(BF16)
        firsts.append((ckvn, kr, cqn))
    for rs, (ckvn, kr, _) in zip(halves, firsts):
        kv = _dot(ckvn, wukv_ref[...])
        krg = kr * g_kn
        if rope:
            krg = _rope(krg, ra_ref, rs, MLA_ROPE_DIM // 4)
        kr_sq = jnp.sum(kr * kr, axis=-1, keepdims=True)
        for hh in range(MLA_HEADS):
            kn = kv[:, hh * LANES:(hh + 1) * LANES]
            msh = (jnp.sum(kn * kn, axis=-1, keepdims=True) + kr_sq) * (1.0 / MLA_QK_DIM)
            k1_o[rs, hh * LANES:(hh + 1) * LANES] = ((kn * g_kn + krg) * lax.rsqrt(msh + EPS)).astype(BF16)
        put_cols(v1t_o, slice(None), rs, jnp.transpose(kv[:, 4 * LANES:]).astype(BF16))

    d1 = proj(C_K2, C_GB if kv_only else C_GA)
    k2 = _head_rms_64(d1[:, 0:LANES], g_k2)
    if rope:
        k2 = _rope(k2, rb_ref, slice(None), GQA_HEAD_DIM // 4)
    lo = _lane_id(k2.shape) < GQA_HEAD_DIM
    k2r = pltpu.roll(k2, GQA_HEAD_DIM, 1)
    zero = jnp.zeros_like(k2)
    k2_o[:, 0 * LANES:1 * LANES] = jnp.where(lo, k2, zero).astype(BF16)
    k2_o[:, 1 * LANES:2 * LANES] = jnp.where(lo, zero, k2r).astype(BF16)
    k2_o[:, 2 * LANES:3 * LANES] = jnp.where(lo, k2r, zero).astype(BF16)
    k2_o[:, 3 * LANES:4 * LANES] = jnp.where(lo, zero, k2).astype(BF16)
    v2t = jnp.transpose(d1[:, LANES:2 * LANES]).astype(BF16)
    hd = GQA_HEAD_DIM
    for rs in halves:
        for dst, src in ((0, 0), (1, 0), (2, 1), (3, 1)):
            put_cols(v2t_o, slice(dst * hd, (dst + 1) * hd), rs, v2t[src * hd:(src + 1) * hd, rs])
    if kv_only:
        return
    sgb_o[...] = _silu(d1[:, 2 * LANES:]).astype(BF16)

    for rs, (_, _, cqn) in zip(halves, firsts):
        qq = _dot(cqn, wuq_ref[...])
        for hh in range(MLA_HEADS):
            qs = qq[:, hh * LANES:(hh + 1) * LANES]
            r = lax.rsqrt(jnp.sum(qs * qs, axis=-1, keepdims=True) * (1.0 / MLA_QK_DIM) + EPS)
            r = r * (LOG2_E * MLA_QK_DIM ** -0.5)
            y = qs * g_qn
            if rope:
                qp = qq[:, (MLA_HEADS + hh) * LANES:(MLA_HEADS + hh + 1) * LANES]
                y = y * ra_ref[0, rs, :] + (qp * g_qn_perm) * (ra_ref[1, rs, :] + ra_ref[2, rs, :])
            q1_o[rs, hh * LANES:(hh + 1) * LANES] = (y * r).astype(BF16)

    d2 = proj(C_GA, C_U)
    sga_o[...] = _silu(d2[:, 0:BRANCH_W]).astype(BF16)
    q2 = _head_rms_64(d2[:, BRANCH_W:], g_q2)
    if rope:
        q2 = _rope(q2, rb_ref, slice(None), GQA_HEAD_DIM // 4)
    q2_o[...] = (q2 * (LOG2_E * GQA_HEAD_DIM ** -0.5)).astype(BF16)

    d3 = proj(C_U, C_GC)
    d4 = proj(C_GC, C_GD)
    u, vc = d3[:, 0:BRANCH_W], d3[:, BRANCH_W:]
    sgc = _silu(d4[:, 0:BRANCH_W])
    f_o[...] = d4[:, BRANCH_W:].astype(BF16)
    sgd_o[...] = _silu(proj(C_GD, C_END)).astype(BF16)
    mu = jnp.mean(vc, axis=-1, keepdims=True)
    var = jnp.mean(jnp.square(vc - mu), axis=-1, keepdims=True)
    vn = ((vc - mu) * lax.rsqrt(var + EPS) * ln_g + ln_b).astype(BF16)
    grp = lax.shift_right_logical(_lane_id((CHUNK, BRANCH_W)), 6)
    for c in range(n_rows // CHUNK):
        cr = slice(c * CHUNK, (c + 1) * CHUNK)
        vnc = vn[cr, :]
        stacked = jnp.concatenate(
            [jnp.where(grp == g, vnc, jnp.zeros_like(vnc)) for g in range(CM_GROUPS)], axis=0)
        s = _dot(wcm_ref[...], stacked) + bcm_ref[...]
        yc_o[cr, :] = (u[cr, :] * s * sgc[cr, :]).astype(BF16)


def _projection(xs, mod, lw, rope_tabs, *, kv_only, merge_in=None, emit_x=False, items=1):
    true_bsz, true_seq, _ = xs.shape
    if items > 1:
        assert rope_tabs is None and mod.shape[0] == 1 and items == PROJ_PARTS
        assert true_bsz % items == 0 and items * true_seq == PROJ_TILE

        def fold(a):
            return a.reshape(true_bsz // items, items * true_seq, a.shape[-1])
        xs = fold(xs)
        if merge_in is not None:
            merge_in = (merge_in[0],) + tuple(fold(a) for a in merge_in[1:5]) + (merge_in[5],)
    bsz, seq, _ = xs.shape
    ts = min(PROJ_TILE, seq)
    rope = rope_tabs is not None

    def full(a):
        nd = a.ndim
        return pl.BlockSpec(a.shape, lambda b, i: (0,) * nd)

    def mod_spec(m):
        per_sample = m.shape[0] == bsz
        return pl.BlockSpec((None, 3, D_MODEL), (lambda b, i: (b, 0, 0)) if per_sample else (lambda b, i: (0, 0, 0)))

    in_specs = [
        pl.BlockSpec((None, ts, D_MODEL), lambda b, i: (b, i, 0)), mod_spec(mod),
        full(lw["norm_g"]), full(lw["w_in"]), full(lw["pvec"]), full(lw["w_uq"]), full(lw["w_ukv"]),
        full(lw["w_cm"]), full(lw["b_cm"]),
    ]
    args = [xs, mod, lw["norm_g"], lw["w_in"], lw["pvec"], lw["w_uq"], lw["w_ukv"], lw["w_cm"], lw["b_cm"]]
    if rope:
        for t in rope_tabs:
            in_specs.append(pl.BlockSpec((3, ts, LANES), lambda b, i: (0, i, 0)))
            args.append(t)
    if merge_in is not None:
        pmod, ya, yb, yc, yd, w_out = merge_in
        branch = pl.BlockSpec((None, ts, BRANCH_W), lambda b, i: (b, i, 0))
        in_specs += [mod_spec(pmod), branch, branch, branch, branch, full(w_out)]
        args += [pmod, ya, yb, yc, yd, w_out]

    def rows_out(width):
        return (pl.BlockSpec((None, ts, width), lambda b, i: (b, i, 0)),
                jax.ShapeDtypeStruct((bsz, seq, width), BF16))

    def cols_out(height):
        if items > 1:
            return (pl.BlockSpec((items, height, true_seq), lambda b, i: (b, 0, 0)),
                    jax.ShapeDtypeStruct((true_bsz, height, true_seq), BF16))
        return (pl.BlockSpec((None, height, ts), lambda b, i: (b, 0, i)),
                jax.ShapeDtypeStruct((bsz, height, seq), BF16))

    if kv_only:
        outs = [rows_out(512), cols_out(256), rows_out(512), cols_out(256)]
    else:
        outs = [rows_out(512), rows_out(512), cols_out(256), rows_out(256), rows_out(512), cols_out(256),
                rows_out(256), rows_out(256), rows_out(256), rows_out(256), rows_out(256)]
    if emit_x:
        outs.insert(0, (pl.BlockSpec((None, ts, D_MODEL), lambda b, i: (b, i, 0)),
                        jax.ShapeDtypeStruct(xs.shape, F32)))
    results = pl.pallas_call(
        functools.partial(_proj_kernel, rope=rope, kv_only=kv_only, merge=merge_in is not None,
                          emit_x=emit_x, split_cols=items > 1),
        grid=(bsz, seq // ts),
        in_specs=in_specs,
        out_specs=[o[0] for o in outs],
        out_shape=[o[1] for o in outs],
        compiler_params=pltpu.CompilerParams(
            dimension_semantics=("arbitrary", "arbitrary"), vmem_limit_bytes=VMEM_LIMIT),
        name="projection",
    )(*args)
    if items > 1:
        results = [r.reshape(true_bsz, true_seq, r.shape[-1]) if r.shape[0] != true_bsz else r
                   for r in results]
    return results


def _interleave(major, minor):
    out, j = [], 0
    for i, th in enumerate(major):
        out.append(th)
        want = ((i + 1) * len(minor)) // len(major)
        out.extend(minor[j:want])
        j = want
    return out


def _attn_kernel(*refs, n_src, tq, n_qt, nb):
    carry = n_qt > 1
    it = iter(refs)
    q_refs = (next(it), next(it))
    g_ref = next(it)
    srcs = [(next(it), next(it), next(it)) for _ in range(n_src)]
    nxt = None
    if carry:
        nxt = ((next(it), next(it)), [(next(it), next(it)) for _ in range(n_src)])
    o_ref = next(it)
    vt_s = (next(it), next(it))
    s_bufs = ((next(it), next(it)), (next(it), next(it)))
    m_s = (next(it), next(it)) if carry else None
    if nb == 1:
        _attn_body(q_refs, g_ref, srcs, nxt, o_ref, vt_s, s_bufs, m_s, tq=tq, n_qt=n_qt)
        return
    for i in range(nb):
        def at(r):
            return r.at[i]
        _attn_body(tuple(map(at, q_refs)), at(g_ref), [tuple(map(at, s)) for s in srcs], None,
                   at(o_ref), tuple(map(at, vt_s)), tuple(tuple(map(at, sl)) for sl in s_bufs), None,
                   tq=tq, n_qt=n_qt)


def _attn_body(q_refs, g_ref, srcs, nxt, o_ref, vt_s, s_bufs, m_s, *, tq, n_qt):
    carry = n_qt > 1
    if carry:
        qn_refs, ksrcs_next = nxt
    ksrcs = [(ke, ko) for ke, ko, _ in srcs]

    off = 0
    for _, _, vt in srcs:
        n = vt.shape[1]
        vt_s[0][0:MLA_V_DIM, off:off + n] = vt[0:MLA_V_DIM, :]
        vt_s[1][0:MLA_V_DIM, off:off + n] = vt[MLA_V_DIM:2 * MLA_V_DIM, :]
        off += n
    n_keys = off
    ones = jnp.ones((V_ROWS - MLA_V_DIM, n_keys), BF16)
    vt_s[0][MLA_V_DIM:V_ROWS, :] = ones
    vt_s[1][MLA_V_DIM:V_ROWS, :] = ones

    def rows(t):
        if isinstance(t, int):
            return slice(t * tq, (t + 1) * tq)
        return pl.ds(pl.multiple_of(t * tq, tq), tq)

    def score_thunks(q_of, keys, slot, m_out):
        per_head = ([], [])
        for h in range(2):
            off = 0
            for pair in keys:
                n = pair[h].shape[0]
                for lo in range(0, n, SCORE_ROWS[h]):
                    hi = min(lo + SCORE_ROWS[h], n)

                    def th(h=h, k_ref=pair[h], lo=lo, hi=hi, off=off):
                        s = _dot_nt(k_ref[lo:hi, :], q_of(h))
                        s_bufs[slot][h][off + lo:off + hi, :] = s
                        mx = jnp.max(s, axis=0, keepdims=True)
                        m_out[h] = mx if m_out[h] is None else jnp.maximum(m_out[h], mx)
                    per_head[h].append(th)
                off += n
        return _interleave(per_head[0], per_head[1])

    def own_scores(t, slot, m_out):
        return score_thunks(lambda h: q_refs[h][rows(t), :], ksrcs, slot, m_out)

    def value_thunks(t, slot, m_in):
        acc = [None, None]
        per_head = ([], [])
        for h in range(2):
            for c in range(0, n_keys, EXP_ROWS[h]):
                def th(h=h, c=c, e=min(c + EXP_ROWS[h], n_keys)):
                    p = jnp.exp2(s_bufs[slot][h][c:e, :] - m_in[h]).astype(BF16)
                    part = _dot(vt_s[h][:, c:e], p)
                    acc[h] = part if acc[h] is None else acc[h] + part
                per_head[h].append(th)
        thunks = _interleave(per_head[0], per_head[1])

        def fin():
            outs = [a[0:MLA_V_DIM, :] / a[MLA_V_DIM:MLA_V_DIM + 1, :] for a in acc]
            att = jnp.transpose(jnp.concatenate(outs, axis=0))
            o_ref[rows(t), :] = (att * g_ref[rows(t), :]).astype(BF16)
        thunks.append(fin)
        return thunks

    def run(thunks):
        for th in thunks:
            th()

    def stage(scores, t_val, slot_val, m_val):
        m_new = [None, None]
        run(_interleave(value_thunks(t_val, slot_val, m_val), scores(m_new)))
        return m_new

    if not carry:
        m0 = [None, None]
        run(own_scores(0, 0, m0))
        run(value_thunks(0, 0, m0))
        return

    @pl.when((pl.program_id(0) == 0) & (pl.program_id(1) == 0))
    def _():
        m_first = [None, None]
        run(own_scores(0, 0, m_first))
        for h in range(2):
            m_s[h][...] = m_first[h]

    def body(u, m_in):
        m_a = stage(lambda m: own_scores(2 * u + 1, 1, m), 2 * u, 0, list(m_in))
        m_b = stage(lambda m: own_scores(2 * u + 2, 0, m), 2 * u + 1, 1, m_a)
        return tuple(m_b)
    m_even = lax.fori_loop(0, n_qt // 2 - 1, body, (m_s[0][...], m_s[1][...]))
    m_last = stage(lambda m: own_scores(n_qt - 1, 1, m), n_qt - 2, 0, list(m_even))
    m_next = stage(lambda m: score_thunks(lambda h: qn_refs[h][...], ksrcs_next, 0, m), n_qt - 1, 1, m_last)
    for h in range(2):
        m_s[h][...] = m_next[h]


def _attention(q, gate, sources, *, paired_q):
    bsz, sq, _ = q.shape
    tq = min(ATTN_TILE, sq)
    n_qt = sq // tq
    assert n_qt == 1 or n_qt % 2 == 0

    def q_cols(j):
        return (j, j) if paired_q else (2 * j, 2 * j + 1)

    def following(b, j):
        g = jnp.minimum(2 * b + j + 1, 2 * bsz - 1)
        return g // 2, g % 2

    nb = 1 if n_qt > 1 else ATTN_ITEMS
    assert bsz % nb == 0
    lead = None if nb == 1 else nb

    def scr(shape, dtype):
        return pltpu.VMEM(shape if nb == 1 else (nb,) + shape, dtype)

    in_specs = [pl.BlockSpec((lead, sq, LANES), lambda b, j, h=h: (b, 0, q_cols(j)[h])) for h in range(2)]
    in_specs.append(pl.BlockSpec((lead, sq, LANES), lambda b, j: (b, 0, j)))
    args = [q, q, gate]
    total = 0
    for k, vt in sources:
        n = k.shape[1]
        total += n
        in_specs += [
            pl.BlockSpec((lead, n, LANES), lambda b, j: (b, 0, 2 * j)),
            pl.BlockSpec((lead, n, LANES), lambda b, j: (b, 0, 2 * j + 1)),
            pl.BlockSpec((lead, LANES, n), lambda b, j: (b, j, 0)),
        ]
        args += [k, k, vt]
    scratch = [scr((V_ROWS, total), BF16), scr((V_ROWS, total), BF16)]
    scratch += [scr((total, tq), F32) for _ in range(4)]
    if n_qt > 1:
        def next_q(b, j, h):
            bn, jn = following(b, j)
            return bn, 0, q_cols(jn)[h]

        def next_k(b, j, h):
            bn, jn = following(b, j)
            return bn, 0, 2 * jn + h

        in_specs += [pl.BlockSpec((None, tq, LANES), functools.partial(next_q, h=h)) for h in range(2)]
        args += [q, q]
        for k, _ in sources:
            in_specs += [pl.BlockSpec((None, k.shape[1], LANES), functools.partial(next_k, h=h))
                         for h in range(2)]
            args += [k, k]
        scratch += [pltpu.VMEM((1, tq), F32), pltpu.VMEM((1, tq), F32)]
    return pl.pallas_call(
        functools.partial(_attn_kernel, n_src=len(sources), tq=tq, n_qt=n_qt, nb=nb),
        grid=(bsz // nb, 2),
        in_specs=in_specs,
        out_specs=pl.BlockSpec((lead, sq, LANES), lambda b, j: (b, 0, j)),
        out_shape=jax.ShapeDtypeStruct((bsz, sq, BRANCH_W), BF16),
        scratch_shapes=scratch,
        compiler_params=pltpu.CompilerParams(
            dimension_semantics=("arbitrary", "arbitrary"), vmem_limit_bytes=VMEM_LIMIT),
        name="attention",
    )(*args)


def _fourier_kernel(f_ref, g_ref, cm_ref, alt_ref, rev_ref, bc_ref, bs_ref, wf_ref, o_ref, *, seq):
    half = seq // 2
    nblk = seq // DFT_BLOCK
    f = f_ref[...]
    g_cos = _dot(f, bc_ref[...]).astype(BF16)
    g_sin = _dot(f, bs_ref[...]).astype(BF16)
    y_cos = y_sin = None
    for a in range(nblk):
        rs = slice(a * DFT_BLOCK, (a + 1) * DFT_BLOCK)
        pc = _dot(cm_ref[a], g_cos[rs, :])
        ps = _dot(cm_ref[nblk + a], g_sin[rs, :])
        y_cos = pc if y_cos is None else y_cos + pc
        y_sin = ps if y_sin is None else y_sin + ps
    norm = (seq * FNET_GDIM) ** -0.5
    low = ((y_cos + y_sin) * norm).astype(BF16)
    diff = ((y_cos - y_sin) * norm).astype(BF16)
    y_mid = _dot(alt_ref[...], g_cos)[0:1, :] * norm
    high = _dot(rev_ref[...], diff)
    first_row = lax.broadcasted_iota(jnp.int32, high.shape, 0) == 0
    high = jnp.where(first_row, y_mid, high).astype(BF16)
    wf = wf_ref[...]
    o_ref[0:half, :] = (_dot(low, wf) * g_ref[0:half, :]).astype(BF16)
    o_ref[half:seq, :] = (_dot(high, wf) * g_ref[half:seq, :]).astype(BF16)


def _fourier(f, gate, dft, lw):
    bsz, seq, _ = f.shape

    def const(a):
        return pl.BlockSpec(a.shape, lambda b: (0,) * a.ndim, pipeline_mode=pl.Buffered(1))

    consts = list(dft) + [lw["fnet_w"]]
    return pl.pallas_call(
        functools.partial(_fourier_kernel, seq=seq),
        grid=(bsz,),
        in_specs=[
            pl.BlockSpec((None, seq, BRANCH_W), lambda b: (b, 0, 0)),
            pl.BlockSpec((None, seq, BRANCH_W), lambda b: (b, 0, 0)),
        ] + [const(a) for a in consts],
        out_specs=pl.BlockSpec((None, seq, BRANCH_W), lambda b: (b, 0, 0)),
        out_shape=jax.ShapeDtypeStruct((bsz, seq, BRANCH_W), BF16),
        compiler_params=pltpu.CompilerParams(
            dimension_semantics=("arbitrary",), vmem_limit_bytes=VMEM_LIMIT),
        name="fourier",
    )(f, gate, *consts)


def _merge_kernel(x_ref, mod_ref, ya_ref, yb_ref, yc_ref, yd_ref, w_ref, o_ref):
    cat = jnp.concatenate([ya_ref[...], yb_ref[...], yc_ref[...], yd_ref[...]], axis=1)
    o_ref[...] = x_ref[...] + mod_ref[2:3, :] * _dot(cat, w_ref[...])


def _merge(xs, mod, ya, yb, yc, yd, w_out):
    bsz, seq, _ = xs.shape
    ts = min(MERGE_TILE, seq)
    per_sample = mod.shape[0] == bsz
    branch = pl.BlockSpec((None, ts, BRANCH_W), lambda b, i: (b, i, 0))
    return pl.pallas_call(
        _merge_kernel,
        grid=(bsz, seq // ts),
        in_specs=[
            pl.BlockSpec((None, ts, D_MODEL), lambda b, i: (b, i, 0)),
            pl.BlockSpec((None, 3, D_MODEL), (lambda b, i: (b, 0, 0)) if per_sample else (lambda b, i: (0, 0, 0))),
            branch, branch, branch, branch,
            pl.BlockSpec(w_out.shape, lambda b, i: (0, 0)),
        ],
        out_specs=pl.BlockSpec((None, ts, D_MODEL), lambda b, i: (b, i, 0)),
        out_shape=jax.ShapeDtypeStruct(xs.shape, F32),
        compiler_params=pltpu.CompilerParams(
            dimension_semantics=("arbitrary", "arbitrary"), vmem_limit_bytes=VMEM_LIMIT),
        name="merge",
    )(xs, mod, ya, yb, yc, yd, w_out)


def _rope_table(seq, segments, half):
    idx = np.zeros(LANES, np.int32)
    active = np.zeros(LANES, bool)
    use_col = np.zeros(LANES, bool)
    first = np.zeros(LANES, bool)
    dim = segments[0][1]
    for start, d, kind in segments:
        assert d == dim and d // 2 == half
        for t in range(d):
            idx[start + t] = t % (d // 2)
            active[start + t] = True
            use_col[start + t] = kind == 1
            first[start + t] = t < d // 2
    inv = ROPE_THETA ** (-jnp.arange(0, dim, 2, dtype=F32) / dim)
    inv_lane = inv[idx][None, :]
    n_grid_rows = seq // GRID_W

    def expand(per_row, per_col):
        by_row = jnp.broadcast_to(per_row[:, None, :], (n_grid_rows, GRID_W, LANES)).reshape(seq, LANES)
        by_col = jnp.broadcast_to(per_col[None, :, :], (n_grid_rows, GRID_W, LANES)).reshape(seq, LANES)
        return jnp.where(use_col[None, :], by_col, by_row)

    ang_row = jnp.arange(n_grid_rows, dtype=jnp.int32).astype(F32)[:, None] * inv_lane
    ang_col = jnp.arange(GRID_W, dtype=jnp.int32).astype(F32)[:, None] * inv_lane
    act = active[None, :]
    cos = jnp.where(act, expand(jnp.cos(ang_row), jnp.cos(ang_col)), 1.0)
    sin = expand(jnp.sin(ang_row), jnp.sin(ang_col))
    sin_a = jnp.where(act & first[None, :], -sin, 0.0)
    sin_b = jnp.where(act & ~first[None, :], sin, 0.0)
    return jnp.stack([cos, sin_a, sin_b]).astype(F32)


def _dft_tables(seq):
    def trig(prod, period):
        ang = (prod % period).astype(F32) * (2.0 * np.pi / period)
        return jnp.cos(ang), jnp.sin(ang)

    half = seq // 2
    nblk = seq // DFT_BLOCK
    k = jnp.arange(half, dtype=jnp.int32)
    ca, sa = trig((jnp.arange(nblk, dtype=jnp.int32) * DFT_BLOCK)[:, None] * k[None, :], seq)
    cb, sb = trig(k[:, None] * jnp.arange(DFT_BLOCK, dtype=jnp.int32)[None, :], seq)
    ca, sa, cb, sb = ca[:, :, None], sa[:, :, None], cb[None], sb[None]
    cm = jnp.concatenate([ca * cb - sa * sb, -(sa * cb + ca * sb)], axis=0).astype(BF16)
    pos = jnp.arange(seq, dtype=jnp.int32)
    alt = jnp.where(jnp.arange(8, dtype=jnp.int32)[:, None] == 0,
                    (1 - 2 * (pos % 2)).astype(F32)[None, :], 0.0).astype(BF16)
    rev = ((k[:, None] + k[None, :] == half) & (k[:, None] >= 1)).astype(BF16)
    c = jnp.arange(BRANCH_W, dtype=jnp.int32)
    same = (c[:, None] // FNET_GDIM) == (c[None, :] // FNET_GDIM)
    cc, cs = trig((c[:, None] % FNET_GDIM) * (c[None, :] % FNET_GDIM), FNET_GDIM)
    bc = jnp.where(same, cc, 0.0).astype(BF16)
    bs = jnp.where(same, cs, 0.0).astype(BF16)
    return cm, alt, rev, bc, bs


def _pad_cols(a, width):
    return jnp.pad(a, ((0, 0), (0, width - a.shape[1])))


def _relayout_w_in(w_in):
    sp = np.cumsum([0, 192, 128, 32, 256, 256, 128, 128, 256, 256, 256, 256, 256, 256])
    cq, ckv, kr, ga, q2, k2, v2, gb, u, vc, gc, f, gd = [w_in[:, :, sp[i]:sp[i + 1]] for i in range(13)]

    def zeros(n):
        return jnp.zeros(w_in.shape[:2] + (n,), w_in.dtype)

    out = jnp.concatenate(
        [cq, zeros(256 - MLA_Q_RANK), ckv,
         zeros(MLA_NOPE_DIM), kr, zeros(LANES - MLA_NOPE_DIM - MLA_ROPE_DIM),
         k2, v2, gb, ga, q2, u, vc, gc, f, gd], axis=2).astype(BF16)
    assert out.shape[2] == C_END
    return out


def _layer_weights(l, norm_g, w_in_p, mla_q_norm, mla_w_uq, mla_kv_norm, mla_w_ukv, mla_qn, mla_kn,
                   gqa_qn, gqa_kn, cm_ln_g, cm_ln_b, cm_w_s, cm_b_s, fnet_w, w_out):
    w_in_p = w_in_p[l]

    seg = np.arange(MLA_ROPE_DIM) % 16
    partner = MLA_NOPE_DIM + (np.arange(MLA_ROPE_DIM) // 16) * 16 + (seg + 8) % 16
    perm = np.concatenate([np.arange(MLA_NOPE_DIM), partner])
    wuq = mla_w_uq[l].reshape(MLA_Q_RANK, MLA_HEADS, MLA_QK_DIM)
    wuq = jnp.concatenate([wuq, wuq[:, :, perm]], axis=1)
    wuq = jnp.pad(wuq, ((0, 256 - MLA_Q_RANK), (0, 0), (0, LANES - MLA_QK_DIM)))
    wuq = wuq.reshape(256, 2 * MLA_HEADS * LANES).astype(BF16)
    wukv = mla_w_ukv[l].reshape(MLA_KV_RANK, MLA_HEADS, MLA_NOPE_DIM + MLA_V_DIM)
    wk = jnp.pad(wukv[:, :, :MLA_NOPE_DIM], ((0, 0), (0, 0), (0, LANES - MLA_NOPE_DIM)))
    wv = wukv[:, :, MLA_NOPE_DIM:]
    wukv_p = jnp.concatenate(
        [wk.reshape(MLA_KV_RANK, MLA_HEADS * LANES), wv.reshape(MLA_KV_RANK, MLA_HEADS * MLA_V_DIM)],
        axis=1).astype(BF16)

    def row(v):
        return _pad_cols(v[None, :].astype(F32), 256)

    pvec = jnp.concatenate([
        row(mla_q_norm[l]), row(mla_kv_norm[l]), row(mla_qn[l]), row(mla_kn[l]),
        row(jnp.tile(gqa_qn[l], 2)), row(jnp.tile(gqa_kn[l], 2)), row(cm_ln_g[l]), row(cm_ln_b[l]),
        row(mla_qn[l][perm])] + [row(jnp.zeros((1,), F32))] * 7, axis=0)

    w_cm = jnp.transpose(cm_w_s[l], (1, 0, 2)).reshape(CHUNK, CM_GROUPS * CHUNK).astype(BF16)
    b_cm = jnp.broadcast_to(cm_b_s[l].T[:, :, None], (CHUNK, CM_GROUPS, BRANCH_W // CM_GROUPS))
    b_cm = b_cm.reshape(CHUNK, BRANCH_W).astype(F32)
    return dict(norm_g=norm_g[l][None, :], w_in=w_in_p, pvec=pvec, w_uq=wuq, w_ukv=wukv_p,
                w_cm=w_cm, b_cm=b_cm, fnet_w=fnet_w[l].astype(BF16), w_out=w_out[l].astype(BF16))


def kernel(x, c, ctx, c_ctx, norm_g, w_mod, b_mod, w_in, mla_q_norm, mla_w_uq, mla_kv_norm, mla_w_ukv,
           mla_qn, mla_kn, gqa_qn, gqa_kn, cm_ln_g, cm_ln_b, cm_w_s, cm_b_s, fnet_w, w_out):
    bsz, seq, _ = x.shape
    ctx_len = ctx.shape[1]
    depth = w_in.shape[0]
    w_in_p = _relayout_w_in(w_in)

    rows = -(-(bsz + 1) // 8) * 8
    cc = jnp.concatenate([c, c_ctx[None, :], jnp.zeros((rows - bsz - 1, D_MODEL), F32)], axis=0)
    mod_all = _modulation(cc, w_mod, b_mod)

    rope_a = _rope_table(seq, [(MLA_NOPE_DIM, 16, 0), (MLA_NOPE_DIM + 16, 16, 1)], 8)
    rope_b = _rope_table(seq, [(0, 32, 0), (32, 32, 1), (64, 32, 0), (96, 32, 1)], 16)
    dft_x = _dft_tables(seq)
    dft_c = _dft_tables(ctx_len)

    pending_x = pending_c = None
    ctx_items = PROJ_PARTS if PROJ_PARTS * ctx_len == PROJ_TILE and bsz % PROJ_PARTS == 0 else 1
    for l in range(depth):
        lw = _layer_weights(l, norm_g, w_in_p, mla_q_norm, mla_w_uq, mla_kv_norm, mla_w_ukv, mla_qn,
                            mla_kn, gqa_qn, gqa_kn, cm_ln_g, cm_ln_b, cm_w_s, cm_b_s, fnet_w, w_out)
        mod_x = mod_all[l, :bsz].reshape(bsz, 3, D_MODEL)
        mod_c = mod_all[l, bsz:bsz + 1].reshape(1, 3, D_MODEL)
        update_ctx = l < depth - 1

        outs = _projection(x, mod_x, lw, (rope_a, rope_b), kv_only=False,
                           merge_in=pending_x, emit_x=pending_x is not None)
        if pending_x is not None:
            x, outs = outs[0], outs[1:]
        (q1, k1, v1t, q2, k2, v2t, sga, sgb, yc, f, sgd) = outs
        outs_c = _projection(ctx, mod_c, lw, None, kv_only=not update_ctx, merge_in=pending_c,
                             emit_x=pending_c is not None and update_ctx, items=ctx_items)
        if pending_c is not None and update_ctx:
            ctx, outs_c = outs_c[0], outs_c[1:]
        if update_ctx:
            (q1c, k1c, v1tc, q2c, k2c, v2tc, sgac, sgbc, ycc, fc, sgdc) = outs_c
        else:
            k1c, v1tc, k2c, v2tc = outs_c

        ya = _attention(q1, sga, [(k1, v1t), (k1c, v1tc)], paired_q=False)
        yb = _attention(q2, sgb, [(k2, v2t), (k2c, v2tc)], paired_q=True)
        yd = _fourier(f, sgd, dft_x, lw)
        pending_x = (mod_x, ya, yb, yc, yd, lw["w_out"])
        pending_c = None
        if update_ctx:
            yac = _attention(q1c, sgac, [(k1c, v1tc)], paired_q=False)
            ybc = _attention(q2c, sgbc, [(k2c, v2tc)], paired_q=True)
            ydc = _fourier(fc, sgdc, dft_c, lw)
            pending_c = (mod_c, yac, ybc, ycc, ydc, lw["w_out"])
    return _merge(x, *pending_x)
```

```python
import functools

import numpy as np
import jax
import jax.numpy as jnp
from jax import lax
from jax.experimental import pallas as pl
from jax.experimental.pallas import tpu as pltpu

D_MODEL = 1024
GRID_W = 64
BRANCH_W = 256
MLA_HEADS = 4
MLA_NOPE_DIM = 64
MLA_ROPE_DIM = 32
MLA_QK_DIM = 96
MLA_V_DIM = 64
MLA_Q_RANK = 192
MLA_KV_RANK = 128
GQA_HEADS = 4
GQA_KV_HEADS = 2
GQA_HEAD_DIM = 64
CHUNK = 128
CM_GROUPS = 4
FNET_GROUPS = 4
FNET_GDIM = BRANCH_W // FNET_GROUPS
ROPE_THETA = 10000.0
EPS = 1e-6
LOG2_E = 1.4426950408889634

LANES = 128
V_ROWS = MLA_V_DIM + 16
PROJ_TILE = 512
PROJ_PARTS = 2
MERGE_TILE = 1024
DFT_BLOCK = 256
ATTN_TILE = 256
ATTN_ITEMS = 8
SCORE_ROWS = (256, 256)
EXP_ROWS = (256, 256)
VMEM_LIMIT = 48 * 1024 * 1024

F32 = jnp.float32
BF16 = jnp.bfloat16

C_CQ, C_CKV, C_KR, C_K2, C_V2, C_GB, C_GA, C_Q2, C_U, C_VC, C_GC, C_F, C_GD, C_END = (
    0, 256, 384, 512, 640, 768, 1024, 1280, 1536, 1792, 2048, 2304, 2560, 2816)


def _dot(a, b):
    return jnp.dot(a, b, preferred_element_type=F32)


def _dot_nt(a, b):
    return lax.dot_general(a, b, (((1,), (1,)), ((), ())), preferred_element_type=F32)


def _silu(x):
    return x * jax.nn.sigmoid(x)


def _lane_id(shape):
    return lax.broadcasted_iota(jnp.int32, shape, len(shape) - 1)


def _mod_kernel(c_ref, w_ref, b_ref, o_ref):
    c = c_ref[...]
    o_ref[...] = _dot(_silu(c).astype(BF16), w_ref[...].astype(BF16)) + b_ref[...]


def _modulation(cc, w_mod, b_mod):
    n_layers = w_mod.shape[0]
    rows = cc.shape[0]
    return pl.pallas_call(
        _mod_kernel,
        grid=(n_layers, 3),
        in_specs=[
            pl.BlockSpec((rows, D_MODEL), lambda l, j: (0, 0)),
            pl.BlockSpec((None, D_MODEL, D_MODEL), lambda l, j: (l, 0, j)),
            pl.BlockSpec((None, 1, D_MODEL), lambda l, j: (l, 0, j)),
        ],
        out_specs=pl.BlockSpec((None, rows, D_MODEL), lambda l, j: (l, 0, j)),
        out_shape=jax.ShapeDtypeStruct((n_layers, rows, 3 * D_MODEL), F32),
        compiler_params=pltpu.CompilerParams(vmem_limit_bytes=VMEM_LIMIT),
        name="modulation",
    )(cc, w_mod, b_mod.reshape(n_layers, 1, 3 * D_MODEL))


def _rope(y, tab_ref, rs, half):
    width = y.shape[1]
    reps = width // LANES
    cos, sin_a, sin_b = tab_ref[0, rs, :], tab_ref[1, rs, :], tab_ref[2, rs, :]
    if reps > 1:
        cos = jnp.concatenate([cos] * reps, axis=1)
        sin_a = jnp.concatenate([sin_a] * reps, axis=1)
        sin_b = jnp.concatenate([sin_b] * reps, axis=1)
    return (y * cos + pltpu.roll(y, width - half, 1) * sin_a + pltpu.roll(y, half, 1) * sin_b)


def _head_rms_64(x, gain):
    outs = []
    for g in range(x.shape[1] // LANES):
        xg = x[:, g * LANES:(g + 1) * LANES]
        lo = _lane_id(xg.shape) < GQA_HEAD_DIM
        x2 = xg * xg
        s_lo = jnp.sum(jnp.where(lo, x2, 0.0), axis=-1, keepdims=True)
        s_hi = jnp.sum(jnp.where(lo, 0.0, x2), axis=-1, keepdims=True)
        ms = jnp.where(lo, s_lo, s_hi) * (1.0 / GQA_HEAD_DIM)
        outs.append(xg * lax.rsqrt(ms + EPS) * gain)
    return jnp.concatenate(outs, axis=1) if len(outs) > 1 else outs[0]


def _proj_kernel(*refs, rope, kv_only, merge, emit_x, split_cols):
    it = iter(refs)
    x_ref, mod_ref, ng_ref, w_ref, pv_ref, wuq_ref, wukv_ref, wcm_ref, bcm_ref = (
        next(it) for _ in range(9))
    ra_ref = rb_ref = None
    if rope:
        ra_ref, rb_ref = next(it), next(it)
    if merge:
        pmod_ref, ya_ref, yb_ref, yc_ref, yd_ref, wout_ref = (next(it) for _ in range(6))
    x_o = next(it) if emit_x else None
    if kv_only:
        k1_o, v1t_o, k2_o, v2t_o = (next(it) for _ in range(4))
    else:
        (q1_o, k1_o, v1t_o, q2_o, k2_o, v2t_o, sga_o, sgb_o, yc_o, f_o, sgd_o) = (
            next(it) for _ in range(11))

    n_rows = x_ref.shape[0]
    part = max(n_rows // PROJ_PARTS, CHUNK)
    halves = [slice(r, r + part) for r in range(0, n_rows, part)]
    shift, scale = mod_ref[0:1, :], mod_ref[1:2, :]
    hb_parts = []
    for rs in halves:
        x = x_ref[rs, :]
        if merge:
            cat = jnp.concatenate([ya_ref[rs, :], yb_ref[rs, :], yc_ref[rs, :], yd_ref[rs, :]], axis=1)
            x = x + pmod_ref[2:3, :] * _dot(cat, wout_ref[...])
        if emit_x:
            x_o[rs, :] = x
        ms = jnp.mean(x * x, axis=-1, keepdims=True)
        h = (x * lax.rsqrt(ms + EPS) * ng_ref[...]) * (1.0 + scale) + shift
        hb_parts.append(h.astype(BF16))
    hb = jnp.concatenate(hb_parts, axis=0) if len(hb_parts) > 1 else hb_parts[0]

    def proj(lo, hi):
        return _dot(hb, w_ref[:, lo:hi])

    def put_cols(ref, height, rs, value):
        if split_cols:
            ref[rs.start // part, height, :] = value
        else:
            ref[height, rs] = value

    g_q, g_kv = pv_ref[0:1, :], pv_ref[1:2, 0:LANES]
    g_qn, g_kn = pv_ref[2:3, 0:LANES], pv_ref[3:4, 0:LANES]
    g_q2, g_k2 = pv_ref[4:5, 0:LANES], pv_ref[5:6, 0:LANES]
    ln_g, ln_b = pv_ref[6:7, :], pv_ref[7:8, :]
    g_qn_perm = pv_ref[8:9, 0:LANES]

    firsts = []
    for rs, hb_half in zip(halves, hb_parts):
        d0 = _dot(hb_half, w_ref[:, (C_CKV if kv_only else C_CQ):C_K2])
        cq = None if kv_only else d0[:, 0:2 * LANES]
        ckv, kr = d0[:, -2 * LANES:-LANES], d0[:, -LANES:]
        msk = jnp.mean(ckv * ckv, axis=-1, keepdims=True)
        ckvn = (ckv * lax.rsqrt(msk + EPS) * g_kv).astype(BF16)
        cqn = None
        if not kv_only:
            msq = jnp.sum(cq * cq, axis=-1, keepdims=True) * (1.0 / MLA_Q_RANK)
            cqn = (cq * lax.rsqrt(msq + EPS) * g_q).astype(BF16)
        firsts.append((ckvn, kr, cqn))
    for rs, (ckvn, kr, _) in zip(halves, firsts):
        kv = _dot(ckvn, wukv_ref[...])
        krg = kr * g_kn
        if rope:
            krg = _rope(krg, ra_ref, rs, MLA_ROPE_DIM // 4)
        kr_sq = jnp.sum(kr * kr, axis=-1, keepdims=True)
        for hh in range(MLA_HEADS):
            kn = kv[:, hh * LANES:(hh + 1) * LANES]
            msh = (jnp.sum(kn * kn, axis=-1, keepdims=True) + kr_sq) * (1.0 / MLA_QK_DIM)
            k1_o[rs, hh * LANES:(hh + 1) * LANES] = ((kn * g_kn + krg) * lax.rsqrt(msh + EPS)).astype(BF16)
        put_cols(v1t_o, slice(None), rs, jnp.transpose(kv[:, 4 * LANES:]).astype(BF16))

    d1 = proj(C_K2, C_GB if kv_only else C_GA)
    k2 = _head_rms_64(d1[:, 0:LANES], g_k2)
    if rope:
        k2 = _rope(k2, rb_ref, slice(None), GQA_HEAD_DIM // 4)
    lo = _lane_id(k2.shape) < GQA_HEAD_DIM
    k2r = pltpu.roll(k2, GQA_HEAD_DIM, 1)
    zero = jnp.zeros_like(k2)
    k2_o[:, 0 * LANES:1 * LANES] = jnp.where(lo, k2, zero).astype(BF16)
    k2_o[:, 1 * LANES:2 * LANES] = jnp.where(lo, zero, k2r).astype(BF16)
    k2_o[:, 2 * LANES:3 * LANES] = jnp.where(lo, k2r, zero).astype(BF16)
    k2_o[:, 3 * LANES:4 * LANES] = jnp.where(lo, zero, k2).astype(BF16)
    v2t = jnp.transpose(d1[:, LANES:2 * LANES]).astype(BF16)
    hd = GQA_HEAD_DIM
    for rs in halves:
        for dst, src in ((0, 0), (1, 0), (2, 1), (3, 1)):
            put_cols(v2t_o, slice(dst * hd, (dst + 1) * hd), rs, v2t[src * hd:(src + 1) * hd, rs])
    if kv_only:
        return
    sgb_o[...] = _silu(d1[:, 2 * LANES:]).astype(BF16)

    for rs, (_, _, cqn) in zip(halves, firsts):
        qq = _dot(cqn, wuq_ref[...])
        for hh in range(MLA_HEADS):
            qs = qq[:, hh * LANES:(hh + 1) * LANES]
            r = lax.rsqrt(jnp.sum(qs * qs, axis=-1, keepdims=True) * (1.0 / MLA_QK_DIM) + EPS)
            r = r * (LOG2_E * MLA_QK_DIM ** -0.5)
            y = qs * g_qn
            if rope:
                qp = qq[:, (MLA_HEADS + hh) * LANES:(MLA_HEADS + hh + 1) * LANES]
                y = y * ra_ref[0, rs, :] + (qp * g_qn_perm) * (ra_ref[1, rs, :] + ra_ref[2, rs, :])
            q1_o[rs, hh * LANES:(hh + 1) * LANES] = (y * r).astype(BF16)

    d2 = proj(C_GA, C_U)
    sga_o[...] = _silu(d2[:, 0:BRANCH_W]).astype(BF16)
    q2 = _head_rms_64(d2[:, BRANCH_W:], g_q2)
    if rope:
        q2 = _rope(q2, rb_ref, slice(None), GQA_HEAD_DIM // 4)
    q2_o[...] = (q2 * (LOG2_E * GQA_HEAD_DIM ** -0.5)).astype(BF16)

    d3 = proj(C_U, C_GC)
    d4 = proj(C_GC, C_GD)
    u, vc = d3[:, 0:BRANCH_W], d3[:, BRANCH_W:]
    sgc = _silu(d4[:, 0:BRANCH_W])
    f_o[...] = d4[:, BRANCH_W:].astype(BF16)
    sgd_o[...] = _silu(proj(C_GD, C_END)).astype(BF16)
    mu = jnp.mean(vc, axis=-1, keepdims=True)
    var = jnp.mean(jnp.square(vc - mu), axis=-1, keepdims=True)
    vn = ((vc - mu) * lax.rsqrt(var + EPS) * ln_g + ln_b).astype(BF16)
    grp = lax.shift_right_logical(_lane_id((CHUNK, BRANCH_W)), 6)
    for c in range(n_rows // CHUNK):
        cr = slice(c * CHUNK, (c + 1) * CHUNK)
        vnc = vn[cr, :]
        stacked = jnp.concatenate(
            [jnp.where(grp == g, vnc, jnp.zeros_like(vnc)) for g in range(CM_GROUPS)], axis=0)
        s = _dot(wcm_ref[...], stacked) + bcm_ref[...]
        yc_o[cr, :] = (u[cr, :] * s * sgc[cr, :]).astype(BF16)


def _projection(xs, mod, lw, rope_tabs, *, kv_only, merge_in=None, emit_x=False, items=1):
    true_bsz, true_seq, _ = xs.shape
    if items > 1:
        assert rope_tabs is None and mod.shape[0] == 1 and items == PROJ_PARTS
        assert true_bsz % items == 0 and items * true_seq == PROJ_TILE

        def fold(a):
            return a.reshape(true_bsz // items, items * true_seq, a.shape[-1])
        xs = fold(xs)
        if merge_in is not None:
            merge_in = (merge_in[0],) + tuple(fold(a) for a in merge_in[1:5]) + (merge_in[5],)
    bsz, seq, _ = xs.shape
    ts = min(PROJ_TILE, seq)
    rope = rope_tabs is not None

    def full(a):
        nd = a.ndim
        return pl.BlockSpec(a.shape, lambda b, i: (0,) * nd)

    def mod_spec(m):
        per_sample = m.shape[0] == bsz
        return pl.BlockSpec((None, 3, D_MODEL), (lambda b, i: (b, 0, 0)) if per_sample else (lambda b, i: (0, 0, 0)))

    in_specs = [
        pl.BlockSpec((None, ts, D_MODEL), lambda b, i: (b, i, 0)), mod_spec(mod),
        full(lw["norm_g"]), full(lw["w_in"]), full(lw["pvec"]), full(lw["w_uq"]), full(lw["w_ukv"]),
        full(lw["w_cm"]), full(lw["b_cm"]),
    ]
    args = [xs, mod, lw["norm_g"], lw["w_in"], lw["pvec"], lw["w_uq"], lw["w_ukv"], lw["w_cm"], lw["b_cm"]]
    if rope:
        for t in rope_tabs:
            in_specs.append(pl.BlockSpec((3, ts, LANES), lambda b, i: (0, i, 0)))
            args.append(t)
    if merge_in is not None:
        pmod, ya, yb, yc, yd, w_out = merge_in
        branch = pl.BlockSpec((None, ts, BRANCH_W), lambda b, i: (b, i, 0))
        in_specs += [mod_spec(pmod), branch, branch, branch, branch, full(w_out)]
        args += [pmod, ya, yb, yc, yd, w_out]

    def rows_out(width):
        return (pl.BlockSpec((None, ts, width), lambda b, i: (b, i, 0)),
                jax.ShapeDtypeStruct((bsz, seq, width), BF16))

    def cols_out(height):
        if items > 1:
            return (pl.BlockSpec((items, height, true_seq), lambda b, i: (b, 0, 0)),
                    jax.ShapeDtypeStruct((true_bsz, height, true_seq), BF16))
        return (pl.BlockSpec((None, height, ts), lambda b, i: (b, 0, i)),
                jax.ShapeDtypeStruct((bsz, height, seq), BF16))

    if kv_only:
        outs = [rows_out(512), cols_out(256), rows_out(512), cols_out(256)]
    else:
        outs = [rows_out(512), rows_out(512), cols_out(256), rows_out(256), rows_out(512), cols_out(256),
                rows_out(256), rows_out(256), rows_out(256), rows_out(256), rows_out(256)]
    if emit_x:
        outs.insert(0, (pl.BlockSpec((None, ts, D_MODEL), lambda b, i: (b, i, 0)),
                        jax.ShapeDtypeStruct(xs.shape, F32)))
    results = pl.pallas_call(
        functools.partial(_proj_kernel, rope=rope, kv_only=kv_only, merge=merge_in is not None,
                          emit_x=emit_x, split_cols=items > 1),
        grid=(bsz, seq // ts),
        in_specs=in_specs,
        out_specs=[o[0] for o in outs],
        out_shape=[o[1] for o in outs],
        compiler_params=pltpu.CompilerParams(
            dimension_semantics=("arbitrary", "arbitrary"), vmem_limit_bytes=VMEM_LIMIT),
        name="projection",
    )(*args)
    if items > 1:
        results = [r.reshape(true_bsz, true_seq, r.shape[-1]) if r.shape[0] != true_bsz else r
                   for r in results]
    return results


def _interleave(major, minor):
    out, j = [], 0
    for i, th in enumerate(major):
        out.append(th)
        want = ((i + 1) * len(minor)) // len(major)
        out.extend(minor[j:want])
        j = want
    return out


def _attn_kernel(*refs, n_src, tq, n_qt, nb):
    carry = n_qt > 1
    it = iter(refs)
    q_refs = (next(it), next(it))
    g_ref = next(it)
    srcs = [(next(it), next(it), next(it)) for _ in range(n_src)]
    nxt = None
    if carry:
        nxt = ((next(it), next(it)), [(next(it), next(it)) for _ in range(n_src)])
    o_ref = next(it)
    vt_s = (next(it), next(it))
    s_bufs = ((next(it), next(it)), (next(it), next(it)))
    m_s = (next(it), next(it)) if carry else None
    if nb == 1:
        _attn_body(q_refs, g_ref, srcs, nxt, o_ref, vt_s, s_bufs, m_s, tq=tq, n_qt=n_qt)
        return
    for i in range(nb):
        def at(r):
            return r.at[i]
        _attn_body(tuple(map(at, q_refs)), at(g_ref), [tuple(map(at, s)) for s in srcs], None,
                   at(o_ref), tuple(map(at, vt_s)), tuple(tuple(map(at, sl)) for sl in s_bufs), None,
                   tq=tq, n_qt=n_qt)


def _attn_body(q_refs, g_ref, srcs, nxt, o_ref, vt_s, s_bufs, m_s, *, tq, n_qt):
    carry = n_qt > 1
    if carry:
        qn_refs, ksrcs_next = nxt
    ksrcs = [(ke, ko) for ke, ko, _ in srcs]

    off = 0
    for _, _, vt in srcs:
        n = vt.shape[1]
        vt_s[0][0:MLA_V_DIM, off:off + n] = vt[0:MLA_V_DIM, :]
        vt_s[1][0:MLA_V_DIM, off:off + n] = vt[MLA_V_DIM:2 * MLA_V_DIM, :]
        off += n
    n_keys = off
    ones = jnp.ones((V_ROWS - MLA_V_DIM, n_keys), BF16)
    vt_s[0][MLA_V_DIM:V_ROWS, :] = ones
    vt_s[1][MLA_V_DIM:V_ROWS, :] = ones

    def rows(t):
        if isinstance(t, int):
            return slice(t * tq, (t + 1) * tq)
        return pl.ds(pl.multiple_of(t * tq, tq), tq)

    def score_thunks(q_of, keys, slot, m_out):
        per_head = ([], [])
        for h in range(2):
            off = 0
            for pair in keys:
                n = pair[h].shape[0]
                for lo in range(0, n, SCORE_ROWS[h]):
                    hi = min(lo + SCORE_ROWS[h], n)

                    def th(h=h, k_ref=pair[h], lo=lo, hi=hi, off=off):
                        s = _dot_nt(k_ref[lo:hi, :], q_of(h))
                        s_bufs[slot][h][off + lo:off + hi, :] = s
                        mx = jnp.max(s, axis=0, keepdims=True)
                        m_out[h] = mx if m_out[h] is None else jnp.maximum(m_out[h], mx)
                    per_head[h].append(th)
                off += n
        return _interleave(per_head[0], per_head[1])

    def own_scores(t, slot, m_out):
        return score_thunks(lambda h: q_refs[h][rows(t), :], ksrcs, slot, m_out)

    def value_thunks(t, slot, m_in):
        acc = [None, None]
        per_head = ([], [])
        for h in range(2):
            for c in range(0, n_keys, EXP_ROWS[h]):
                def th(h=h, c=c, e=min(c + EXP_ROWS[h], n_keys)):
                    p = jnp.exp2(s_bufs[slot][h][c:e, :] - m_in[h]).astype(BF16)
                    part = _dot(vt_s[h][:, c:e], p)
                    acc[h] = part if acc[h] is None else acc[h] + part
                per_head[h].append(th)
        thunks = _interleave(per_head[0], per_head[1])

        def fin():
            outs = [a[0:MLA_V_DIM, :] / a[MLA_V_DIM:MLA_V_DIM + 1, :] for a in acc]
            att = jnp.transpose(jnp.concatenate(outs, axis=0))
            o_ref[rows(t), :] = (att * g_ref[rows(t), :]).astype(BF16)
        thunks.append(fin)
        return thunks

    def run(thunks):
        for th in thunks:
            th()

    def stage(scores, t_val, slot_val, m_val):
        m_new = [None, None]
        run(_interleave(value_thunks(t_val, slot_val, m_val), scores(m_new)))
        return m_new

    if not carry:
        m0 = [None, None]
        run(own_scores(0, 0, m0))
        run(value_thunks(0, 0, m0))
        return

    @pl.when((pl.program_id(0) == 0) & (pl.program_id(1) == 0))
    def _():
        m_first = [None, None]
        run(own_scores(0, 0, m_first))
        for h in range(2):
            m_s[h][...] = m_first[h]

    def body(u, m_in):
        m_a = stage(lambda m: own_scores(2 * u + 1, 1, m), 2 * u, 0, list(m_in))
        m_b = stage(lambda m: own_scores(2 * u + 2, 0, m), 2 * u + 1, 1, m_a)
        return tuple(m_b)
    m_even = lax.fori_loop(0, n_qt // 2 - 1, body, (m_s[0][...], m_s[1][...]))
    m_last = stage(lambda m: own_scores(n_qt - 1, 1, m), n_qt - 2, 0, list(m_even))
    m_next = stage(lambda m: score_thunks(lambda h: qn_refs[h][...], ksrcs_next, 0, m), n_qt - 1, 1, m_last)
    for h in range(2):
        m_s[h][...] = m_next[h]


def _attention(q, gate, sources, *, paired_q):
    bsz, sq, _ = q.shape
    tq = min(ATTN_TILE, sq)
    n_qt = sq // tq
    assert n_qt == 1 or n_qt % 2 == 0

    def q_cols(j):
        return (j, j) if paired_q else (2 * j, 2 * j + 1)

    def following(b, j):
        g = jnp.minimum(2 * b + j + 1, 2 * bsz - 1)
        return g // 2, g % 2

    nb = 1 if n_qt > 1 else ATTN_ITEMS
    assert bsz % nb == 0
    lead = None if nb == 1 else nb

    def scr(shape, dtype):
        return pltpu.VMEM(shape if nb == 1 else (nb,) + shape, dtype)

    in_specs = [pl.BlockSpec((lead, sq, LANES), lambda b, j, h=h: (b, 0, q_cols(j)[h])) for h in range(2)]
    in_specs.append(pl.BlockSpec((lead, sq, LANES), lambda b, j: (b, 0, j)))
    args = [q, q, gate]
    total = 0
    for k, vt in sources:
        n = k.shape[1]
        total += n
        in_specs += [
            pl.BlockSpec((lead, n, LANES), lambda b, j: (b, 0, 2 * j)),
            pl.BlockSpec((lead, n, LANES), lambda b, j: (b, 0, 2 * j + 1)),
            pl.BlockSpec((lead, LANES, n), lambda b, j: (b, j, 0)),
        ]
        args += [k, k, vt]
    scratch = [scr((V_ROWS, total), BF16), scr((V_ROWS, total), BF16)]
    scratch += [scr((total, tq), F32) for _ in range(4)]
    if n_qt > 1:
        def next_q(b, j, h):
            bn, jn = following(b, j)
            return bn, 0, q_cols(jn)[h]

        def next_k(b, j, h):
            bn, jn = following(b, j)
            return bn, 0, 2 * jn + h

        in_specs += [pl.BlockSpec((None, tq, LANES), functools.partial(next_q, h=h)) for h in range(2)]
        args += [q, q]
        for k, _ in sources:
            in_specs += [pl.BlockSpec((None, k.shape[1], LANES), functools.partial(next_k, h=h))
                         for h in range(2)]
            args += [k, k]
        scratch += [pltpu.VMEM((1, tq), F32), pltpu.VMEM((1, tq), F32)]
    return pl.pallas_call(
        functools.partial(_attn_kernel, n_src=len(sources), tq=tq, n_qt=n_qt, nb=nb),
        grid=(bsz // nb, 2),
        in_specs=in_specs,
        out_specs=pl.BlockSpec((lead, sq, LANES), lambda b, j: (b, 0, j)),
        out_shape=jax.ShapeDtypeStruct((bsz, sq, BRANCH_W), BF16),
        scratch_shapes=scratch,
        compiler_params=pltpu.CompilerParams(
            dimension_semantics=("arbitrary", "arbitrary"), vmem_limit_bytes=VMEM_LIMIT),
        name="attention",
    )(*args)


def _fourier_kernel(f_ref, g_ref, cm_ref, alt_ref, rev_ref, bc_ref, bs_ref, wf_ref, o_ref, *, seq):
    half = seq // 2
    nblk = seq // DFT_BLOCK
    f = f_ref[...]
    g_cos = _dot(f, bc_ref[...]).astype(BF16)
    g_sin = _dot(f, bs_ref[...]).astype(BF16)
    y_cos = y_sin = None
    for a in range(nblk):
        rs = slice(a * DFT_BLOCK, (a + 1) * DFT_BLOCK)
        pc = _dot(cm_ref[a], g_cos[rs, :])
        ps = _dot(cm_ref[nblk + a], g_sin[rs, :])
        y_cos = pc if y_cos is None else y_cos + pc
        y_sin = ps if y_sin is None else y_sin + ps
    norm = (seq * FNET_GDIM) ** -0.5
    low = ((y_cos + y_sin) * norm).astype(BF16)
    diff = ((y_cos - y_sin) * norm).astype(BF16)
    y_mid = _dot(alt_ref[...], g_cos)[0:1, :] * norm
    high = _dot(rev_ref[...], diff)
    first_row = lax.broadcasted_iota(jnp.int32, high.shape, 0) == 0
    high = jnp.where(first_row, y_mid, high).astype(BF16)
    wf = wf_ref[...]
    o_ref[0:half, :] = (_dot(low, wf) * g_ref[0:half, :]).astype(BF16)
    o_ref[half:seq, :] = (_dot(high, wf) * g_ref[half:seq, :]).astype(BF16)


def _fourier(f, gate, dft, lw):
    bsz, seq, _ = f.shape

    def const(a):
        return pl.BlockSpec(a.shape, lambda b: (0,) * a.ndim, pipeline_mode=pl.Buffered(1))

    consts = list(dft) + [lw["fnet_w"]]
    return pl.pallas_call(
        functools.partial(_fourier_kernel, seq=seq),
        grid=(bsz,),
        in_specs=[
            pl.BlockSpec((None, seq, BRANCH_W), lambda b: (b, 0, 0)),
            pl.BlockSpec((None, seq, BRANCH_W), lambda b: (b, 0, 0)),
        ] + [const(a) for a in consts],
        out_specs=pl.BlockSpec((None, seq, BRANCH_W), lambda b: (b, 0, 0)),
        out_shape=jax.ShapeDtypeStruct((bsz, seq, BRANCH_W), BF16),
        compiler_params=pltpu.CompilerParams(
            dimension_semantics=("arbitrary",), vmem_limit_bytes=VMEM_LIMIT),
        name="fourier",
    )(f, gate, *consts)


def _merge_kernel(x_ref, mod_ref, ya_ref, yb_ref, yc_ref, yd_ref, w_ref, o_ref):
    cat = jnp.concatenate([ya_ref[...], yb_ref[...], yc_ref[...], yd_ref[...]], axis=1)
    o_ref[...] = x_ref[...] + mod_ref[2:3, :] * _dot(cat, w_ref[...])


def _merge(xs, mod, ya, yb, yc, yd, w_out):
    bsz, seq, _ = xs.shape
    ts = min(MERGE_TILE, seq)
    per_sample = mod.shape[0] == bsz
    branch = pl.BlockSpec((None, ts, BRANCH_W), lambda b, i: (b, i, 0))
    return pl.pallas_call(
        _merge_kernel,
        grid=(bsz, seq // ts),
        in_specs=[
            pl.BlockSpec((None, ts, D_MODEL), lambda b, i: (b, i, 0)),
            pl.BlockSpec((None, 3, D_MODEL), (lambda b, i: (b, 0, 0)) if per_sample else (lambda b, i: (0, 0, 0))),
            branch, branch, branch, branch,
            pl.BlockSpec(w_out.shape, lambda b, i: (0, 0)),
        ],
        out_specs=pl.BlockSpec((None, ts, D_MODEL), lambda b, i: (b, i, 0)),
        out_shape=jax.ShapeDtypeStruct(xs.shape, F32),
        compiler_params=pltpu.CompilerParams(
            dimension_semantics=("arbitrary", "arbitrary"), vmem_limit_bytes=VMEM_LIMIT),
        name="merge",
    )(xs, mod, ya, yb, yc, yd, w_out)


def _rope_table(seq, segments, half):
    idx = np.zeros(LANES, np.int32)
    active = np.zeros(LANES, bool)
    use_col = np.zeros(LANES, bool)
    first = np.zeros(LANES, bool)
    dim = segments[0][1]
    for start, d, kind in segments:
        assert d == dim and d // 2 == half
        for t in range(d):
            idx[start + t] = t % (d // 2)
            active[start + t] = True
            use_col[start + t] = kind == 1
            first[start + t] = t < d // 2
    inv = ROPE_THETA ** (-jnp.arange(0, dim, 2, dtype=F32) / dim)
    inv_lane = inv[idx][None, :]
    n_grid_rows = seq // GRID_W

    def expand(per_row, per_col):
        by_row = jnp.broadcast_to(per_row[:, None, :], (n_grid_rows, GRID_W, LANES)).reshape(seq, LANES)
        by_col = jnp.broadcast_to(per_col[None, :, :], (n_grid_rows, GRID_W, LANES)).reshape(seq, LANES)
        return jnp.where(use_col[None, :], by_col, by_row)

    ang_row = jnp.arange(n_grid_rows, dtype=jnp.int32).astype(F32)[:, None] * inv_lane
    ang_col = jnp.arange(GRID_W, dtype=jnp.int32).astype(F32)[:, None] * inv_lane
    act = active[None, :]
    cos = jnp.where(act, expand(jnp.cos(ang_row), jnp.cos(ang_col)), 1.0)
    sin = expand(jnp.sin(ang_row), jnp.sin(ang_col))
    sin_a = jnp.where(act & first[None, :], -sin, 0.0)
    sin_b = jnp.where(act & ~first[None, :], sin, 0.0)
    return jnp.stack([cos, sin_a, sin_b]).astype(F32)


def _dft_tables(seq):
    def trig(prod, period):
        ang = (prod % period).astype(F32) * (2.0 * np.pi / period)
        return jnp.cos(ang), jnp.sin(ang)

    half = seq // 2
    nblk = seq // DFT_BLOCK
    k = jnp.arange(half, dtype=jnp.int32)
    ca, sa = trig((jnp.arange(nblk, dtype=jnp.int32) * DFT_BLOCK)[:, None] * k[None, :], seq)
    cb, sb = trig(k[:, None] * jnp.arange(DFT_BLOCK, dtype=jnp.int32)[None, :], seq)
    ca, sa, cb, sb = ca[:, :, None], sa[:, :, None], cb[None], sb[None]
    cm = jnp.concatenate([ca * cb - sa * sb, -(sa * cb + ca * sb)], axis=0).astype(BF16)
    pos = jnp.arange(seq, dtype=jnp.int32)
    alt = jnp.where(jnp.arange(8, dtype=jnp.int32)[:, None] == 0,
                    (1 - 2 * (pos % 2)).astype(F32)[None, :], 0.0).astype(BF16)
    rev = ((k[:, None] + k[None, :] == half) & (k[:, None] >= 1)).astype(BF16)
    c = jnp.arange(BRANCH_W, dtype=jnp.int32)
    same = (c[:, None] // FNET_GDIM) == (c[None, :] // FNET_GDIM)
    cc, cs = trig((c[:, None] % FNET_GDIM) * (c[None, :] % FNET_GDIM), FNET_GDIM)
    bc = jnp.where(same, cc, 0.0).astype(BF16)
    bs = jnp.where(same, cs, 0.0).astype(BF16)
    return cm, alt, rev, bc, bs


def _pad_cols(a, width):
    return jnp.pad(a, ((0, 0), (0, width - a.shape[1])))


W_IN_NATURAL = (192, 128, 32, 256, 256, 128, 128, 256, 256, 256, 256, 256, 256)
W_IN_PLACED = (C_CQ, C_CKV, C_KR + MLA_NOPE_DIM, C_GA, C_Q2, C_K2, C_V2, C_GB, C_U, C_VC, C_GC, C_F, C_GD)
RELAYOUT_ROWS = 256


def _relayout_kernel(w_ref, o_ref):
    o_ref[...] = jnp.zeros(o_ref.shape, o_ref.dtype)
    src = 0
    for width, dst in zip(W_IN_NATURAL, W_IN_PLACED):
        o_ref[:, dst:dst + width] = w_ref[:, src:src + width].astype(o_ref.dtype)
        src += width


def _relayout_w_in(w_in):
    n_layers, rows, cols = w_in.shape
    assert cols == sum(W_IN_NATURAL)
    return pl.pallas_call(
        _relayout_kernel,
        grid=(n_layers, rows // RELAYOUT_ROWS),
        in_specs=[pl.BlockSpec((None, RELAYOUT_ROWS, cols), lambda l, i: (l, i, 0))],
        out_specs=pl.BlockSpec((None, RELAYOUT_ROWS, C_END), lambda l, i: (l, i, 0)),
        out_shape=jax.ShapeDtypeStruct((n_layers, rows, C_END), BF16),
        compiler_params=pltpu.CompilerParams(vmem_limit_bytes=VMEM_LIMIT),
        name="relayout",
    )(w_in)


def _layer_weights(l, norm_g, w_in_p, mla_q_norm, mla_w_uq, mla_kv_norm, mla_w_ukv, mla_qn, mla_kn,
                   gqa_qn, gqa_kn, cm_ln_g, cm_ln_b, cm_w_s, cm_b_s, fnet_w, w_out):
    w_in_p = w_in_p[l]

    seg = np.arange(MLA_ROPE_DIM) % 16
    partner = MLA_NOPE_DIM + (np.arange(MLA_ROPE_DIM) // 16) * 16 + (seg + 8) % 16
    perm = np.concatenate([np.arange(MLA_NOPE_DIM), partner])
    wuq = mla_w_uq[l].reshape(MLA_Q_RANK, MLA_HEADS, MLA_QK_DIM)
    wuq = jnp.concatenate([wuq, wuq[:, :, perm]], axis=1)
    wuq = jnp.pad(wuq, ((0, 256 - MLA_Q_RANK), (0, 0), (0, LANES - MLA_QK_DIM)))
    wuq = wuq.reshape(256, 2 * MLA_HEADS * LANES).astype(BF16)
    wukv = mla_w_ukv[l].reshape(MLA_KV_RANK, MLA_HEADS, MLA_NOPE_DIM + MLA_V_DIM)
    wk = jnp.pad(wukv[:, :, :MLA_NOPE_DIM], ((0, 0), (0, 0), (0, LANES - MLA_NOPE_DIM)))
    wv = wukv[:, :, MLA_NOPE_DIM:]
    wukv_p = jnp.concatenate(
        [wk.reshape(MLA_KV_RANK, MLA_HEADS * LANES), wv.reshape(MLA_KV_RANK, MLA_HEADS * MLA_V_DIM)],
        axis=1).astype(BF16)

    def row(v):
        return _pad_cols(v[None, :].astype(F32), 256)

    pvec = jnp.concatenate([
        row(mla_q_norm[l]), row(mla_kv_norm[l]), row(mla_qn[l]), row(mla_kn[l]),
        row(jnp.tile(gqa_qn[l], 2)), row(jnp.tile(gqa_kn[l], 2)), row(cm_ln_g[l]), row(cm_ln_b[l]),
        row(mla_qn[l][perm])] + [row(jnp.zeros((1,), F32))] * 7, axis=0)

    w_cm = jnp.transpose(cm_w_s[l], (1, 0, 2)).reshape(CHUNK, CM_GROUPS * CHUNK).astype(BF16)
    b_cm = jnp.broadcast_to(cm_b_s[l].T[:, :, None], (CHUNK, CM_GROUPS, BRANCH_W // CM_GROUPS))
    b_cm = b_cm.reshape(CHUNK, BRANCH_W).astype(F32)
    return dict(norm_g=norm_g[l][None, :], w_in=w_in_p, pvec=pvec, w_uq=wuq, w_ukv=wukv_p,
                w_cm=w_cm, b_cm=b_cm, fnet_w=fnet_w[l].astype(BF16), w_out=w_out[l].astype(BF16))


def kernel(x, c, ctx, c_ctx, norm_g, w_mod, b_mod, w_in, mla_q_norm, mla_w_uq, mla_kv_norm, mla_w_ukv,
           mla_qn, mla_kn, gqa_qn, gqa_kn, cm_ln_g, cm_ln_b, cm_w_s, cm_b_s, fnet_w, w_out):
    bsz, seq, _ = x.shape
    ctx_len = ctx.shape[1]
    depth = w_in.shape[0]
    w_in_p = _relayout_w_in(w_in)

    rows = -(-(bsz + 1) // 8) * 8
    cc = jnp.concatenate([c, c_ctx[None, :], jnp.zeros((rows - bsz - 1, D_MODEL), F32)], axis=0)
    mod_all = _modulation(cc, w_mod, b_mod)

    rope_a = _rope_table(seq, [(MLA_NOPE_DIM, 16, 0), (MLA_NOPE_DIM + 16, 16, 1)], 8)
    rope_b = _rope_table(seq, [(0, 32, 0), (32, 32, 1), (64, 32, 0), (96, 32, 1)], 16)
    dft_x = _dft_tables(seq)
    dft_c = _dft_tables(ctx_len)

    pending_x = pending_c = None
    ctx_items = PROJ_PARTS if PROJ_PARTS * ctx_len == PROJ_TILE and bsz % PROJ_PARTS == 0 else 1
    for l in range(depth):
        lw = _layer_weights(l, norm_g, w_in_p, mla_q_norm, mla_w_uq, mla_kv_norm, mla_w_ukv, mla_qn,
                            mla_kn, gqa_qn, gqa_kn, cm_ln_g, cm_ln_b, cm_w_s, cm_b_s, fnet_w, w_out)
        mod_x = mod_all[l, :bsz].reshape(bsz, 3, D_MODEL)
        mod_c = mod_all[l, bsz:bsz + 1].reshape(1, 3, D_MODEL)
        update_ctx = l < depth - 1

        outs = _projection(x, mod_x, lw, (rope_a, rope_b), kv_only=False,
                           merge_in=pending_x, emit_x=pending_x is not None)
        if pending_x is not None:
            x, outs = outs[0], outs[1:]
        (q1, k1, v1t, q2, k2, v2t, sga, sgb, yc, f, sgd) = outs
        outs_c = _projection(ctx, mod_c, lw, None, kv_only=not update_ctx, merge_in=pending_c,
                             emit_x=pending_c is not None and update_ctx, items=ctx_items)
        if pending_c is not None and update_ctx:
            ctx, outs_c = outs_c[0], outs_c[1:]
        if update_ctx:
            (q1c, k1c, v1tc, q2c, k2c, v2tc, sgac, sgbc, ycc, fc, sgdc) = outs_c
        else:
            k1c, v1tc, k2c, v2tc = outs_c

        ya = _attention(q1, sga, [(k1, v1t), (k1c, v1tc)], paired_q=False)
        yb = _attention(q2, sgb, [(k2, v2t), (k2c, v2tc)], paired_q=True)
        yd = _fourier(f, sgd, dft_x, lw)
        pending_x = (mod_x, ya, yb, yc, yd, lw["w_out"])
        pending_c = None
        if update_ctx:
            yac = _attention(q1c, sgac, [(k1c, v1tc)], paired_q=False)
            ybc = _attention(q2c, sgbc, [(k2c, v2tc)], paired_q=True)
            ydc = _fourier(fc, sgdc, dft_c, lw)
            pending_c = (mod_c, yac, ybc, ycc, ydc, lw["w_out"])
    return _merge(x, *pending_x)
```

```python
import functools

import numpy as np
import jax
import jax.numpy as jnp
from jax import lax
from jax.experimental import pallas as pl
from jax.experimental.pallas import tpu as pltpu

D_MODEL = 1024
GRID_W = 64
BRANCH_W = 256
MLA_HEADS = 4
MLA_NOPE_DIM = 64
MLA_ROPE_DIM = 32
MLA_QK_DIM = 96
MLA_V_DIM = 64
MLA_Q_RANK = 192
MLA_KV_RANK = 128
GQA_HEAD_DIM = 64
CHUNK = 128
CM_GROUPS = 4
FNET_GROUPS = 4
FNET_GDIM = BRANCH_W // FNET_GROUPS
ROPE_THETA = 10000.0
EPS = 1e-6
LOG2_E = 1.4426950408889634

LANES = 128
V_ROWS = MLA_V_DIM + 16
PROJ_TILE = 512
PROJ_PARTS = 2
MERGE_TILE = 1024
DFT_BLOCK = 256
FOURIER_ROWS = 2048
ATTN_TILE = 256
ATTN_ITEMS = 8
SCORE_ROWS = 256
EXP_ROWS = 256
VMEM_LIMIT = 48 * 1024 * 1024

F32 = jnp.float32
BF16 = jnp.bfloat16

C_CQ, C_CKV, C_KR, C_K2, C_V2, C_GB, C_GA, C_Q2, C_U, C_VC, C_GC, C_F, C_GD, C_END = (
    0, 256, 384, 512, 640, 768, 1024, 1280, 1536, 1792, 2048, 2304, 2560, 2816)


def _dot(a, b):
    return jnp.dot(a, b, preferred_element_type=F32)


def _dot_nt(a, b):
    return lax.dot_general(a, b, (((1,), (1,)), ((), ())), preferred_element_type=F32)


def _silu(x):
    return x * jax.nn.sigmoid(x)


def _lane_id(shape):
    return lax.broadcasted_iota(jnp.int32, shape, len(shape) - 1)


def _mod_kernel(c_ref, w_ref, b_ref, o_ref):
    c = c_ref[...]
    o_ref[...] = _dot(_silu(c).astype(BF16), w_ref[...].astype(BF16)) + b_ref[...]


def _modulation(cc, w_mod, b_mod):
    n_layers = w_mod.shape[0]
    rows = cc.shape[0]
    return pl.pallas_call(
        _mod_kernel,
        grid=(n_layers, 3),
        in_specs=[
            pl.BlockSpec((rows, D_MODEL), lambda l, j: (0, 0)),
            pl.BlockSpec((None, D_MODEL, D_MODEL), lambda l, j: (l, 0, j)),
            pl.BlockSpec((None, 1, D_MODEL), lambda l, j: (l, 0, j)),
        ],
        out_specs=pl.BlockSpec((None, rows, D_MODEL), lambda l, j: (l, 0, j)),
        out_shape=jax.ShapeDtypeStruct((n_layers, rows, 3 * D_MODEL), F32),
        compiler_params=pltpu.CompilerParams(vmem_limit_bytes=VMEM_LIMIT),
        name="modulation",
    )(cc, w_mod, b_mod.reshape(n_layers, 1, 3 * D_MODEL))


def _rope(y, tab_ref, rs, half):
    width = y.shape[1]
    reps = width // LANES
    cos, sin_a, sin_b = tab_ref[0, rs, :], tab_ref[1, rs, :], tab_ref[2, rs, :]
    if reps > 1:
        cos = jnp.concatenate([cos] * reps, axis=1)
        sin_a = jnp.concatenate([sin_a] * reps, axis=1)
        sin_b = jnp.concatenate([sin_b] * reps, axis=1)
    return (y * cos + pltpu.roll(y, width - half, 1) * sin_a + pltpu.roll(y, half, 1) * sin_b)


def _head_rms_64(x, gain):
    outs = []
    for g in range(x.shape[1] // LANES):
        xg = x[:, g * LANES:(g + 1) * LANES]
        lo = _lane_id(xg.shape) < GQA_HEAD_DIM
        x2 = xg * xg
        s_lo = jnp.sum(jnp.where(lo, x2, 0.0), axis=-1, keepdims=True)
        s_hi = jnp.sum(jnp.where(lo, 0.0, x2), axis=-1, keepdims=True)
        ms = jnp.where(lo, s_lo, s_hi) * (1.0 / GQA_HEAD_DIM)
        outs.append(xg * lax.rsqrt(ms + EPS) * gain)
    return jnp.concatenate(outs, axis=1) if len(outs) > 1 else outs[0]


def _proj_kernel(*refs, rope, kv_only, merge, emit_x, split_cols):
    it = iter(refs)
    x_ref, mod_ref, ng_ref, w_ref, pv_ref, wuq_ref, wukv_ref, wcm_ref, bcm_ref = (
        next(it) for _ in range(9))
    ra_ref = rb_ref = None
    if rope:
        ra_ref, rb_ref = next(it), next(it)
    if merge:
        pmod_ref, ya_ref, yb_ref, yc_ref, yd_ref, wout_ref = (next(it) for _ in range(6))
    x_o = next(it) if emit_x else None
    if kv_only:
        k1_o, v1t_o, k2_o, v2t_o = (next(it) for _ in range(4))
    else:
        (q1_o, k1_o, v1t_o, q2_o, k2_o, v2t_o, sga_o, sgb_o, yc_o, f_o, sgd_o) = (
            next(it) for _ in range(11))

    n_rows = x_ref.shape[0]
    part = max(n_rows // PROJ_PARTS, CHUNK)
    halves = [slice(r, r + part) for r in range(0, n_rows, part)]
    shift, scale = mod_ref[0:1, :], mod_ref[1:2, :]
    hb_parts = []
    for rs in halves:
        x = x_ref[rs, :]
        if merge:
            cat = jnp.concatenate([ya_ref[rs, :], yb_ref[rs, :], yc_ref[rs, :], yd_ref[rs, :]], axis=1)
            x = x + pmod_ref[2:3, :] * _dot(cat, wout_ref[...])
        if emit_x:
            x_o[rs, :] = x
        ms = jnp.mean(x * x, axis=-1, keepdims=True)
        h = (x * lax.rsqrt(ms + EPS) * ng_ref[...]) * (1.0 + scale) + shift
        hb_parts.append(h.astype(BF16))
    hb = jnp.concatenate(hb_parts, axis=0) if len(hb_parts) > 1 else hb_parts[0]

    def proj(lo, hi):
        return _dot(hb, w_ref[:, lo:hi])

    def put_cols(ref, height, rs, value):
        if split_cols:
            ref[rs.start // part, height, :] = value
        else:
            ref[height, rs] = value

    g_q, g_kv = pv_ref[0:1, :], pv_ref[1:2, 0:LANES]
    g_qn, g_kn = pv_ref[2:3, 0:LANES], pv_ref[3:4, 0:LANES]
    g_q2, g_k2 = pv_ref[4:5, 0:LANES], pv_ref[5:6, 0:LANES]
    ln_g, ln_b = pv_ref[6:7, :], pv_ref[7:8, :]
    g_qn_perm = pv_ref[8:9, 0:LANES]

    firsts = []
    for rs, hb_half in zip(halves, hb_parts):
        d0 = _dot(hb_half, w_ref[:, (C_CKV if kv_only else C_CQ):C_K2])
        cq = None if kv_only else d0[:, 0:2 * LANES]
        ckv, kr = d0[:, -2 * LANES:-LANES], d0[:, -LANES:]
        msk = jnp.mean(ckv * ckv, axis=-1, keepdims=True)
        ckvn = (ckv * lax.rsqrt(msk + EPS) * g_kv).astype(BF16)
        cqn = None
        if not kv_only:
            msq = jnp.sum(cq * cq, axis=-1, keepdims=True) * (1.0 / MLA_Q_RANK)
            cqn = (cq * lax.rsqrt(msq + EPS) * g_q).astype(BF16)
        firsts.append((ckvn, kr, cqn))
    for rs, (ckvn, kr, _) in zip(halves, firsts):
        kv = _dot(ckvn, wukv_ref[...])
        krg = kr * g_kn
        if rope:
            krg = _rope(krg, ra_ref, rs, MLA_ROPE_DIM // 4)
        kr_sq = jnp.sum(kr * kr, axis=-1, keepdims=True)
        for hh in range(MLA_HEADS):
            kn = kv[:, hh * LANES:(hh + 1) * LANES]
            msh = (jnp.sum(kn * kn, axis=-1, keepdims=True) + kr_sq) * (1.0 / MLA_QK_DIM)
            k1_o[rs, hh * LANES:(hh + 1) * LANES] = ((kn * g_kn + krg) * lax.rsqrt(msh + EPS)).astype(BF16)
        put_cols(v1t_o, slice(None), rs, jnp.transpose(kv[:, 4 * LANES:]).astype(BF16))

    d1 = proj(C_K2, C_GB if kv_only else C_GA)
    k2 = _head_rms_64(d1[:, 0:LANES], g_k2)
    if rope:
        k2 = _rope(k2, rb_ref, slice(None), GQA_HEAD_DIM // 4)
    lo = _lane_id(k2.shape) < GQA_HEAD_DIM
    k2r = pltpu.roll(k2, GQA_HEAD_DIM, 1)
    zero = jnp.zeros_like(k2)
    k2_o[:, 0 * LANES:1 * LANES] = jnp.where(lo, k2, zero).astype(BF16)
    k2_o[:, 1 * LANES:2 * LANES] = jnp.where(lo, zero, k2r).astype(BF16)
    k2_o[:, 2 * LANES:3 * LANES] = jnp.where(lo, k2r, zero).astype(BF16)
    k2_o[:, 3 * LANES:4 * LANES] = jnp.where(lo, zero, k2).astype(BF16)
    v2t = jnp.transpose(d1[:, LANES:2 * LANES]).astype(BF16)
    hd = GQA_HEAD_DIM
    for rs in halves:
        for dst, src in ((0, 0), (1, 0), (2, 1), (3, 1)):
            put_cols(v2t_o, slice(dst * hd, (dst + 1) * hd), rs, v2t[src * hd:(src + 1) * hd, rs])
    if kv_only:
        return
    sgb_o[...] = _silu(d1[:, 2 * LANES:]).astype(BF16)

    for rs, (_, _, cqn) in zip(halves, firsts):
        qq = _dot(cqn, wuq_ref[...])
        for hh in range(MLA_HEADS):
            qs = qq[:, hh * LANES:(hh + 1) * LANES]
            r = lax.rsqrt(jnp.sum(qs * qs, axis=-1, keepdims=True) * (1.0 / MLA_QK_DIM) + EPS)
            r = r * (LOG2_E * MLA_QK_DIM ** -0.5)
            y = qs * g_qn
            if rope:
                qp = qq[:, (MLA_HEADS + hh) * LANES:(MLA_HEADS + hh + 1) * LANES]
                y = y * ra_ref[0, rs, :] + (qp * g_qn_perm) * (ra_ref[1, rs, :] + ra_ref[2, rs, :])
            q1_o[rs, hh * LANES:(hh + 1) * LANES] = (y * r).astype(BF16)

    d2 = proj(C_GA, C_U)
    sga_o[...] = _silu(d2[:, 0:BRANCH_W]).astype(BF16)
    q2 = _head_rms_64(d2[:, BRANCH_W:], g_q2)
    if rope:
        q2 = _rope(q2, rb_ref, slice(None), GQA_HEAD_DIM // 4)
    q2_o[...] = (q2 * (LOG2_E * GQA_HEAD_DIM ** -0.5)).astype(BF16)

    d3 = proj(C_U, C_GC)
    d4 = proj(C_GC, C_GD)
    u, vc = d3[:, 0:BRANCH_W], d3[:, BRANCH_W:]
    sgc = _silu(d4[:, 0:BRANCH_W])
    f_o[...] = d4[:, BRANCH_W:].astype(BF16)
    sgd_o[...] = _silu(proj(C_GD, C_END)).astype(BF16)
    mu = jnp.mean(vc, axis=-1, keepdims=True)
    var = jnp.mean(jnp.square(vc - mu), axis=-1, keepdims=True)
    vn = ((vc - mu) * lax.rsqrt(var + EPS) * ln_g + ln_b).astype(BF16)
    grp = lax.shift_right_logical(_lane_id((CHUNK, BRANCH_W)), 6)
    for c in range(n_rows // CHUNK):
        cr = slice(c * CHUNK, (c + 1) * CHUNK)
        vnc = vn[cr, :]
        stacked = jnp.concatenate(
            [jnp.where(grp == g, vnc, jnp.zeros_like(vnc)) for g in range(CM_GROUPS)], axis=0)
        s = _dot(wcm_ref[...], stacked) + bcm_ref[...]
        yc_o[cr, :] = (u[cr, :] * s * sgc[cr, :]).astype(BF16)


def _projection(xs, mod, lw, rope_tabs, *, kv_only, merge_in=None, emit_x=False, items=1):
    true_bsz, true_seq, _ = xs.shape
    if items > 1:
        assert rope_tabs is None and mod.shape[0] == 1 and items == PROJ_PARTS
        assert true_bsz % items == 0 and items * true_seq == PROJ_TILE

        def fold(a):
            return a.reshape(true_bsz // items, items * true_seq, a.shape[-1])
        xs = fold(xs)
        if merge_in is not None:
            merge_in = (merge_in[0],) + tuple(fold(a) for a in merge_in[1:5]) + (merge_in[5],)
    bsz, seq, _ = xs.shape
    ts = min(PROJ_TILE, seq)
    rope = rope_tabs is not None

    def full(a):
        nd = a.ndim
        return pl.BlockSpec(a.shape, lambda b, i: (0,) * nd)

    def mod_spec(m):
        per_sample = m.shape[0] == bsz
        return pl.BlockSpec((None, 3, D_MODEL), (lambda b, i: (b, 0, 0)) if per_sample else (lambda b, i: (0, 0, 0)))

    in_specs = [
        pl.BlockSpec((None, ts, D_MODEL), lambda b, i: (b, i, 0)), mod_spec(mod),
        full(lw["norm_g"]), full(lw["w_in"]), full(lw["pvec"]), full(lw["w_uq"]), full(lw["w_ukv"]),
        full(lw["w_cm"]), full(lw["b_cm"]),
    ]
    args = [xs, mod, lw["norm_g"], lw["w_in"], lw["pvec"], lw["w_uq"], lw["w_ukv"], lw["w_cm"], lw["b_cm"]]
    if rope:
        for t in rope_tabs:
            in_specs.append(pl.BlockSpec((3, ts, LANES), lambda b, i: (0, i, 0)))
            args.append(t)
    if merge_in is not None:
        pmod, ya, yb, yc, yd, w_out = merge_in
        branch = pl.BlockSpec((None, ts, BRANCH_W), lambda b, i: (b, i, 0))
        in_specs += [mod_spec(pmod), branch, branch, branch, branch, full(w_out)]
        args += [pmod, ya, yb, yc, yd, w_out]

    def rows_out(width):
        return (pl.BlockSpec((None, ts, width), lambda b, i: (b, i, 0)),
                jax.ShapeDtypeStruct((bsz, seq, width), BF16))

    def cols_out(height):
        if items > 1:
            return (pl.BlockSpec((items, height, true_seq), lambda b, i: (b, 0, 0)),
                    jax.ShapeDtypeStruct((true_bsz, height, true_seq), BF16))
        return (pl.BlockSpec((None, height, ts), lambda b, i: (b, 0, i)),
                jax.ShapeDtypeStruct((bsz, height, seq), BF16))

    if kv_only:
        outs = [rows_out(512), cols_out(256), rows_out(512), cols_out(256)]
    else:
        outs = [rows_out(512), rows_out(512), cols_out(256), rows_out(256), rows_out(512), cols_out(256),
                rows_out(256), rows_out(256), rows_out(256), rows_out(256), rows_out(256)]
    if emit_x:
        outs.insert(0, (pl.BlockSpec((None, ts, D_MODEL), lambda b, i: (b, i, 0)),
                        jax.ShapeDtypeStruct(xs.shape, F32)))
    results = pl.pallas_call(
        functools.partial(_proj_kernel, rope=rope, kv_only=kv_only, merge=merge_in is not None,
                          emit_x=emit_x, split_cols=items > 1),
        grid=(bsz, seq // ts),
        in_specs=in_specs,
        out_specs=[o[0] for o in outs],
        out_shape=[o[1] for o in outs],
        compiler_params=pltpu.CompilerParams(
            dimension_semantics=("arbitrary", "arbitrary"), vmem_limit_bytes=VMEM_LIMIT),
        name="projection",
    )(*args)
    if items > 1:
        results = [r.reshape(true_bsz, true_seq, r.shape[-1]) if r.shape[0] != true_bsz else r
                   for r in results]
    return results


def _interleave(major, minor):
    out, j = [], 0
    for i, th in enumerate(major):
        out.append(th)
        want = ((i + 1) * len(minor)) // len(major)
        out.extend(minor[j:want])
        j = want
    return out


def _attn_kernel(*refs, n_src, tq, n_qt, nb):
    carry = n_qt > 1
    it = iter(refs)
    q_refs = (next(it), next(it))
    g_ref = next(it)
    srcs = [(next(it), next(it), next(it)) for _ in range(n_src)]
    nxt = None
    if carry:
        nxt = ((next(it), next(it)), [(next(it), next(it)) for _ in range(n_src)])
    o_ref = next(it)
    vt_s = (next(it), next(it))
    s_bufs = ((next(it), next(it)), (next(it), next(it)))
    m_s = (next(it), next(it)) if carry else None
    if nb == 1:
        _attn_body(q_refs, g_ref, srcs, nxt, o_ref, vt_s, s_bufs, m_s, tq=tq, n_qt=n_qt)
        return
    for i in range(nb):
        def at(r):
            return r.at[i]
        _attn_body(tuple(map(at, q_refs)), at(g_ref), [tuple(map(at, s)) for s in srcs], None,
                   at(o_ref), tuple(map(at, vt_s)), tuple(tuple(map(at, sl)) for sl in s_bufs), None,
                   tq=tq, n_qt=n_qt)


def _attn_body(q_refs, g_ref, srcs, nxt, o_ref, vt_s, s_bufs, m_s, *, tq, n_qt):
    carry = n_qt > 1
    if carry:
        qn_refs, ksrcs_next = nxt
    ksrcs = [(ke, ko) for ke, ko, _ in srcs]

    off = 0
    for _, _, vt in srcs:
        n = vt.shape[1]
        vt_s[0][0:MLA_V_DIM, off:off + n] = vt[0:MLA_V_DIM, :]
        vt_s[1][0:MLA_V_DIM, off:off + n] = vt[MLA_V_DIM:2 * MLA_V_DIM, :]
        off += n
    n_keys = off
    ones = jnp.ones((V_ROWS - MLA_V_DIM, n_keys), BF16)
    vt_s[0][MLA_V_DIM:V_ROWS, :] = ones
    vt_s[1][MLA_V_DIM:V_ROWS, :] = ones

    def rows(t):
        if isinstance(t, int):
            return slice(t * tq, (t + 1) * tq)
        return pl.ds(pl.multiple_of(t * tq, tq), tq)

    def score_thunks(q_of, keys, slot, m_out):
        per_head = ([], [])
        for h in range(2):
            off = 0
            for pair in keys:
                n = pair[h].shape[0]
                for lo in range(0, n, SCORE_ROWS):
                    hi = min(lo + SCORE_ROWS, n)

                    def th(h=h, k_ref=pair[h], lo=lo, hi=hi, off=off):
                        s = _dot_nt(k_ref[lo:hi, :], q_of(h))
                        s_bufs[slot][h][off + lo:off + hi, :] = s
                        mx = jnp.max(s, axis=0, keepdims=True)
                        m_out[h] = mx if m_out[h] is None else jnp.maximum(m_out[h], mx)
                    per_head[h].append(th)
                off += n
        return _interleave(per_head[0], per_head[1])

    def own_scores(t, slot, m_out):
        return score_thunks(lambda h: q_refs[h][rows(t), :], ksrcs, slot, m_out)

    def value_thunks(t, slot, m_in):
        acc = [None, None]
        per_head = ([], [])
        for h in range(2):
            for c in range(0, n_keys, EXP_ROWS):
                def th(h=h, c=c, e=min(c + EXP_ROWS, n_keys)):
                    p = jnp.exp2(s_bufs[slot][h][c:e, :] - m_in[h]).astype(BF16)
                    part = _dot(vt_s[h][:, c:e], p)
                    acc[h] = part if acc[h] is None else acc[h] + part
                per_head[h].append(th)
        thunks = _interleave(per_head[0], per_head[1])

        def fin():
            outs = [a[0:MLA_V_DIM, :] / a[MLA_V_DIM:MLA_V_DIM + 1, :] for a in acc]
            att = jnp.transpose(jnp.concatenate(outs, axis=0))
            o_ref[rows(t), :] = (att * g_ref[rows(t), :]).astype(BF16)
        thunks.append(fin)
        return thunks

    def run(thunks):
        for th in thunks:
            th()

    def stage(scores, t_val, slot_val, m_val):
        m_new = [None, None]
        run(_interleave(value_thunks(t_val, slot_val, m_val), scores(m_new)))
        return m_new

    if not carry:
        m0 = [None, None]
        run(own_scores(0, 0, m0))
        run(value_thunks(0, 0, m0))
        return

    @pl.when((pl.program_id(0) == 0) & (pl.program_id(1) == 0))
    def _():
        m_first = [None, None]
        run(own_scores(0, 0, m_first))
        for h in range(2):
            m_s[h][...] = m_first[h]

    def body(u, m_in):
        m_a = stage(lambda m: own_scores(2 * u + 1, 1, m), 2 * u, 0, list(m_in))
        m_b = stage(lambda m: own_scores(2 * u + 2, 0, m), 2 * u + 1, 1, m_a)
        return tuple(m_b)
    m_even = lax.fori_loop(0, n_qt // 2 - 1, body, (m_s[0][...], m_s[1][...]))
    m_last = stage(lambda m: own_scores(n_qt - 1, 1, m), n_qt - 2, 0, list(m_even))
    m_next = stage(lambda m: score_thunks(lambda h: qn_refs[h][...], ksrcs_next, 0, m), n_qt - 1, 1, m_last)
    for h in range(2):
        m_s[h][...] = m_next[h]


def _attention(q, gate, sources, *, paired_q):
    bsz, sq, _ = q.shape
    tq = min(ATTN_TILE, sq)
    n_qt = sq // tq
    assert n_qt == 1 or n_qt % 2 == 0

    def q_cols(j):
        return (j, j) if paired_q else (2 * j, 2 * j + 1)

    def following(b, j):
        g = jnp.minimum(2 * b + j + 1, 2 * bsz - 1)
        return g // 2, g % 2

    nb = 1 if n_qt > 1 else ATTN_ITEMS
    assert bsz % nb == 0
    lead = None if nb == 1 else nb

    def scr(shape, dtype):
        return pltpu.VMEM(shape if nb == 1 else (nb,) + shape, dtype)

    in_specs = [pl.BlockSpec((lead, sq, LANES), lambda b, j, h=h: (b, 0, q_cols(j)[h])) for h in range(2)]
    in_specs.append(pl.BlockSpec((lead, sq, LANES), lambda b, j: (b, 0, j)))
    args = [q, q, gate]
    total = 0
    for k, vt in sources:
        n = k.shape[1]
        total += n
        in_specs += [
            pl.BlockSpec((lead, n, LANES), lambda b, j: (b, 0, 2 * j)),
            pl.BlockSpec((lead, n, LANES), lambda b, j: (b, 0, 2 * j + 1)),
            pl.BlockSpec((lead, LANES, n), lambda b, j: (b, j, 0)),
        ]
        args += [k, k, vt]
    scratch = [scr((V_ROWS, total), BF16), scr((V_ROWS, total), BF16)]
    scratch += [scr((total, tq), F32) for _ in range(4)]
    if n_qt > 1:
        def next_q(b, j, h):
            bn, jn = following(b, j)
            return bn, 0, q_cols(jn)[h]

        def next_k(b, j, h):
            bn, jn = following(b, j)
            return bn, 0, 2 * jn + h

        in_specs += [pl.BlockSpec((None, tq, LANES), functools.partial(next_q, h=h)) for h in range(2)]
        args += [q, q]
        for k, _ in sources:
            in_specs += [pl.BlockSpec((None, k.shape[1], LANES), functools.partial(next_k, h=h))
                         for h in range(2)]
            args += [k, k]
        scratch += [pltpu.VMEM((1, tq), F32), pltpu.VMEM((1, tq), F32)]
    return pl.pallas_call(
        functools.partial(_attn_kernel, n_src=len(sources), tq=tq, n_qt=n_qt, nb=nb),
        grid=(bsz // nb, 2),
        in_specs=in_specs,
        out_specs=pl.BlockSpec((lead, sq, LANES), lambda b, j: (b, 0, j)),
        out_shape=jax.ShapeDtypeStruct((bsz, sq, BRANCH_W), BF16),
        scratch_shapes=scratch,
        compiler_params=pltpu.CompilerParams(
            dimension_semantics=("arbitrary", "arbitrary"), vmem_limit_bytes=VMEM_LIMIT),
        name="attention",
    )(*args)


def _fourier_kernel(f_ref, g_ref, cm_ref, alt_ref, rev_ref, bc_ref, bs_ref, wf_ref, o_ref, *, seq, nb):
    for i in range(nb):
        _fourier_item(f_ref.at[i], g_ref.at[i], cm_ref, alt_ref, rev_ref, bc_ref, bs_ref, wf_ref,
                      o_ref.at[i], seq=seq)


def _fourier_item(f_ref, g_ref, cm_ref, alt_ref, rev_ref, bc_ref, bs_ref, wf_ref, o_ref, *, seq):
    half = seq // 2
    nblk = seq // DFT_BLOCK
    f = f_ref[...]
    g_cos = _dot(f, bc_ref[...]).astype(BF16)
    g_sin = _dot(f, bs_ref[...]).astype(BF16)
    y_cos = y_sin = None
    for a in range(nblk):
        rs = slice(a * DFT_BLOCK, (a + 1) * DFT_BLOCK)
        pc = _dot(cm_ref[a], g_cos[rs, :])
        ps = _dot(cm_ref[nblk + a], g_sin[rs, :])
        y_cos = pc if y_cos is None else y_cos + pc
        y_sin = ps if y_sin is None else y_sin + ps
    norm = (seq * FNET_GDIM) ** -0.5
    low = ((y_cos + y_sin) * norm).astype(BF16)
    diff = ((y_cos - y_sin) * norm).astype(BF16)
    y_mid = _dot(alt_ref[...], g_cos)[0:1, :] * norm
    high = _dot(rev_ref[...], diff)
    first_row = lax.broadcasted_iota(jnp.int32, high.shape, 0) == 0
    high = jnp.where(first_row, y_mid, high).astype(BF16)
    wf = wf_ref[...]
    o_ref[0:half, :] = (_dot(low, wf) * g_ref[0:half, :]).astype(BF16)
    o_ref[half:seq, :] = (_dot(high, wf) * g_ref[half:seq, :]).astype(BF16)


def _fourier(f, gate, dft, lw):
    bsz, seq, _ = f.shape

    def const(a):
        return pl.BlockSpec(a.shape, lambda b: (0,) * a.ndim, pipeline_mode=pl.Buffered(1))

    consts = list(dft) + [lw["fnet_w"]]
    nb = max(1, min(FOURIER_ROWS // seq, bsz))
    assert bsz % nb == 0
    return pl.pallas_call(
        functools.partial(_fourier_kernel, seq=seq, nb=nb),
        grid=(bsz // nb,),
        in_specs=[
            pl.BlockSpec((nb, seq, BRANCH_W), lambda b: (b, 0, 0)),
            pl.BlockSpec((nb, seq, BRANCH_W), lambda b: (b, 0, 0)),
        ] + [const(a) for a in consts],
        out_specs=pl.BlockSpec((nb, seq, BRANCH_W), lambda b: (b, 0, 0)),
        out_shape=jax.ShapeDtypeStruct((bsz, seq, BRANCH_W), BF16),
        compiler_params=pltpu.CompilerParams(
            dimension_semantics=("arbitrary",), vmem_limit_bytes=VMEM_LIMIT),
        name="fourier",
    )(f, gate, *consts)


def _merge_kernel(x_ref, mod_ref, ya_ref, yb_ref, yc_ref, yd_ref, w_ref, o_ref):
    cat = jnp.concatenate([ya_ref[...], yb_ref[...], yc_ref[...], yd_ref[...]], axis=1)
    o_ref[...] = x_ref[...] + mod_ref[2:3, :] * _dot(cat, w_ref[...])


def _merge(xs, mod, ya, yb, yc, yd, w_out):
    bsz, seq, _ = xs.shape
    ts = min(MERGE_TILE, seq)
    per_sample = mod.shape[0] == bsz
    branch = pl.BlockSpec((None, ts, BRANCH_W), lambda b, i: (b, i, 0))
    return pl.pallas_call(
        _merge_kernel,
        grid=(bsz, seq // ts),
        in_specs=[
            pl.BlockSpec((None, ts, D_MODEL), lambda b, i: (b, i, 0)),
            pl.BlockSpec((None, 3, D_MODEL), (lambda b, i: (b, 0, 0)) if per_sample else (lambda b, i: (0, 0, 0))),
            branch, branch, branch, branch,
            pl.BlockSpec(w_out.shape, lambda b, i: (0, 0)),
        ],
        out_specs=pl.BlockSpec((None, ts, D_MODEL), lambda b, i: (b, i, 0)),
        out_shape=jax.ShapeDtypeStruct(xs.shape, F32),
        compiler_params=pltpu.CompilerParams(
            dimension_semantics=("arbitrary", "arbitrary"), vmem_limit_bytes=VMEM_LIMIT),
        name="merge",
    )(xs, mod, ya, yb, yc, yd, w_out)


def _rope_table(seq, segments, half):
    idx = np.zeros(LANES, np.int32)
    active = np.zeros(LANES, bool)
    use_col = np.zeros(LANES, bool)
    first = np.zeros(LANES, bool)
    dim = segments[0][1]
    for start, d, kind in segments:
        assert d == dim and d // 2 == half
        for t in range(d):
            idx[start + t] = t % (d // 2)
            active[start + t] = True
            use_col[start + t] = kind == 1
            first[start + t] = t < d // 2
    inv = ROPE_THETA ** (-jnp.arange(0, dim, 2, dtype=F32) / dim)
    inv_lane = inv[idx][None, :]
    n_grid_rows = seq // GRID_W

    def expand(per_row, per_col):
        by_row = jnp.broadcast_to(per_row[:, None, :], (n_grid_rows, GRID_W, LANES)).reshape(seq, LANES)
        by_col = jnp.broadcast_to(per_col[None, :, :], (n_grid_rows, GRID_W, LANES)).reshape(seq, LANES)
        return jnp.where(use_col[None, :], by_col, by_row)

    ang_row = jnp.arange(n_grid_rows, dtype=jnp.int32).astype(F32)[:, None] * inv_lane
    ang_col = jnp.arange(GRID_W, dtype=jnp.int32).astype(F32)[:, None] * inv_lane
    act = active[None, :]
    cos = jnp.where(act, expand(jnp.cos(ang_row), jnp.cos(ang_col)), 1.0)
    sin = expand(jnp.sin(ang_row), jnp.sin(ang_col))
    sin_a = jnp.where(act & first[None, :], -sin, 0.0)
    sin_b = jnp.where(act & ~first[None, :], sin, 0.0)
    return jnp.stack([cos, sin_a, sin_b]).astype(F32)


def _dft_tables(seq):
    def trig(prod, period):
        ang = (prod % period).astype(F32) * (2.0 * np.pi / period)
        return jnp.cos(ang), jnp.sin(ang)

    half = seq // 2
    nblk = seq // DFT_BLOCK
    k = jnp.arange(half, dtype=jnp.int32)
    ca, sa = trig((jnp.arange(nblk, dtype=jnp.int32) * DFT_BLOCK)[:, None] * k[None, :], seq)
    cb, sb = trig(k[:, None] * jnp.arange(DFT_BLOCK, dtype=jnp.int32)[None, :], seq)
    ca, sa, cb, sb = ca[:, :, None], sa[:, :, None], cb[None], sb[None]
    cm = jnp.concatenate([ca * cb - sa * sb, -(sa * cb + ca * sb)], axis=0).astype(BF16)
    pos = jnp.arange(seq, dtype=jnp.int32)
    alt = jnp.where(jnp.arange(8, dtype=jnp.int32)[:, None] == 0,
                    (1 - 2 * (pos % 2)).astype(F32)[None, :], 0.0).astype(BF16)
    rev = ((k[:, None] + k[None, :] == half) & (k[:, None] >= 1)).astype(BF16)
    c = jnp.arange(BRANCH_W, dtype=jnp.int32)
    same = (c[:, None] // FNET_GDIM) == (c[None, :] // FNET_GDIM)
    cc, cs = trig((c[:, None] % FNET_GDIM) * (c[None, :] % FNET_GDIM), FNET_GDIM)
    bc = jnp.where(same, cc, 0.0).astype(BF16)
    bs = jnp.where(same, cs, 0.0).astype(BF16)
    return cm, alt, rev, bc, bs


def _pad_cols(a, width):
    return jnp.pad(a, ((0, 0), (0, width - a.shape[1])))


W_IN_NATURAL = (192, 128, 32, 256, 256, 128, 128, 256, 256, 256, 256, 256, 256)
W_IN_PLACED = (C_CQ, C_CKV, C_KR + MLA_NOPE_DIM, C_GA, C_Q2, C_K2, C_V2, C_GB, C_U, C_VC, C_GC, C_F, C_GD)
RELAYOUT_ROWS = 256


def _relayout_kernel(w_ref, o_ref):
    o_ref[...] = jnp.zeros(o_ref.shape, o_ref.dtype)
    src = 0
    for width, dst in zip(W_IN_NATURAL, W_IN_PLACED):
        o_ref[:, dst:dst + width] = w_ref[:, src:src + width].astype(o_ref.dtype)
        src += width


def _relayout_w_in(w_in):
    n_layers, rows, cols = w_in.shape
    assert cols == sum(W_IN_NATURAL)
    return pl.pallas_call(
        _relayout_kernel,
        grid=(n_layers, rows // RELAYOUT_ROWS),
        in_specs=[pl.BlockSpec((None, RELAYOUT_ROWS, cols), lambda l, i: (l, i, 0))],
        out_specs=pl.BlockSpec((None, RELAYOUT_ROWS, C_END), lambda l, i: (l, i, 0)),
        out_shape=jax.ShapeDtypeStruct((n_layers, rows, C_END), BF16),
        compiler_params=pltpu.CompilerParams(vmem_limit_bytes=VMEM_LIMIT),
        name="relayout",
    )(w_in)


def _layer_weights(l, norm_g, w_in_p, mla_q_norm, mla_w_uq, mla_kv_norm, mla_w_ukv, mla_qn, mla_kn,
                   gqa_qn, gqa_kn, cm_ln_g, cm_ln_b, cm_w_s, cm_b_s, fnet_w, w_out):
    w_in_p = w_in_p[l]

    seg = np.arange(MLA_ROPE_DIM) % 16
    partner = MLA_NOPE_DIM + (np.arange(MLA_ROPE_DIM) // 16) * 16 + (seg + 8) % 16
    perm = np.concatenate([np.arange(MLA_NOPE_DIM), partner])
    wuq = mla_w_uq[l].reshape(MLA_Q_RANK, MLA_HEADS, MLA_QK_DIM)
    wuq = jnp.concatenate([wuq, wuq[:, :, perm]], axis=1)
    wuq = jnp.pad(wuq, ((0, 256 - MLA_Q_RANK), (0, 0), (0, LANES - MLA_QK_DIM)))
    wuq = wuq.reshape(256, 2 * MLA_HEADS * LANES).astype(BF16)
    wukv = mla_w_ukv[l].reshape(MLA_KV_RANK, MLA_HEADS, MLA_NOPE_DIM + MLA_V_DIM)
    wk = jnp.pad(wukv[:, :, :MLA_NOPE_DIM], ((0, 0), (0, 0), (0, LANES - MLA_NOPE_DIM)))
    wv = wukv[:, :, MLA_NOPE_DIM:]
    wukv_p = jnp.concatenate(
        [wk.reshape(MLA_KV_RANK, MLA_HEADS * LANES), wv.reshape(MLA_KV_RANK, MLA_HEADS * MLA_V_DIM)],
        axis=1).astype(BF16)

    def row(v):
        return _pad_cols(v[None, :].astype(F32), 256)

    pvec = jnp.concatenate([
        row(mla_q_norm[l]), row(mla_kv_norm[l]), row(mla_qn[l]), row(mla_kn[l]),
        row(jnp.tile(gqa_qn[l], 2)), row(jnp.tile(gqa_kn[l], 2)), row(cm_ln_g[l]), row(cm_ln_b[l]),
        row(mla_qn[l][perm])] + [row(jnp.zeros((1,), F32))] * 7, axis=0)

    w_cm = jnp.transpose(cm_w_s[l], (1, 0, 2)).reshape(CHUNK, CM_GROUPS * CHUNK).astype(BF16)
    b_cm = jnp.broadcast_to(cm_b_s[l].T[:, :, None], (CHUNK, CM_GROUPS, BRANCH_W // CM_GROUPS))
    b_cm = b_cm.reshape(CHUNK, BRANCH_W).astype(F32)
    return dict(norm_g=norm_g[l][None, :], w_in=w_in_p, pvec=pvec, w_uq=wuq, w_ukv=wukv_p,
                w_cm=w_cm, b_cm=b_cm, fnet_w=fnet_w[l].astype(BF16), w_out=w_out[l].astype(BF16))


def kernel(x, c, ctx, c_ctx, norm_g, w_mod, b_mod, w_in, mla_q_norm, mla_w_uq, mla_kv_norm, mla_w_ukv,
           mla_qn, mla_kn, gqa_qn, gqa_kn, cm_ln_g, cm_ln_b, cm_w_s, cm_b_s, fnet_w, w_out):
    bsz, seq, _ = x.shape
    ctx_len = ctx.shape[1]
    depth = w_in.shape[0]
    w_in_p = _relayout_w_in(w_in)

    rows = -(-(bsz + 1) // 8) * 8
    cc = jnp.concatenate([c, c_ctx[None, :], jnp.zeros((rows - bsz - 1, D_MODEL), F32)], axis=0)
    mod_all = _modulation(cc, w_mod, b_mod)

    rope_a = _rope_table(seq, [(MLA_NOPE_DIM, 16, 0), (MLA_NOPE_DIM + 16, 16, 1)], 8)
    rope_b = _rope_table(seq, [(0, 32, 0), (32, 32, 1), (64, 32, 0), (96, 32, 1)], 16)
    dft_x = _dft_tables(seq)
    dft_c = _dft_tables(ctx_len)

    pending_x = pending_c = None
    ctx_items = PROJ_PARTS if PROJ_PARTS * ctx_len == PROJ_TILE and bsz % PROJ_PARTS == 0 else 1
    for l in range(depth):
        lw = _layer_weights(l, norm_g, w_in_p, mla_q_norm, mla_w_uq, mla_kv_norm, mla_w_ukv, mla_qn,
                            mla_kn, gqa_qn, gqa_kn, cm_ln_g, cm_ln_b, cm_w_s, cm_b_s, fnet_w, w_out)
        mod_x = mod_all[l, :bsz].reshape(bsz, 3, D_MODEL)
        mod_c = mod_all[l, bsz:bsz + 1].reshape(1, 3, D_MODEL)
        update_ctx = l < depth - 1

        outs = _projection(x, mod_x, lw, (rope_a, rope_b), kv_only=False,
                           merge_in=pending_x, emit_x=pending_x is not None)
        if pending_x is not None:
            x, outs = outs[0], outs[1:]
        (q1, k1, v1t, q2, k2, v2t, sga, sgb, yc, f, sgd) = outs
        outs_c = _projection(ctx, mod_c, lw, None, kv_only=not update_ctx, merge_in=pending_c,
                             emit_x=pending_c is not None and update_ctx, items=ctx_items)
        if pending_c is not None and update_ctx:
            ctx, outs_c = outs_c[0], outs_c[1:]
        if update_ctx:
            (q1c, k1c, v1tc, q2c, k2c, v2tc, sgac, sgbc, ycc, fc, sgdc) = outs_c
        else:
            k1c, v1tc, k2c, v2tc = outs_c

        ya = _attention(q1, sga, [(k1, v1t), (k1c, v1tc)], paired_q=False)
        yb = _attention(q2, sgb, [(k2, v2t), (k2c, v2tc)], paired_q=True)
        yd = _fourier(f, sgd, dft_x, lw)
        pending_x = (mod_x, ya, yb, yc, yd, lw["w_out"])
        pending_c = None
        if update_ctx:
            yac = _attention(q1c, sgac, [(k1c, v1tc)], paired_q=False)
            ybc = _attention(q2c, sgbc, [(k2c, v2tc)], paired_q=True)
            ydc = _fourier(fc, sgdc, dft_c, lw)
            pending_c = (mod_c, yac, ybc, ycc, ydc, lw["w_out"])
    return _merge(x, *pending_x)
```
